```python
import functools
import jax, jax.numpy as jnp
from jax import lax
import numpy as np

D_MODEL = 1024
BATCH = 2
SEQ = 8192
DEPTH = 2
DEC_BATCH = 128
DEC_SEQ = 4
PAST_LEN = 2048
PAGE_SIZE = 128

GLA_HEADS = 4
GLA_DK = 64
GLA_DV = 128
GLA_LR = 16
GLA_TAU = 16.0
GLA_CHUNK = 16
FOX_HEADS = 8
FOX_HD = 64
FOX_BLOCK = 128
SC_WIDTH = 512
CONV_K = 3
D_FF = 2816
N_BRANCH = 3
BRANCH_W = 512
EPS = 1e-6
IN_SIZES = (GLA_HEADS * GLA_DK, GLA_HEADS * GLA_DK, GLA_HEADS * GLA_DV, GLA_LR, GLA_HEADS * GLA_DV,
            FOX_HEADS * FOX_HD, FOX_HEADS * FOX_HD, FOX_HEADS * FOX_HD, FOX_HEADS,
            SC_WIDTH, SC_WIDTH, SC_WIDTH, N_BRANCH * D_MODEL)
IN_COLS = sum(IN_SIZES)

kernel_name = 'hybrid_gla_fox_shortconv_convffn_step'


def rmsnorm(x, g):
    xf = x.astype(jnp.float32)
    y = xf * lax.rsqrt(jnp.mean(xf * xf, axis=-1, keepdims=True) + EPS)
    return (y * g.astype(jnp.float32)).astype(x.dtype)


def causal_dwconv(u, prev, w):
    L = u.shape[1]
    ext = jnp.concatenate([prev.astype(u.dtype), u], axis=1)
    out = sum(ext[:, i:i + L] * w[i] for i in range(CONV_K))
    return out, ext[:, -(CONV_K - 1):]


def gla_chunked(q, k, v, log_a, s0, chunk):
    B, L, H, DK = q.shape
    DV = v.shape[-1]
    n = L // chunk

    def blocks(t):
        return t.astype(jnp.float32).reshape(B, n, chunk, H, t.shape[-1]).transpose(1, 0, 3, 2, 4)

    qc, kc, vc, ac = blocks(q), blocks(k), blocks(v), blocks(log_a)
    b = jnp.cumsum(ac, axis=3)
    b_last = b[:, :, :, -1:, :]
    causal = jnp.tril(jnp.ones((chunk, chunk), dtype=bool))
    rel = jnp.where(causal[:, :, None], b[..., :, None, :] - b[..., None, :, :], -jnp.inf)
    scores = jnp.einsum('nbhtd,nbhsd,nbhtsd->nbhts', qc, kc, jnp.exp(rel))
    o_intra = jnp.einsum('nbhts,nbhsv->nbhtv', scores, vc)
    q_dec = qc * jnp.exp(b)
    k_dec = kc * jnp.exp(b_last - b)
    a_last = jnp.exp(b_last[:, :, :, 0, :])

    def step(S, inp):
        qd, kd, vv, al = inp
        o = jnp.einsum('bhtd,bhdv->bhtv', qd, S)
        S = al[..., None] * S + jnp.einsum('bhsd,bhsv->bhdv', kd, vv)
        return S, o

    s_fin, o_inter = lax.scan(step, s0.astype(jnp.float32), (q_dec, k_dec, vc, a_last))
    o = (o_intra + o_inter).transpose(1, 0, 3, 2, 4).reshape(B, L, H, DV)
    return o.astype(v.dtype), s_fin


def fox_attend(q, k, v, c_q, c_k, q_pos, k_pos):
    s = jnp.einsum('bqhd,bkhd->bhqk', q, k).astype(jnp.float32) * (FOX_HD ** -0.5)
    bias = jnp.transpose(c_q, (0, 2, 1))[..., :, None] - jnp.transpose(c_k, (0, 2, 1))[..., None, :]
    logits = jnp.where(k_pos[None, :] <= q_pos[:, None], s + bias, -jnp.inf)
    p = jax.nn.softmax(logits, axis=-1)
    return jnp.einsum('bhqk,bkhd->bqhd', p.astype(v.dtype), v)


def fox_prompt(q, k, v, logf):
    B, L, H, D = q.shape
    c = jnp.cumsum(logf.astype(jnp.float32), axis=1)
    nb = L // FOX_BLOCK
    qb = q.reshape(B, nb, FOX_BLOCK, H, D).transpose(1, 0, 2, 3, 4)
    cb = c.reshape(B, nb, FOX_BLOCK, H).transpose(1, 0, 2, 3)
    pb = jnp.arange(L).reshape(nb, FOX_BLOCK)
    k_pos = jnp.arange(L)
    out = lax.map(lambda a: fox_attend(a[0], k, v, a[1], c, a[2], k_pos), (qb, cb, pb))
    return out.transpose(1, 0, 2, 3, 4).reshape(B, L, H * D)


def fox_sample(k_past, v_past, logf_past, q, k, v, logf):
    B, L, H, D = q.shape
    P = k_past.shape[1]
    c_past = jnp.cumsum(logf_past.astype(jnp.float32), axis=1)
    c_new = c_past[:, -1:] + jnp.cumsum(logf.astype(jnp.float32), axis=1)
    k_all = jnp.concatenate([k_past.astype(k.dtype), k], axis=1)
    v_all = jnp.concatenate([v_past.astype(v.dtype), v], axis=1)
    c_all = jnp.concatenate([c_past, c_new], axis=1)
    out = fox_attend(q, k_all, v_all, c_new, c_all, P + jnp.arange(L), jnp.arange(P + L))
    return out.reshape(B, L, H * D)


def mixer_inputs(xn, w_in, w_alpha2, b_alpha, b_forget):
    Bt, L, _ = xn.shape
    z = xn @ w_in
    parts = []
    off = 0
    for size in IN_SIZES:
        parts.append(z[..., off:off + size])
        off += size
    gq, gk, gv, glr, gg, fq, fk, fv, ff, sc_b, sc_c, sc_h, gates = parts
    gla_q = gq.reshape(Bt, L, GLA_HEADS, GLA_DK) * (GLA_DK ** -0.5)
    gla_k = gk.reshape(Bt, L, GLA_HEADS, GLA_DK)
    gla_v = gv.reshape(Bt, L, GLA_HEADS, GLA_DV)
    log_a = jax.nn.log_sigmoid((glr @ w_alpha2 + b_alpha).astype(jnp.float32)) / GLA_TAU
    log_a = log_a.reshape(Bt, L, GLA_HEADS, GLA_DK)
    fox_q = fq.reshape(Bt, L, FOX_HEADS, FOX_HD)
    fox_k = fk.reshape(Bt, L, FOX_HEADS, FOX_HD)
    fox_v = fv.reshape(Bt, L, FOX_HEADS, FOX_HD)
    fox_logf = jax.nn.log_sigmoid((ff + b_forget).astype(jnp.float32))
    return gla_q, gla_k, gla_v, log_a, gg, fox_q, fox_k, fox_v, fox_logf, sc_b, sc_c * sc_h, gates


def layer(h, fox_fn, gla_s0, sc_prev, ffn_prev, norm1_g, w_in, w_alpha2, b_alpha, b_forget, gla_norm_g,
          sconv_w, w_branch, w_out, norm2_g, w_up, ffn_conv_w, ffn_conv_b, w_down):
    Bt, L, _ = h.shape
    xn = rmsnorm(h, norm1_g)
    (gla_q, gla_k, gla_v, log_a, gla_gate, fox_q, fox_k, fox_v, fox_logf,
     sc_b, sc_u, gates) = mixer_inputs(xn, w_in, w_alpha2, b_alpha, b_forget)
    chunk = GLA_CHUNK if L % GLA_CHUNK == 0 else L
    o_gla, gla_state = gla_chunked(gla_q, gla_k, gla_v, log_a, gla_s0, chunk)
    y_gla = rmsnorm(o_gla, gla_norm_g).reshape(Bt, L, GLA_HEADS * GLA_DV) * jax.nn.silu(gla_gate)
    y_fox = fox_fn(fox_q, fox_k, fox_v, fox_logf)
    z_sc, sc_state = causal_dwconv(sc_u, sc_prev, sconv_w)
    y_sc = sc_b * z_sc
    branches = jnp.stack([y_gla, y_fox.astype(y_gla.dtype), y_sc.astype(y_gla.dtype)], axis=-2)
    proj = jnp.einsum('blnc,ncd->blnd', branches, w_branch)
    g = jax.nn.sigmoid(gates.reshape(Bt, L, N_BRANCH, D_MODEL).astype(jnp.float32)).astype(proj.dtype)
    h = h + jnp.sum(g * proj, axis=-2) @ w_out
    xn2 = rmsnorm(h, norm2_g)
    up = xn2 @ w_up
    a, gate = up[..., :D_FF], up[..., D_FF:]
    a_c, ffn_state = causal_dwconv(a, ffn_prev, ffn_conv_w)
    h = h + (jax.nn.gelu(a_c + ffn_conv_b, approximate=False) * gate) @ w_down
    return h, fox_k, fox_v, fox_logf, gla_state, sc_state, ffn_state


def setup_inputs(seed: int = 0) -> dict:
    key = jax.random.key(seed)
    ks = jax.random.split(key, 24)
    n_pages = PAST_LEN // PAGE_SIZE
    n_pool = (5 * DEC_BATCH * n_pages) // 4

    def nrm(k, shape, scale=1.0):
        return jax.random.normal(k, shape, jnp.float32) * scale

    page_table = jax.random.permutation(ks[8], n_pool)[: DEC_BATCH * n_pages]
    page_table = page_table.reshape(DEC_BATCH, n_pages).astype(jnp.int32)
    return {
        'x_prompt': nrm(ks[0], (BATCH, SEQ, D_MODEL)),
        'x_sample': nrm(ks[1], (DEC_BATCH, DEC_SEQ, D_MODEL)),
        'cache_fox_k': nrm(ks[2], (DEPTH, n_pool, PAGE_SIZE, FOX_HEADS, FOX_HD)),
        'cache_fox_v': nrm(ks[3], (DEPTH, n_pool, PAGE_SIZE, FOX_HEADS, FOX_HD)),
        'cache_fox_logf': jax.nn.log_sigmoid(2.0 + nrm(ks[4], (DEPTH, n_pool, PAGE_SIZE, FOX_HEADS), 0.5)),
        'state_gla': nrm(ks[5], (DEPTH, DEC_BATCH, GLA_HEADS, GLA_DK, GLA_DV), 0.3),
        'state_sconv': nrm(ks[6], (DEPTH, DEC_BATCH, CONV_K - 1, SC_WIDTH)),
        'state_ffn_conv': nrm(ks[7], (DEPTH, DEC_BATCH, CONV_K - 1, D_FF)),
        'page_table': page_table,
        'norm1_g': 1.0 + nrm(ks[9], (DEPTH, D_MODEL), 0.01),
        'w_in': nrm(ks[10], (DEPTH, D_MODEL, IN_COLS), D_MODEL ** -0.5),
        'w_alpha2': nrm(ks[11], (DEPTH, GLA_LR, GLA_HEADS * GLA_DK), GLA_LR ** -0.5),
        'b_alpha': nrm(ks[12], (DEPTH, GLA_HEADS * GLA_DK), 0.01),
        'b_forget': 2.0 + nrm(ks[13], (DEPTH, FOX_HEADS), 0.5),
        'gla_norm_g': 1.0 + nrm(ks[14], (DEPTH, GLA_DV), 0.01),
        'sconv_w': nrm(ks[15], (DEPTH, CONV_K, SC_WIDTH), CONV_K ** -0.5),
        'w_branch': nrm(ks[16], (DEPTH, N_BRANCH, BRANCH_W, D_MODEL), BRANCH_W ** -0.5),
        'w_out': nrm(ks[17], (DEPTH, D_MODEL, D_MODEL), D_MODEL ** -0.5),
        'norm2_g': 1.0 + nrm(ks[18], (DEPTH, D_MODEL), 0.01),
        'w_up': nrm(ks[19], (DEPTH, D_MODEL, 2 * D_FF), D_MODEL ** -0.5),
        'ffn_conv_w': nrm(ks[20], (DEPTH, CONV_K, D_FF), CONV_K ** -0.5),
        'ffn_conv_b': nrm(ks[21], (DEPTH, D_FF), 0.01),
        'w_down': nrm(ks[22], (DEPTH, D_FF, D_MODEL), D_FF ** -0.5),
        'final_norm_g': 1.0 + nrm(ks[23], (D_MODEL,), 0.01),
    }


def reference(x_prompt, x_sample, cache_fox_k, cache_fox_v, cache_fox_logf, state_gla, state_sconv,
              state_ffn_conv, page_table, norm1_g, w_in, w_alpha2, b_alpha, b_forget, gla_norm_g, sconv_w,
              w_branch, w_out, norm2_g, w_up, ffn_conv_w, ffn_conv_b, w_down, final_norm_g):
    bp = x_prompt.shape[0]
    bd = x_sample.shape[0]
    n_pages = page_table.shape[1]
    past = n_pages * cache_fox_k.shape[2]
    hp, hs = x_prompt, x_sample
    prompt_states, sample_states = [], []
    for l in range(DEPTH):
        w = (norm1_g[l], w_in[l], w_alpha2[l], b_alpha[l], b_forget[l], gla_norm_g[l], sconv_w[l],
             w_branch[l], w_out[l], norm2_g[l], w_up[l], ffn_conv_w[l], ffn_conv_b[l], w_down[l])
        hp, *st_p = layer(hp, fox_prompt,
                          jnp.zeros((bp, GLA_HEADS, GLA_DK, GLA_DV), jnp.float32),
                          jnp.zeros((bp, CONV_K - 1, SC_WIDTH), x_prompt.dtype),
                          jnp.zeros((bp, CONV_K - 1, D_FF), x_prompt.dtype), *w)
        k_past = cache_fox_k[l][page_table].reshape(bd, past, FOX_HEADS, FOX_HD)
        v_past = cache_fox_v[l][page_table].reshape(bd, past, FOX_HEADS, FOX_HD)
        lf_past = cache_fox_logf[l][page_table].reshape(bd, past, FOX_HEADS)
        fox_fn = functools.partial(fox_sample, k_past, v_past, lf_past)
        hs, *st_s = layer(hs, fox_fn, state_gla[l], state_sconv[l], state_ffn_conv[l], *w)
        prompt_states.append(st_p)
        sample_states.append(st_s)
    p_fox_k, p_fox_v, p_fox_logf, p_gla, p_sconv, p_ffn = [jnp.stack(t) for t in zip(*prompt_states)]
    s_fox_k, s_fox_v, s_fox_logf, s_gla, s_sconv, s_ffn = [jnp.stack(t) for t in zip(*sample_states)]
    y_prompt = rmsnorm(hp, final_norm_g)
    y_sample = rmsnorm(hs, final_norm_g)
    return (y_prompt, y_sample, p_fox_k, p_fox_v, p_fox_logf, p_gla, p_sconv, p_ffn,
            s_fox_k, s_fox_v, s_fox_logf, s_gla, s_sconv, s_ffn)
```

```python
import functools

import numpy as np
import jax
import jax.numpy as jnp
from jax import lax
from jax.experimental import pallas as pl
from jax.experimental.pallas import tpu as pltpu

F32 = jnp.float32
BF16 = jnp.bfloat16

D_MODEL = 1024
GLA_HEADS = 4
GLA_DK = 64
GLA_DV = 128
GLA_LR = 16
GLA_TAU = 16.0
FOX_HEADS = 8
FOX_HD = 64
SC_WIDTH = 512
CONV_K = 3
D_FF = 2816
N_BRANCH = 3
EPS = 1e-6

LANES = 128
SUBLANES = 8
VMEM_BYTES = 64 << 20

Z_GATES = 0
Z_GQK = 3072
Z_GV = 3584
Z_GG = 4096
Z_FQ = 4608
Z_FK = 5120
Z_FV = 5632
Z_SC = 6144
Z_COLS = 7680
IN_BN = 512
SMALL_COLS = 128

GLA_CHUNK = 128
NEG = -1e30


def _cparams(sem, vmem_mb):
    return pltpu.CompilerParams(dimension_semantics=sem, vmem_limit_bytes=min(vmem_mb << 20, VMEM_BYTES - (4 << 20)))


def _log_sigmoid(x):
    return jnp.minimum(x, 0.0) - jnp.log1p(jnp.exp(-jnp.abs(x)))


def _dot(a, b):
    return jnp.dot(a, b, preferred_element_type=F32)


def _dot_nt(a, b):
    return lax.dot_general(a, b, (((1,), (1,)), ((), ())), preferred_element_type=F32)


def _dot_tn(a, b):
    return lax.dot_general(a, b, (((0,), (0,)), ((), ())), preferred_element_type=F32)


def _rms(x, g):
    return x * lax.rsqrt(jnp.mean(x * x, axis=-1, keepdims=True) + EPS) * g


def _lane_prefix_sum(c, lane):
    sh = 1
    while sh < LANES:
        c = c + jnp.where(lane >= sh, pltpu.roll(c, sh, axis=1), 0.0)
        sh *= 2
    return c


def _in_proj_body(x_ref, g_ref, w_ref, ws_ref, wft_ref, z_ref, fk_ref, fv_ref, small_ref, fft_ref, xn_scr):
    j = pl.program_id(1)

    @pl.when(j == 0)
    def _():
        xb = _rms(x_ref[...], g_ref[...]).astype(BF16)
        xn_scr[...] = xb
        small_ref[...] = _dot(xb, ws_ref[...])
        fft_ref[...] = _dot_nt(wft_ref[...], xb)

    zt = _dot(xn_scr[...], w_ref[...])
    z_ref[...] = zt.astype(BF16)

    @pl.when(j == Z_FK // IN_BN)
    def _():
        fk_ref[...] = zt

    @pl.when(j == Z_FV // IN_BN)
    def _():
        fv_ref[...] = zt


def _in_proj(x, g, w, ws, wft, bm):
    T = x.shape[0]
    grid = (T // bm, Z_COLS // IN_BN)
    return pl.pallas_call(
        _in_proj_body,
        grid=grid,
        in_specs=[
            pl.BlockSpec((bm, D_MODEL), lambda i, j: (i, 0)),
            pl.BlockSpec((1, D_MODEL), lambda i, j: (0, 0)),
            pl.BlockSpec((D_MODEL, IN_BN), lambda i, j: (0, j)),
            pl.BlockSpec((D_MODEL, SMALL_COLS), lambda i, j: (0, 0)),
            pl.BlockSpec((FOX_HEADS, D_MODEL), lambda i, j: (0, 0)),
        ],
        out_specs=[
            pl.BlockSpec((bm, IN_BN), lambda i, j: (i, j)),
            pl.BlockSpec((bm, FOX_HEADS * FOX_HD), lambda i, j: (i, 0)),
            pl.BlockSpec((bm, FOX_HEADS * FOX_HD), lambda i, j: (i, 0)),
            pl.BlockSpec((bm, SMALL_COLS), lambda i, j: (i, 0)),
            pl.BlockSpec((FOX_HEADS, bm), lambda i, j: (0, i)),
        ],
        out_shape=[
            jax.ShapeDtypeStruct((T, Z_COLS), BF16),
            jax.ShapeDtypeStruct((T, FOX_HEADS * FOX_HD), F32),
            jax.ShapeDtypeStruct((T, FOX_HEADS * FOX_HD), F32),
            jax.ShapeDtypeStruct((T, SMALL_COLS), F32),
            jax.ShapeDtypeStruct((FOX_HEADS, T), F32),
        ],
        scratch_shapes=[pltpu.VMEM((bm, D_MODEL), BF16)],
        compiler_params=_cparams(("arbitrary", "arbitrary"), 40),
    )(x, g, w, ws, wft)


def _gla_consts(C):
    t = np.arange(C)
    r = t[None, :]
    mats = [r <= t[:, None], r > t[:, None]]
    masks = []
    n = C // 2
    while n >= 1:
        pair = t // (2 * n)
        second = (t // n) % 2 == 1
        mid = pair * 2 * n + n
        mats.append(np.where(second[:, None], (r >= mid[:, None]) & (r <= t[:, None]),
                             (r > t[:, None]) & (r < mid[:, None])))
        masks.append(second[:, None] & ~second[None, :] & (pair[:, None] == pair[None, :]))
        n //= 2
    masks.append(np.eye(C, dtype=bool))
    return (jnp.asarray(np.concatenate(mats, 0).astype(np.float32), BF16),
            jnp.asarray(np.stack(masks).astype(np.float32)))


def _gla_prompt_body(qk_ref, v_ref, gg_ref, small_ref, fft_ref, wa_ref, ba_ref, bf_ref, gn_ref, wall_ref,
                     masks_ref, y_ref, st_ref, logf_ref, c_ref, s_scr, c_scr, *, C, nlev):
    i = pl.program_id(1)

    @pl.when(i == 0)
    def _():
        s_scr[...] = jnp.zeros_like(s_scr)
        c_scr[...] = jnp.zeros_like(c_scr)

    lf = _log_sigmoid(fft_ref[...] + bf_ref[...])
    logf_ref[...] = lf
    lane = lax.broadcasted_iota(jnp.int32, lf.shape, 1)
    c = _lane_prefix_sum(lf, lane) + c_scr[...]
    c_ref[...] = c
    c_scr[...] = jnp.broadcast_to(c[:, C - 1:C], c_scr.shape)

    la = _log_sigmoid(_dot(small_ref[...].astype(BF16), wa_ref[...]) + ba_ref[...]) * (1.0 / GLA_TAU)
    hi = la.astype(BF16)
    lo = (la - hi.astype(F32)).astype(BF16)
    wall = wall_ref[...]
    P = jnp.exp(_dot(wall, hi) + _dot(wall, lo))

    W = GLA_HEADS * GLA_DK
    qk = qk_ref[...].astype(F32)
    q = qk[:, :W] * (GLA_DK ** -0.5)
    k = qk[:, W:]
    qd = (q * P[0:C]).astype(BF16)
    kd = (k * P[C:2 * C]).astype(BF16)
    a_last = P[C - 1:C, :]
    row = lax.broadcasted_iota(jnp.int32, (C, 1), 0)
    X = []
    n = C // 2
    for l in range(nlev):
        X.append((jnp.where((row & n) != 0, q, k) * P[(2 + l) * C:(3 + l) * C]).astype(BF16))
        n //= 2
    qb = q.astype(BF16)
    kb = k.astype(BF16)

    lane_p = lax.broadcasted_iota(jnp.int32, (1, LANES), 1)
    for p in range(GLA_HEADS // 2):
        sl = slice(LANES * p, LANES * (p + 1))
        ST = s_scr[p]
        STb = ST.astype(BF16)
        U = []
        for hh in range(2):
            h = 2 * p + hh
            hm = (lane_p >= GLA_DK) == bool(hh)

            def msk(a):
                return jnp.where(hm, a, jnp.zeros_like(a))

            sc = masks_ref[nlev] * _dot_nt(msk(qb[:, sl]), kb[:, sl])
            for l in range(nlev):
                xl = X[l][:, sl]
                sc = sc + masks_ref[l] * _dot_nt(msk(xl), xl)
            vh = v_ref[:, GLA_DV * h:GLA_DV * (h + 1)]
            o = _dot(sc.astype(BF16), vh) + _dot_nt(msk(qd[:, sl]), STb)
            U.append(_dot_tn(vh, kd[:, sl]))
            g = gg_ref[:, GLA_DV * h:GLA_DV * (h + 1)].astype(F32)
            y_ref[:, GLA_DV * h:GLA_DV * (h + 1)] = (_rms(o, gn_ref[...]) * (g * jax.nn.sigmoid(g))).astype(BF16)
        s_scr[p] = ST * a_last[:, sl] + jnp.where(lane_p < GLA_DK, U[0], U[1])

    @pl.when(i == pl.num_programs(1) - 1)
    def _():
        st_ref[0] = s_scr[...]


def _gla_prompt(z, small, fft, wa, ba, bf, gn, B, L):
    C = GLA_CHUNK
    nL = L // C
    wall, masks = _gla_consts(C)
    nlev = masks.shape[0] - 1
    T = B * L
    W = GLA_HEADS * GLA_DK
    row = lambda b, i: b * nL + i
    return pl.pallas_call(
        functools.partial(_gla_prompt_body, C=C, nlev=nlev),
        grid=(B, nL),
        in_specs=[
            pl.BlockSpec((C, 2 * W), lambda b, i: (row(b, i), Z_GQK // (2 * W))),
            pl.BlockSpec((C, GLA_HEADS * GLA_DV), lambda b, i: (row(b, i), Z_GV // (GLA_HEADS * GLA_DV))),
            pl.BlockSpec((C, GLA_HEADS * GLA_DV), lambda b, i: (row(b, i), Z_GG // (GLA_HEADS * GLA_DV))),
            pl.BlockSpec((C, SMALL_COLS), lambda b, i: (row(b, i), 0)),
            pl.BlockSpec((FOX_HEADS, C), lambda b, i: (0, row(b, i))),
            pl.BlockSpec((SMALL_COLS, W), lambda b, i: (0, 0)),
            pl.BlockSpec((1, W), lambda b, i: (0, 0)),
            pl.BlockSpec((FOX_HEADS, 1), lambda b, i: (0, 0)),
            pl.BlockSpec((1, GLA_DV), lambda b, i: (0, 0)),
            pl.BlockSpec(wall.shape, lambda b, i: (0, 0)),
            pl.BlockSpec(masks.shape, lambda b, i: (0, 0, 0)),
        ],
        out_specs=[
            pl.BlockSpec((C, GLA_HEADS * GLA_DV), lambda b, i: (row(b, i), 0)),
            pl.BlockSpec((1, GLA_HEADS // 2, GLA_DV, 2 * GLA_DK), lambda b, i: (b, 0, 0, 0)),
            pl.BlockSpec((FOX_HEADS, C), lambda b, i: (0, row(b, i))),
            pl.BlockSpec((FOX_HEADS, C), lambda b, i: (0, row(b, i))),
        ],
        out_shape=[
            jax.ShapeDtypeStruct((T, GLA_HEADS * GLA_DV), BF16),
            jax.ShapeDtypeStruct((B, GLA_HEADS // 2, GLA_DV, 2 * GLA_DK), F32),
            jax.ShapeDtypeStruct((FOX_HEADS, T), F32),
            jax.ShapeDtypeStruct((FOX_HEADS, T), F32),
        ],
        scratch_shapes=[pltpu.VMEM((GLA_HEADS // 2, GLA_DV, 2 * GLA_DK), F32),
                        pltpu.VMEM((FOX_HEADS, LANES), F32)],
        compiler_params=_cparams(("arbitrary", "arbitrary"), 32),
    )(z, z, z, small, fft, wa, ba, bf, gn, wall, masks)


def _gla_sample_body(q_ref, k_ref, v_ref, gg_ref, small_ref, wa_ref, ba_ref, gn_ref, s0_ref, y_ref, s1_ref,
                     *, nb, nt):
    la = _log_sigmoid(_dot(small_ref[...].astype(BF16), wa_ref[0]) + ba_ref[0]) * (1.0 / GLA_TAU)
    q = q_ref[0].astype(F32) * (GLA_DK ** -0.5)
    k = k_ref[0].astype(F32)
    rows = [slice(t * nb, (t + 1) * nb) for t in range(nt)]
    b = []
    for t in range(nt):
        b.append(la[rows[t]] if t == 0 else b[-1] + la[rows[t]])
    qt = [q[r] for r in rows]
    kt = [k[r] for r in rows]
    qd = [qt[t] * jnp.exp(b[t]) for t in range(nt)]
    kd = [kt[t] * jnp.exp(b[nt - 1] - b[t]) for t in range(nt)]
    a_last = jnp.exp(b[nt - 1])

    for d in range(GLA_DK):
        cs = slice(d * GLA_DV, (d + 1) * GLA_DV)
        acc = a_last[:, d:d + 1] * s0_ref[:, cs]
        for t in range(nt):
            acc = acc + kd[t][:, d:d + 1] * v_ref[rows[t], :].astype(F32)
        s1_ref[:, cs] = acc

    for t in range(nt):
        o = jnp.zeros((nb, GLA_DV), F32)
        for s in range(t + 1):
            w = jnp.sum(qt[t] * kt[s] * jnp.exp(b[t] - b[s]), axis=-1, keepdims=True)
            o = o + w * v_ref[rows[s], :].astype(F32)
        for d in range(GLA_DK):
            o = o + qd[t][:, d:d + 1] * s0_ref[:, d * GLA_DV:(d + 1) * GLA_DV]
        g = gg_ref[rows[t], :].astype(F32)
        y_ref[rows[t], :] = (_rms(o, gn_ref[...]) * (g * jax.nn.sigmoid(g))).astype(BF16)


def _gla_sample(qh, kh, z, small, wa_h, ba_h, gn, s0, nb, nt):
    T = nb * nt
    slab = GLA_DK * GLA_DV
    return pl.pallas_call(
        functools.partial(_gla_sample_body, nb=nb, nt=nt),
        grid=(GLA_HEADS,),
        in_specs=[
            pl.BlockSpec((1, T, GLA_DK), lambda h: (h, 0, 0)),
            pl.BlockSpec((1, T, GLA_DK), lambda h: (h, 0, 0)),
            pl.BlockSpec((T, GLA_DV), lambda h: (0, Z_GV // GLA_DV + h)),
            pl.BlockSpec((T, GLA_DV), lambda h: (0, Z_GG // GLA_DV + h)),
            pl.BlockSpec((T, SMALL_COLS), lambda h: (0, 0)),
            pl.BlockSpec((1, SMALL_COLS, GLA_DK), lambda h: (h, 0, 0)),
            pl.BlockSpec((1, 1, GLA_DK), lambda h: (h, 0, 0)),
            pl.BlockSpec((1, GLA_DV), lambda h: (0, 0)),
            pl.BlockSpec((nb, slab), lambda h: (0, h)),
        ],
        out_specs=[
            pl.BlockSpec((T, GLA_DV), lambda h: (0, h)),
            pl.BlockSpec((nb, slab), lambda h: (0, h)),
        ],
        out_shape=[
            jax.ShapeDtypeStruct((T, GLA_HEADS * GLA_DV), BF16),
            jax.ShapeDtypeStruct((nb, GLA_HEADS * slab), F32),
        ],
        compiler_params=_cparams(("arbitrary",), 40),
    )(qh, kh, z, z, small, wa_h, ba_h, gn, s0)


def _fox_prompt_body(q_ref, k_ref, v_ref, cq_ref, ck_ref, o_ref, *, blk):
    qi = pl.program_id(2)
    q2 = q_ref[...]
    lane = lax.broadcasted_iota(jnp.int32, (1, LANES), 1)
    causal = (lax.broadcasted_iota(jnp.int32, (blk, blk), 1) <= lax.broadcasted_iota(jnp.int32, (blk, blk), 0))
    outs = []
    for hh in range(2):
        hm = (lane >= FOX_HD) == bool(hh)
        qm = jnp.where(hm, q2, jnp.zeros_like(q2)) * (FOX_HD ** -0.5)
        cqh = cq_ref[0][:, hh:hh + 1]

        def block(j, carry, masked):
            m, l, acc = carry
            start = pl.multiple_of(j * blk, blk)
            kb = k_ref[pl.ds(start, blk), :]
            vb = v_ref[pl.ds(start, blk), :]
            ckh = ck_ref[0, hh:hh + 1, pl.ds(start, blk)]
            s = _dot_nt(qm, kb) + (cqh - ckh)
            if masked:
                s = jnp.where(causal, s, NEG)
            m_new = jnp.maximum(m, jnp.max(s, axis=-1, keepdims=True))
            p = jnp.exp(s - m_new)
            alpha = jnp.exp(m - m_new)
            l = alpha * l + jnp.sum(p, axis=-1, keepdims=True)
            acc = alpha * acc + _dot(p.astype(BF16), vb)
            return m_new, l, acc

        init = (jnp.full((blk, 1), NEG, F32), jnp.zeros((blk, 1), F32), jnp.zeros((blk, LANES), F32))
        carry = lax.fori_loop(0, qi, lambda j, c: block(j, c, False), init)
        m, l, acc = block(qi, carry, True)
        outs.append(acc / l)
    o_ref[...] = jnp.where(lane < FOX_HD, outs[0], outs[1]).astype(BF16)


def _fox_prompt(z, c4, ct4, B, L, blk):
    nq = L // blk
    T = B * L
    npair = FOX_HEADS // 2
    return pl.pallas_call(
        functools.partial(_fox_prompt_body, blk=blk),
        grid=(B, npair, nq),
        in_specs=[
            pl.BlockSpec((blk, LANES), lambda b, p, i: (b * nq + i, Z_FQ // LANES + p)),
            pl.BlockSpec((L, LANES), lambda b, p, i: (b, Z_FK // LANES + p)),
            pl.BlockSpec((L, LANES), lambda b, p, i: (b, Z_FV // LANES + p)),
            pl.BlockSpec((1, blk, 2), lambda b, p, i: (p, b * nq + i, 0)),
            pl.BlockSpec((1, 2, L), lambda b, p, i: (p, 0, b)),
        ],
        out_specs=pl.BlockSpec((blk, LANES), lambda b, p, i: (b * nq + i, p)),
        out_shape=jax.ShapeDtypeStruct((T, FOX_HEADS * FOX_HD), BF16),
        compiler_params=_cparams(("arbitrary", "arbitrary", "arbitrary"), 40),
    )(z, z, z, c4, ct4)


def _fox_sample_body(pt_ref, q_ref, kn_ref, vn_ref, fft_ref, bf_ref, *rest, n_pages, nt):
    del pt_ref
    k_refs = rest[:n_pages]
    v_refs = rest[n_pages:2 * n_pages]
    lf_refs = rest[2 * n_pages:3 * n_pages]
    o_ref, lfo_ref = rest[3 * n_pages:]
    W = FOX_HEADS * FOX_HD
    H = FOX_HEADS

    q = q_ref[0] * (FOX_HD ** -0.5)
    head_shift = FOX_HD.bit_length() - 1
    hmask = (lax.broadcasted_iota(jnp.int32, (H, W), 1) >> head_shift) == lax.broadcasted_iota(jnp.int32, (H, W), 0)
    qbd = jnp.concatenate([jnp.where(hmask, jnp.broadcast_to(q[t:t + 1], (H, W)), 0.0) for t in range(nt)], axis=0)
    qbd_b = qbd.astype(BF16)

    lane = lax.broadcasted_iota(jnp.int32, (H, LANES), 1)
    carry = jnp.zeros((H, 1), F32)
    ck = []
    for j in range(n_pages):
        c = _lane_prefix_sum(lf_refs[j][...], lane) + carry
        carry = c[:, LANES - 1:LANES]
        ck.append(c)
    lfn = _log_sigmoid(fft_ref[0] + bf_ref[...])
    lfo_ref[0] = lfn
    lane_t = lax.broadcasted_iota(jnp.int32, (H, nt), 1)
    cn = carry
    for j in range(nt):
        cn = cn + jnp.where(lane_t >= j, lfn[:, j:j + 1], 0.0)
    cq = jnp.concatenate([cn[:, t:t + 1] for t in range(nt)], axis=0)

    ck_all = jnp.concatenate(ck, axis=1)
    s_past = jnp.concatenate([_dot_nt(qbd_b, k_refs[j][...].astype(BF16)) for j in range(n_pages)], axis=1)
    s_past = s_past + (cq - jnp.concatenate([ck_all] * nt, axis=0))

    kn = kn_ref[0]
    vn = vn_ref[0]
    trow = lax.broadcasted_iota(jnp.int32, (nt * H, 1), 0) >> (H.bit_length() - 1)
    s_new = []
    for tp in range(nt):
        ckn = jnp.concatenate([cn[:, tp:tp + 1]] * nt, axis=0)
        sn = jnp.sum(qbd * kn[tp:tp + 1], axis=-1, keepdims=True) + (cq - ckn)
        s_new.append(jnp.where(trow >= tp, sn, NEG))
    m = jnp.max(s_past, axis=-1, keepdims=True)
    for sn in s_new:
        m = jnp.maximum(m, sn)
    p_past = jnp.exp(s_past - m)
    den = jnp.sum(p_past, axis=-1, keepdims=True)
    out = jnp.zeros((nt * H, W), F32)
    for j in range(n_pages):
        out = out + _dot(p_past[:, j * LANES:(j + 1) * LANES].astype(BF16), v_refs[j][...].astype(BF16))
    for tp in range(nt):
        pn = jnp.exp(s_new[tp] - m)
        den = den + pn
        out = out + pn * vn[tp:tp + 1]
    out = out / den
    for t in range(nt):
        o_ref[0, t:t + 1, :] = jnp.sum(jnp.where(hmask, out[t * H:(t + 1) * H], 0.0), axis=0, keepdims=True)


def _fox_sample(pt_flat, q_s, kn_s, vn_s, fft_s, bf, cache_k, cache_v, cache_lft, layer, nb, nt, n_pages, page):
    W = FOX_HEADS * FOX_HD

    def page_spec(shape, j):
        return pl.BlockSpec((None, None) + shape, lambda s, pt: (layer, pt[s * n_pages + j], 0, 0))

    seq3 = lambda s, pt: (s, 0, 0)
    in_specs = [
        pl.BlockSpec((1, nt, W), seq3),
        pl.BlockSpec((1, nt, W), seq3),
        pl.BlockSpec((1, nt, W), seq3),
        pl.BlockSpec((1, FOX_HEADS, nt), seq3),
        pl.BlockSpec((FOX_HEADS, 1), lambda s, pt: (0, 0)),
    ]
    in_specs += [page_spec((page, W), j) for j in range(n_pages)]
    in_specs += [page_spec((page, W), j) for j in range(n_pages)]
    in_specs += [page_spec((FOX_HEADS, page), j) for j in range(n_pages)]
    grid_spec = pltpu.PrefetchScalarGridSpec(
        num_scalar_prefetch=1,
        grid=(nb,),
        in_specs=in_specs,
        out_specs=[pl.BlockSpec((1, nt, W), seq3), pl.BlockSpec((1, FOX_HEADS, nt), seq3)],
    )
    return pl.pallas_call(
        functools.partial(_fox_sample_body, n_pages=n_pages, nt=nt),
        grid_spec=grid_spec,
        out_shape=[jax.ShapeDtypeStruct((nb, nt, W), F32), jax.ShapeDtypeStruct((nb, FOX_HEADS, nt), F32)],
        compiler_params=_cparams(("arbitrary",), 48),
    )(pt_flat, q_s, kn_s, vn_s, fft_s, bf, *([cache_k] * n_pages), *([cache_v] * n_pages), *([cache_lft] * n_pages))


def _causal_conv(ext_scr, u, cw_ref, H, shift):
    bm = u.shape[0]
    ext_scr[H:H + bm, :] = u
    out = cw_ref[CONV_K - 1:CONV_K, :] * u
    for i in range(1, CONV_K):
        lo = H - i * shift
        out = out + cw_ref[CONV_K - 1 - i:CONV_K - i, :] * ext_scr[lo:lo + bm, :]
    return out


def _mix_body(*refs, H, shift, tiles_per_seq, has_state):
    if has_state:
        h_ref, gates_ref, sc_ref, yg_ref, yf_ref, cw_ref, wb_ref, wo_ref, halo_ref, hout_ref, tail_ref, ext_scr = refs
    else:
        h_ref, gates_ref, sc_ref, yg_ref, yf_ref, cw_ref, wb_ref, wo_ref, hout_ref, tail_ref, ext_scr = refs
    bm = h_ref.shape[0]
    if has_state:
        ext_scr[0:H, :] = halo_ref[...]
    else:
        seq_start = pl.program_id(0) % tiles_per_seq == 0

        @pl.when(seq_start)
        def _():
            ext_scr[0:H, :] = jnp.zeros((H, SC_WIDTH), F32)

        @pl.when(jnp.logical_not(seq_start))
        def _():
            ext_scr[0:H, :] = ext_scr[bm:bm + H, :]

    sc = sc_ref[...].astype(F32)
    u = sc[:, SC_WIDTH:2 * SC_WIDTH] * sc[:, 2 * SC_WIDTH:]
    y_sc = (sc[:, :SC_WIDTH] * _causal_conv(ext_scr, u, cw_ref, H, shift)).astype(BF16)
    tail_ref[0] = ext_scr[bm:bm + H, :]

    acc = jnp.zeros((bm, D_MODEL), F32)
    for b, y in enumerate((yg_ref[...], yf_ref[...], y_sc)):
        g = jax.nn.sigmoid(gates_ref[:, b * D_MODEL:(b + 1) * D_MODEL].astype(F32))
        acc = acc + g * _dot(y, wb_ref[b])
    hout_ref[...] = h_ref[...] + _dot(acc.astype(BF16), wo_ref[...])


def _mix(h, z, yg, yf, cw, wb, wo, halo, bm, H, shift, tiles_per_seq):
    T = h.shape[0]
    has_state = halo is not None
    in_specs = [
        pl.BlockSpec((bm, D_MODEL), lambda i: (i, 0)),
        pl.BlockSpec((bm, N_BRANCH * D_MODEL), lambda i: (i, 0)),
        pl.BlockSpec((bm, 3 * SC_WIDTH), lambda i: (i, Z_SC // (3 * SC_WIDTH))),
        pl.BlockSpec((bm, SC_WIDTH), lambda i: (i, 0)),
        pl.BlockSpec((bm, SC_WIDTH), lambda i: (i, 0)),
        pl.BlockSpec((CONV_K, SC_WIDTH), lambda i: (0, 0)),
        pl.BlockSpec((N_BRANCH, SC_WIDTH, D_MODEL), lambda i: (0, 0, 0)),
        pl.BlockSpec((D_MODEL, D_MODEL), lambda i: (0, 0)),
    ]
    args = [h, z, z, yg, yf, cw, wb, wo]
    if has_state:
        in_specs.append(pl.BlockSpec((H, SC_WIDTH), lambda i: (0, 0)))
        args.append(halo)
    return pl.pallas_call(
        functools.partial(_mix_body, H=H, shift=shift, tiles_per_seq=tiles_per_seq, has_state=has_state),
        grid=(T // bm,),
        in_specs=in_specs,
        out_specs=[pl.BlockSpec((bm, D_MODEL), lambda i: (i, 0)),
                   pl.BlockSpec((1, H, SC_WIDTH), lambda i: (i, 0, 0))],
        out_shape=[jax.ShapeDtypeStruct((T, D_MODEL), F32),
                   jax.ShapeDtypeStruct((T // bm, H, SC_WIDTH), F32)],
        scratch_shapes=[pltpu.VMEM((H + bm, SC_WIDTH), F32)],
        compiler_params=_cparams(("arbitrary",), 48),
    )(*args)


FFN_CK = 256


def _ffn_body(*refs, H, shift, tiles_per_seq, has_state, final_norm):
    refs = list(refs)
    h_ref, g_ref, wa_ref, wg_ref, cw_ref, cb_ref, wd_ref = refs[:7]
    refs = refs[7:]
    halo_ref = refs.pop(0) if has_state else None
    fg_ref = refs.pop(0) if final_norm else None
    out_ref, tail_ref, xn_scr, acc_scr, ext_scr, halo_scr = refs
    i = pl.program_id(0)
    c = pl.program_id(1)
    bm = h_ref.shape[0]

    @pl.when(c == 0)
    def _():
        xn_scr[...] = _rms(h_ref[...], g_ref[...]).astype(BF16)
        acc_scr[...] = jnp.zeros_like(acc_scr)

    xn = xn_scr[...]
    a = _dot(xn, wa_ref[...])
    gate = _dot(xn, wg_ref[...])
    if has_state:
        ext_scr[0:H, :] = halo_ref[...]
    else:
        seq_start = i % tiles_per_seq == 0

        @pl.when(seq_start)
        def _():
            ext_scr[0:H, :] = jnp.zeros((H, a.shape[1]), F32)

        @pl.when(jnp.logical_not(seq_start))
        def _():
            ext_scr[0:H, :] = halo_scr[c]

    ac =_causal_conv(ext_scr, a, cw_ref, H, shift) + cb_ref[...]
    tail = ext_scr[bm:bm + H, :]
    tail_ref[0] = tail
    if not has_state:
        halo_scr[c] = tail
    act = 0.5 * ac * (1.0 + lax.erf(ac * (2.0 ** -0.5))) * gate
    acc_scr[...] += _dot(act.astype(BF16), wd_ref[...])

    @pl.when(c == pl.num_programs(1) - 1)
    def _():
        hn = h_ref[...] + acc_scr[...]
        out_ref[...] = _rms(hn, fg_ref[...]) if final_norm else hn


def _ffn(h, g, wa, wg, cw, cb, wd, halo, fg, bm, H, shift, tiles_per_seq):
    T = h.shape[0]
    ck = FFN_CK
    nc = D_FF // ck
    has_state = halo is not None
    final_norm = fg is not None
    in_specs = [
        pl.BlockSpec((bm, D_MODEL), lambda i, c: (i, 0)),
        pl.BlockSpec((1, D_MODEL), lambda i, c: (0, 0)),
        pl.BlockSpec((D_MODEL, ck), lambda i, c: (0, c)),
        pl.BlockSpec((D_MODEL, ck), lambda i, c: (0, c)),
        pl.BlockSpec((CONV_K, ck), lambda i, c: (0, c)),
        pl.BlockSpec((1, ck), lambda i, c: (0, c)),
        pl.BlockSpec((ck, D_MODEL), lambda i, c: (c, 0)),
    ]
    args = [h, g, wa, wg, cw, cb, wd]
    if has_state:
        in_specs.append(pl.BlockSpec((H, ck), lambda i, c: (0, c)))
        args.append(halo)
    if final_norm:
        in_specs.append(pl.BlockSpec((1, D_MODEL), lambda i, c: (0, 0)))
        args.append(fg)
    return pl.pallas_call(
        functools.partial(_ffn_body, H=H, shift=shift, tiles_per_seq=tiles_per_seq, has_state=has_state,
                          final_norm=final_norm),
        grid=(T // bm, nc),
        in_specs=in_specs,
        out_specs=[pl.BlockSpec((bm, D_MODEL), lambda i, c: (i, 0)),
                   pl.BlockSpec((1, H, ck), lambda i, c: (i, 0, c))],
        out_shape=[jax.ShapeDtypeStruct((T, D_MODEL), F32),
                   jax.ShapeDtypeStruct((T // bm, H, D_FF), F32)],
        scratch_shapes=[pltpu.VMEM((bm, D_MODEL), BF16), pltpu.VMEM((bm, D_MODEL), F32),
                        pltpu.VMEM((H + bm, ck), F32), pltpu.VMEM((nc, H, ck), F32)],
        compiler_params=_cparams(("arbitrary", "arbitrary"), 48),
    )(*args)


def _pack_in_proj(w_in):
    sizes = (GLA_HEADS * GLA_DK, GLA_HEADS * GLA_DK, GLA_HEADS * GLA_DV, GLA_LR, GLA_HEADS * GLA_DV,
             FOX_HEADS * FOX_HD, FOX_HEADS * FOX_HD, FOX_HEADS * FOX_HD, FOX_HEADS,
             SC_WIDTH, SC_WIDTH, SC_WIDTH, N_BRANCH * D_MODEL)
    offs = np.concatenate([[0], np.cumsum(sizes)])
    part = lambda n: w_in[:, :, offs[n]:offs[n + 1]]
    gq, gk, gv, glr, gg, fq, fk, fv, ff, scb, scc, sch, gates = [part(n) for n in range(len(sizes))]
    w_main = jnp.concatenate([gates, gq, gk, gv, gg, fq, fk, fv, scb, scc, sch], axis=-1).astype(BF16)
    pad = jnp.zeros(w_in.shape[:2] + (SMALL_COLS - GLA_LR - FOX_HEADS,), w_in.dtype)
    w_small = jnp.concatenate([glr, ff, pad], axis=-1).astype(BF16)
    w_fft = jnp.swapaxes(ff, 1, 2).astype(BF16)
    return w_main, w_small, w_fft


def kernel(x_prompt, x_sample, cache_fox_k, cache_fox_v, cache_fox_logf, state_gla, state_sconv, state_ffn_conv,
           page_table, norm1_g, w_in, w_alpha2, b_alpha, b_forget, gla_norm_g, sconv_w, w_branch, w_out, norm2_g,
           w_up, ffn_conv_w, ffn_conv_b, w_down, final_norm_g):
    depth = w_in.shape[0]
    B, L, _ = x_prompt.shape
    nb, nt, _ = x_sample.shape
    n_pool, page = cache_fox_k.shape[1], cache_fox_k.shape[2]
    n_pages = page_table.shape[1]
    W = FOX_HEADS * FOX_HD
    Tp = B * L
    Ts = nb * nt
    bm_p = min(512, L)
    fox_blk = min(512, L)
    H_p, H_s = SUBLANES, (CONV_K - 1) * nb

    w_main, w_small, w_fft = _pack_in_proj(w_in)
    wa_pad = jnp.concatenate(
        [w_alpha2, jnp.zeros((depth, SMALL_COLS - GLA_LR, GLA_HEADS * GLA_DK), w_alpha2.dtype)], axis=1).astype(BF16)
    wa_heads = wa_pad.reshape(depth, SMALL_COLS, GLA_HEADS, GLA_DK).transpose(0, 2, 1, 3)
    ba_heads = b_alpha.reshape(depth, GLA_HEADS, 1, GLA_DK)
    wb = w_branch.astype(BF16)
    wo = w_out.astype(BF16)
    wu_a = w_up[:, :, :D_FF].astype(BF16)
    wu_g = w_up[:, :, D_FF:].astype(BF16)
    wd = w_down.astype(BF16)

    cache_k = cache_fox_k.reshape(depth, n_pool, page, W)
    cache_v = cache_fox_v.reshape(depth, n_pool, page, W)
    cache_lft = jnp.swapaxes(cache_fox_logf, 2, 3)
    pt_flat = page_table.reshape(-1).astype(jnp.int32)

    def to_seq(a):
        return jnp.swapaxes(a.reshape((a.shape[0] // nb, nb) + a.shape[1:]), 0, 1)

    def to_time(a):
        return jnp.swapaxes(a, 0, 1).reshape((a.shape[0] * a.shape[1],) + a.shape[2:])

    hp = x_prompt.reshape(Tp, D_MODEL)
    hs = to_time(x_sample)
    outs_p, outs_s = [], []
    for l in range(depth):
        g1 = norm1_g[l].reshape(1, D_MODEL)
        g2 = norm2_g[l].reshape(1, D_MODEL)
        bf = b_forget[l].reshape(FOX_HEADS, 1)
        ba = b_alpha[l].reshape(1, -1)
        gn = gla_norm_g[l].reshape(1, GLA_DV)
        cb = ffn_conv_b[l].reshape(1, D_FF)
        fg = final_norm_g.reshape(1, D_MODEL) if l == depth - 1 else None

        z, fk, fv, small, fft = _in_proj(hp, g1, w_main[l], w_small[l], w_fft[l], bm_p)
        yg, st, logf_t, c_t = _gla_prompt(z, small, fft, wa_pad[l], ba, bf, gn, B, L)
        ct4 = c_t.reshape(FOX_HEADS // 2, 2, Tp)
        yf = _fox_prompt(z, jnp.swapaxes(ct4, 1, 2), ct4, B, L, fox_blk)
        hp, sc_tail = _mix(hp, z, yg, yf, sconv_w[l], wb[l], wo[l], None, bm_p, H_p, 1, L // bm_p)
        hp, ffn_tail = _ffn(hp, g2, wu_a[l], wu_g[l], ffn_conv_w[l], cb, wd[l], None, fg, bm_p, H_p, 1, L // bm_p)
        last = lambda t: t.reshape(B, L // bm_p, H_p, -1)[:, -1, H_p - (CONV_K - 1):, :]
        p_gla = st.reshape(B, GLA_HEADS // 2, GLA_DV, 2, GLA_DK).transpose(0, 1, 3, 4, 2)
        outs_p.append((fk.reshape(B, L, FOX_HEADS, FOX_HD), fv.reshape(B, L, FOX_HEADS, FOX_HD),
                       logf_t.T.reshape(B, L, FOX_HEADS), p_gla.reshape(B, GLA_HEADS, GLA_DK, GLA_DV),
                       last(sc_tail), last(ffn_tail)))

        z, fk, fv, small, fft = _in_proj(hs, g1, w_main[l], w_small[l], w_fft[l], Ts)
        qk_h = z[:, Z_GQK:Z_GV].reshape(Ts, 2, GLA_HEADS, GLA_DK).transpose(1, 2, 0, 3)
        yg, s1 = _gla_sample(qk_h[0], qk_h[1], z, small, wa_heads[l], ba_heads[l], gn,
                             state_gla[l].reshape(nb, -1), nb, nt)
        yf_s, lfn = _fox_sample(pt_flat, to_seq(z[:, Z_FQ:Z_FK].astype(F32)), to_seq(fk), to_seq(fv),
                                jnp.swapaxes(fft.reshape(FOX_HEADS, nt, nb), 0, 2).swapaxes(1, 2), bf,
                                cache_k, cache_v, cache_lft, l, nb, nt, n_pages, page)
        yf = to_time(yf_s).astype(BF16)
        halo_sc = to_time(state_sconv[l])
        halo_ffn = to_time(state_ffn_conv[l])
        hs, sc_tail = _mix(hs, z, yg, yf, sconv_w[l], wb[l], wo[l], halo_sc, Ts, H_s, nb, 1)
        hs, ffn_tail = _ffn(hs, g2, wu_a[l], wu_g[l], ffn_conv_w[l], cb, wd[l], halo_ffn, fg, Ts, H_s, nb, 1)
        outs_s.append((to_seq(fk).reshape(nb, nt, FOX_HEADS, FOX_HD), to_seq(fv).reshape(nb, nt, FOX_HEADS, FOX_HD),
                       jnp.swapaxes(lfn, 1, 2), s1.reshape(nb, GLA_HEADS, GLA_DK, GLA_DV),
                       to_seq(sc_tail[0]), to_seq(ffn_tail[0])))

    p_states = [jnp.stack(t) for t in zip(*outs_p)]
    s_states = [jnp.stack(t) for t in zip(*outs_s)]
    y_prompt = hp.reshape(B, L, D_MODEL)
    y_sample = to_seq(hs)
    return (y_prompt, y_sample, *p_states, *s_states)
```

```python
import functools

import numpy as np
import jax
import jax.numpy as jnp
from jax import lax
from jax.experimental import pallas as pl
from jax.experimental.pallas import tpu as pltpu

F32 = jnp.float32
BF16 = jnp.bfloat16

D_MODEL = 1024
GLA_HEADS = 4
GLA_DK = 64
GLA_DV = 128
GLA_LR = 16
GLA_TAU = 16.0
FOX_HEADS = 8
FOX_HD = 64
SC_WIDTH = 512
CONV_K = 3
D_FF = 2816
N_BRANCH = 3
EPS = 1e-6

LANES = 128
SUBLANES = 8
VMEM_BYTES = 64 << 20

Z_GATES = 0
Z_GQK = 3072
Z_GV = 3584
Z_GG = 4096
Z_FQ = 4608
Z_FK = 5120
Z_FV = 5632
Z_SC = 6144
Z_COLS = 7680
IN_BN = 512
SMALL_COLS = 128

GLA_CHUNK = 128
NEG = -1e30


def _cparams(sem, vmem_mb):
    return pltpu.CompilerParams(dimension_semantics=sem, vmem_limit_bytes=min(vmem_mb << 20, VMEM_BYTES - (4 << 20)))


def _log_sigmoid(x):
    return jnp.minimum(x, 0.0) - jnp.log1p(jnp.exp(-jnp.abs(x)))


def _dot(a, b):
    return jnp.dot(a, b, preferred_element_type=F32)


def _dot_nt(a, b):
    return lax.dot_general(a, b, (((1,), (1,)), ((), ())), preferred_element_type=F32)


def _dot_tn(a, b):
    return lax.dot_general(a, b, (((0,), (0,)), ((), ())), preferred_element_type=F32)


def _rms(x, g):
    return x * lax.rsqrt(jnp.mean(x * x, axis=-1, keepdims=True) + EPS) * g


def _lane_prefix_sum(c, lane):
    sh = 1
    while sh < LANES:
        c = c + jnp.where(lane >= sh, pltpu.roll(c, sh, axis=1), 0.0)
        sh *= 2
    return c


def _in_proj_body(x_ref, g_ref, w_ref, ws_ref, wft_ref, z_ref, fk_ref, fv_ref, small_ref, fft_ref, xn_scr):
    j = pl.program_id(1)

    @pl.when(j == 0)
    def _():
        xb = _rms(x_ref[...], g_ref[...]).astype(BF16)
        xn_scr[...] = xb
        small_ref[...] = _dot(xb, ws_ref[...])
        fft_ref[...] = _dot_nt(wft_ref[...], xb)

    zt = _dot(xn_scr[...], w_ref[...])
    z_ref[...] = zt.astype(BF16)

    @pl.when(j == Z_FK // IN_BN)
    def _():
        fk_ref[...] = zt

    @pl.when(j == Z_FV // IN_BN)
    def _():
        fv_ref[...] = zt


def _in_proj(x, g, w, ws, wft, bm):
    T = x.shape[0]
    grid = (T // bm, Z_COLS // IN_BN)
    return pl.pallas_call(
        _in_proj_body,
        grid=grid,
        in_specs=[
            pl.BlockSpec((bm, D_MODEL), lambda i, j: (i, 0)),
            pl.BlockSpec((1, D_MODEL), lambda i, j: (0, 0)),
            pl.BlockSpec((D_MODEL, IN_BN), lambda i, j: (0, j)),
            pl.BlockSpec((D_MODEL, SMALL_COLS), lambda i, j: (0, 0)),
            pl.BlockSpec((FOX_HEADS, D_MODEL), lambda i, j: (0, 0)),
        ],
        out_specs=[
            pl.BlockSpec((bm, IN_BN), lambda i, j: (i, j)),
            pl.BlockSpec((bm, FOX_HEADS * FOX_HD), lambda i, j: (i, 0)),
            pl.BlockSpec((bm, FOX_HEADS * FOX_HD), lambda i, j: (i, 0)),
            pl.BlockSpec((bm, SMALL_COLS), lambda i, j: (i, 0)),
            pl.BlockSpec((FOX_HEADS, bm), lambda i, j: (0, i)),
        ],
        out_shape=[
            jax.ShapeDtypeStruct((T, Z_COLS), BF16),
            jax.ShapeDtypeStruct((T, FOX_HEADS * FOX_HD), F32),
            jax.ShapeDtypeStruct((T, FOX_HEADS * FOX_HD), F32),
            jax.ShapeDtypeStruct((T, SMALL_COLS), F32),
            jax.ShapeDtypeStruct((FOX_HEADS, T), F32),
        ],
        scratch_shapes=[pltpu.VMEM((bm, D_MODEL), BF16)],
        compiler_params=_cparams(("arbitrary", "arbitrary"), 40),
    )(x, g, w, ws, wft)


def _gla_consts(C):
    t = np.arange(C)
    r = t[None, :]
    mats = [r <= t[:, None], r > t[:, None]]
    masks = []
    n = C // 2
    while n >= 1:
        pair = t // (2 * n)
        second = (t // n) % 2 == 1
        mid = pair * 2 * n + n
        mats.append(np.where(second[:, None], (r >= mid[:, None]) & (r <= t[:, None]),
                             (r > t[:, None]) & (r < mid[:, None])))
        masks.append(second[:, None] & ~second[None, :] & (pair[:, None] == pair[None, :]))
        n //= 2
    masks.append(np.eye(C, dtype=bool))
    return (jnp.asarray(np.concatenate(mats, 0).astype(np.float32), BF16),
            jnp.asarray(np.stack(masks).astype(np.float32)))


def _gla_prompt_body(qk_ref, v_ref, gg_ref, small_ref, fft_ref, wa_ref, ba_ref, bf_ref, gn_ref, wall_ref,
                     masks_ref, y_ref, st_ref, logf_ref, c_ref, s_scr, c_scr, *, C, nlev):
    i = pl.program_id(1)

    @pl.when(i == 0)
    def _():
        s_scr[...] = jnp.zeros_like(s_scr)
        c_scr[...] = jnp.zeros_like(c_scr)

    lf = _log_sigmoid(fft_ref[...] + bf_ref[...])
    logf_ref[...] = lf
    lane = lax.broadcasted_iota(jnp.int32, lf.shape, 1)
    c = _lane_prefix_sum(lf, lane) + c_scr[...]
    c_ref[...] = c
    c_scr[...] = jnp.broadcast_to(c[:, C - 1:C], c_scr.shape)

    la = _log_sigmoid(_dot(small_ref[...].astype(BF16), wa_ref[...]) + ba_ref[...]) * (1.0 / GLA_TAU)
    hi = la.astype(BF16)
    lo = (la - hi.astype(F32)).astype(BF16)
    wall = wall_ref[...]
    P = jnp.exp(_dot(wall, hi) + _dot(wall, lo))

    W = GLA_HEADS * GLA_DK
    qk = qk_ref[...].astype(F32)
    q = qk[:, :W] * (GLA_DK ** -0.5)
    k = qk[:, W:]
    qd = (q * P[0:C]).astype(BF16)
    kd = (k * P[C:2 * C]).astype(BF16)
    a_last = P[C - 1:C, :]
    row = lax.broadcasted_iota(jnp.int32, (C, 1), 0)
    X = []
    n = C // 2
    for l in range(nlev):
        X.append((jnp.where((row & n) != 0, q, k) * P[(2 + l) * C:(3 + l) * C]).astype(BF16))
        n //= 2
    qb = q.astype(BF16)
    kb = k.astype(BF16)

    lane_p = lax.broadcasted_iota(jnp.int32, (1, LANES), 1)
    for p in range(GLA_HEADS // 2):
        sl = slice(LANES * p, LANES * (p + 1))
        ST = s_scr[p]
        STb = ST.astype(BF16)
        U = []
        for hh in range(2):
            h = 2 * p + hh
            hm = (lane_p >= GLA_DK) == bool(hh)

            def msk(a):
                return jnp.where(hm, a, jnp.zeros_like(a))

            sc = masks_ref[nlev] * _dot_nt(msk(qb[:, sl]), kb[:, sl])
            for l in range(nlev):
                xl = X[l][:, sl]
                sc = sc + masks_ref[l] * _dot_nt(msk(xl), xl)
            vh = v_ref[:, GLA_DV * h:GLA_DV * (h + 1)]
            o = _dot(sc.astype(BF16), vh) + _dot_nt(msk(qd[:, sl]), STb)
            U.append(_dot_tn(vh, kd[:, sl]))
            g = gg_ref[:, GLA_DV * h:GLA_DV * (h + 1)].astype(F32)
            y_ref[:, GLA_DV * h:GLA_DV * (h + 1)] = (_rms(o, gn_ref[...]) * (g * jax.nn.sigmoid(g))).astype(BF16)
        s_scr[p] = ST * a_last[:, sl] + jnp.where(lane_p < GLA_DK, U[0], U[1])

    @pl.when(i == pl.num_programs(1) - 1)
    def _():
        st_ref[0] = s_scr[...]


def _gla_prompt(z, small, fft, wa, ba, bf, gn, B, L):
    C = GLA_CHUNK
    nL = L // C
    wall, masks = _gla_consts(C)
    nlev = masks.shape[0] - 1
    T = B * L
    W = GLA_HEADS * GLA_DK
    row = lambda b, i: b * nL + i
    return pl.pallas_call(
        functools.partial(_gla_prompt_body, C=C, nlev=nlev),
        grid=(B, nL),
        in_specs=[
            pl.BlockSpec((C, 2 * W), lambda b, i: (row(b, i), Z_GQK // (2 * W))),
            pl.BlockSpec((C, GLA_HEADS * GLA_DV), lambda b, i: (row(b, i), Z_GV // (GLA_HEADS * GLA_DV))),
            pl.BlockSpec((C, GLA_HEADS * GLA_DV), lambda b, i: (row(b, i), Z_GG // (GLA_HEADS * GLA_DV))),
            pl.BlockSpec((C, SMALL_COLS), lambda b, i: (row(b, i), 0)),
            pl.BlockSpec((FOX_HEADS, C), lambda b, i: (0, row(b, i))),
            pl.BlockSpec((SMALL_COLS, W), lambda b, i: (0, 0)),
            pl.BlockSpec((1, W), lambda b, i: (0, 0)),
            pl.BlockSpec((FOX_HEADS, 1), lambda b, i: (0, 0)),
            pl.BlockSpec((1, GLA_DV), lambda b, i: (0, 0)),
            pl.BlockSpec(wall.shape, lambda b, i: (0, 0)),
            pl.BlockSpec(masks.shape, lambda b, i: (0, 0, 0)),
        ],
        out_specs=[
            pl.BlockSpec((C, GLA_HEADS * GLA_DV), lambda b, i: (row(b, i), 0)),
            pl.BlockSpec((1, GLA_HEADS // 2, GLA_DV, 2 * GLA_DK), lambda b, i: (b, 0, 0, 0)),
            pl.BlockSpec((FOX_HEADS, C), lambda b, i: (0, row(b, i))),
            pl.BlockSpec((FOX_HEADS, C), lambda b, i: (0, row(b, i))),
        ],
        out_shape=[
            jax.ShapeDtypeStruct((T, GLA_HEADS * GLA_DV), BF16),
            jax.ShapeDtypeStruct((B, GLA_HEADS // 2, GLA_DV, 2 * GLA_DK), F32),
            jax.ShapeDtypeStruct((FOX_HEADS, T), F32),
            jax.ShapeDtypeStruct((FOX_HEADS, T), F32),
        ],
        scratch_shapes=[pltpu.VMEM((GLA_HEADS // 2, GLA_DV, 2 * GLA_DK), F32),
                        pltpu.VMEM((FOX_HEADS, LANES), F32)],
        compiler_params=_cparams(("arbitrary", "arbitrary"), 32),
    )(z, z, z, small, fft, wa, ba, bf, gn, wall, masks)


def _gla_sample_body(q_ref, k_ref, v_ref, gg_ref, small_ref, wa_ref, ba_ref, gn_ref, s0_ref, y_ref, s1_ref,
                     *, nb, nt):
    la = _log_sigmoid(_dot(small_ref[...].astype(BF16), wa_ref[0]) + ba_ref[0]) * (1.0 / GLA_TAU)
    q = q_ref[0].astype(F32) * (GLA_DK ** -0.5)
    k = k_ref[0].astype(F32)
    rows = [slice(t * nb, (t + 1) * nb) for t in range(nt)]
    b = []
    for t in range(nt):
        b.append(la[rows[t]] if t == 0 else b[-1] + la[rows[t]])
    qt = [q[r] for r in rows]
    kt = [k[r] for r in rows]
    qd = [qt[t] * jnp.exp(b[t]) for t in range(nt)]
    kd = [kt[t] * jnp.exp(b[nt - 1] - b[t]) for t in range(nt)]
    a_last = jnp.exp(b[nt - 1])

    for d in range(GLA_DK):
        cs = slice(d * GLA_DV, (d + 1) * GLA_DV)
        acc = a_last[:, d:d + 1] * s0_ref[:, cs]
        for t in range(nt):
            acc = acc + kd[t][:, d:d + 1] * v_ref[rows[t], :].astype(F32)
        s1_ref[:, cs] = acc

    for t in range(nt):
        o = jnp.zeros((nb, GLA_DV), F32)
        for s in range(t + 1):
            w = jnp.sum(qt[t] * kt[s] * jnp.exp(b[t] - b[s]), axis=-1, keepdims=True)
            o = o + w * v_ref[rows[s], :].astype(F32)
        for d in range(GLA_DK):
            o = o + qd[t][:, d:d + 1] * s0_ref[:, d * GLA_DV:(d + 1) * GLA_DV]
        g = gg_ref[rows[t], :].astype(F32)
        y_ref[rows[t], :] = (_rms(o, gn_ref[...]) * (g * jax.nn.sigmoid(g))).astype(BF16)


def _gla_sample(qh, kh, z, small, wa_h, ba_h, gn, s0, nb, nt):
    T = nb * nt
    slab = GLA_DK * GLA_DV
    return pl.pallas_call(
        functools.partial(_gla_sample_body, nb=nb, nt=nt),
        grid=(GLA_HEADS,),
        in_specs=[
            pl.BlockSpec((1, T, GLA_DK), lambda h: (h, 0, 0)),
            pl.BlockSpec((1, T, GLA_DK), lambda h: (h, 0, 0)),
            pl.BlockSpec((T, GLA_DV), lambda h: (0, Z_GV // GLA_DV + h)),
            pl.BlockSpec((T, GLA_DV), lambda h: (0, Z_GG // GLA_DV + h)),
            pl.BlockSpec((T, SMALL_COLS), lambda h: (0, 0)),
            pl.BlockSpec((1, SMALL_COLS, GLA_DK), lambda h: (h, 0, 0)),
            pl.BlockSpec((1, 1, GLA_DK), lambda h: (h, 0, 0)),
            pl.BlockSpec((1, GLA_DV), lambda h: (0, 0)),
            pl.BlockSpec((nb, slab), lambda h: (0, h)),
        ],
        out_specs=[
            pl.BlockSpec((T, GLA_DV), lambda h: (0, h)),
            pl.BlockSpec((nb, slab), lambda h: (0, h)),
        ],
        out_shape=[
            jax.ShapeDtypeStruct((T, GLA_HEADS * GLA_DV), BF16),
            jax.ShapeDtypeStruct((nb, GLA_HEADS * slab), F32),
        ],
        compiler_params=_cparams(("arbitrary",), 40),
    )(qh, kh, z, z, small, wa_h, ba_h, gn, s0)


def _split3(c):
    def top(x):
        bits = lax.bitcast_convert_type(x, jnp.uint32) & jnp.uint32(0xFFFF0000)
        return lax.bitcast_convert_type(bits, F32)

    hi = top(c)
    mid = top(c - hi)
    lo = c - hi - mid
    return hi.astype(BF16), mid.astype(BF16), lo.astype(BF16)


def _fox_operands(fq, fk, fv, c):
    T = fq.shape[0]
    heads = lambda a: a.reshape(T, FOX_HEADS, FOX_HD)
    cs = [t[..., None] for t in _split3(c)]
    ones = jnp.ones((T, FOX_HEADS, 3), BF16)
    pad = jnp.zeros((T, FOX_HEADS, LANES - FOX_HD - 6), BF16)
    q = jnp.concatenate([heads(fq) * (FOX_HD ** -0.5)] + cs + [ones, pad], axis=-1)
    k = jnp.concatenate([heads(fk), ones] + [-t for t in cs] + [pad], axis=-1)
    v = jnp.concatenate([heads(fv), jnp.ones((T, FOX_HEADS, LANES - FOX_HD), BF16)], axis=-1)
    q, k, v = [a.reshape(T, FOX_HEADS * LANES) for a in (q, k, v)]
    return q.T, k, v.T


def _fox_prompt_body(qt_ref, k_ref, vt_ref, o_ref, s_scr, *, bq, bk):
    row0 = pl.program_id(2) * bq
    n_full = lax.div(row0, bk)
    qts = [qt_ref[hh * LANES:(hh + 1) * LANES, :] for hh in range(2)]

    def scores(j):
        start = pl.multiple_of(j * bk, bk)
        return tuple(_dot(k_ref[pl.ds(start, bk), hh * LANES:(hh + 1) * LANES], qts[hh]) for hh in range(2))

    def softmax(j, s, m, masked):
        out = []
        for hh in range(2):
            sh = s[hh]
            if masked:
                key = j * bk + lax.broadcasted_iota(jnp.int32, (bk, bq), 0)
                qry = row0 + lax.broadcasted_iota(jnp.int32, (bk, bq), 1)
                sh = jnp.where(key <= qry, sh, NEG)
            m_new = jnp.maximum(m[hh], jnp.max(sh, axis=0, keepdims=True))
            out.append((m_new, jnp.exp(sh - m_new).astype(BF16), jnp.exp(m[hh] - m_new)))
        return tuple(zip(*out))

    def accumulate(j, p, alpha, acc):
        start = pl.multiple_of(j * bk, bk)
        return tuple(alpha[hh] * acc[hh] + _dot(vt_ref[hh * LANES:(hh + 1) * LANES, pl.ds(start, bk)], p[hh])
                     for hh in range(2))

    def put_scores(j):
        for hh, sh in enumerate(scores(j)):
            s_scr[hh] = sh

    def step(j, carry):
        m, acc = carry
        m, p, alpha = softmax(j, (s_scr[0], s_scr[1]), m, False)
        put_scores(j + 1)
        return m, accumulate(j, p, alpha, acc)

    two = lambda a: (a, a)
    put_scores(0)
    m, acc = lax.fori_loop(0, n_full, step, (two(jnp.full((1, bq), NEG, F32)), two(jnp.zeros((LANES, bq), F32))))
    m, p, alpha = softmax(n_full, (s_scr[0], s_scr[1]), m, True)
    acc = accumulate(n_full, p, alpha, acc)
    o_ref[...] = jnp.concatenate([a[:FOX_HD] / a[FOX_HD:FOX_HD + 1] for a in acc], axis=0).astype(BF16)


def _fox_prompt(qt, ka, vt, B, L, bq, bk):
    nq = L // bq
    T = B * L
    npair = FOX_HEADS // 2
    return pl.pallas_call(
        functools.partial(_fox_prompt_body, bq=bq, bk=bk),
        grid=(B, npair, nq),
        in_specs=[
            pl.BlockSpec((2 * LANES, bq), lambda b, p, i: (p, b * nq + i)),
            pl.BlockSpec((L, 2 * LANES), lambda b, p, i: (b, p)),
            pl.BlockSpec((2 * LANES, L), lambda b, p, i: (p, b)),
        ],
        out_specs=pl.BlockSpec((LANES, bq), lambda b, p, i: (p, b * nq + i)),
        out_shape=jax.ShapeDtypeStruct((FOX_HEADS * FOX_HD, T), BF16),
        scratch_shapes=[pltpu.VMEM((2, bk, bq), F32)],
        compiler_params=_cparams(("arbitrary", "arbitrary", "arbitrary"), 48),
    )(qt, ka, vt)


def _fox_sample_body(pt_ref, q_ref, kn_ref, vn_ref, fft_ref, bf_ref, *rest, n_pages, nt):
    del pt_ref
    k_refs = rest[:n_pages]
    v_refs = rest[n_pages:2 * n_pages]
    lf_refs = rest[2 * n_pages:3 * n_pages]
    o_ref, lfo_ref = rest[3 * n_pages:]
    W = FOX_HEADS * FOX_HD
    H = FOX_HEADS

    q = q_ref[0] * (FOX_HD ** -0.5)
    head_shift = FOX_HD.bit_length() - 1
    hmask = (lax.broadcasted_iota(jnp.int32, (H, W), 1) >> head_shift) == lax.broadcasted_iota(jnp.int32, (H, W), 0)
    qbd = jnp.concatenate([jnp.where(hmask, jnp.broadcast_to(q[t:t + 1], (H, W)), 0.0) for t in range(nt)], axis=0)
    qbd_b = qbd.astype(BF16)

    lane = lax.broadcasted_iota(jnp.int32, (H, LANES), 1)
    carry = jnp.zeros((H, 1), F32)
    ck = []
    for j in range(n_pages):
        c = _lane_prefix_sum(lf_refs[j][...], lane) + carry
        carry = c[:, LANES - 1:LANES]
        ck.append(c)
    lfn = _log_sigmoid(fft_ref[0] + bf_ref[...])
    lfo_ref[0] = lfn
    lane_t = lax.broadcasted_iota(jnp.int32, (H, nt), 1)
    cn = carry
    for j in range(nt):
        cn = cn + jnp.where(lane_t >= j, lfn[:, j:j + 1], 0.0)
    cq = jnp.concatenate([cn[:, t:t + 1] for t in range(nt)], axis=0)

    ck_all = jnp.concatenate(ck, axis=1)
    s_past = jnp.concatenate([_dot_nt(qbd_b, k_refs[j][...].astype(BF16)) for j in range(n_pages)], axis=1)
    s_past = s_past + (cq - jnp.concatenate([ck_all] * nt, axis=0))

    kn = kn_ref[0]
    vn = vn_ref[0]
    trow = lax.broadcasted_iota(jnp.int32, (nt * H, 1), 0) >> (H.bit_length() - 1)
    s_new = []
    for tp in range(nt):
        ckn = jnp.concatenate([cn[:, tp:tp + 1]] * nt, axis=0)
        sn = jnp.sum(qbd * kn[tp:tp + 1], axis=-1, keepdims=True) + (cq - ckn)
        s_new.append(jnp.where(trow >= tp, sn, NEG))
    m = jnp.max(s_past, axis=-1, keepdims=True)
    for sn in s_new:
        m = jnp.maximum(m, sn)
    p_past = jnp.exp(s_past - m)
    den = jnp.sum(p_past, axis=-1, keepdims=True)
    out = jnp.zeros((nt * H, W), F32)
    for j in range(n_pages):
        out = out + _dot(p_past[:, j * LANES:(j + 1) * LANES].astype(BF16), v_refs[j][...].astype(BF16))
    for tp in range(nt):
        pn = jnp.exp(s_new[tp] - m)
        den = den + pn
        out = out + pn * vn[tp:tp + 1]
    out = out / den
    for t in range(nt):
        o_ref[0, t:t + 1, :] = jnp.sum(jnp.where(hmask, out[t * H:(t + 1) * H], 0.0), axis=0, keepdims=True)


def _fox_sample(pt_flat, q_s, kn_s, vn_s, fft_s, bf, cache_k, cache_v, cache_lft, layer, nb, nt, n_pages, page):
    W = FOX_HEADS * FOX_HD

    def page_spec(shape, j):
        return pl.BlockSpec((None, None) + shape, lambda s, pt: (layer, pt[s * n_pages + j], 0, 0))

    seq3 = lambda s, pt: (s, 0, 0)
    in_specs = [
        pl.BlockSpec((1, nt, W), seq3),
        pl.BlockSpec((1, nt, W), seq3),
        pl.BlockSpec((1, nt, W), seq3),
        pl.BlockSpec((1, FOX_HEADS, nt), seq3),
        pl.BlockSpec((FOX_HEADS, 1), lambda s, pt: (0, 0)),
    ]
    in_specs += [page_spec((page, W), j) for j in range(n_pages)]
    in_specs += [page_spec((page, W), j) for j in range(n_pages)]
    in_specs += [page_spec((FOX_HEADS, page), j) for j in range(n_pages)]
    grid_spec = pltpu.PrefetchScalarGridSpec(
        num_scalar_prefetch=1,
        grid=(nb,),
        in_specs=in_specs,
        out_specs=[pl.BlockSpec((1, nt, W), seq3), pl.BlockSpec((1, FOX_HEADS, nt), seq3)],
    )
    return pl.pallas_call(
        functools.partial(_fox_sample_body, n_pages=n_pages, nt=nt),
        grid_spec=grid_spec,
        out_shape=[jax.ShapeDtypeStruct((nb, nt, W), F32), jax.ShapeDtypeStruct((nb, FOX_HEADS, nt), F32)],
        compiler_params=_cparams(("arbitrary",), 48),
    )(pt_flat, q_s, kn_s, vn_s, fft_s, bf, *([cache_k] * n_pages), *([cache_v] * n_pages), *([cache_lft] * n_pages))


def _causal_conv(ext_scr, u, cw_ref, H, shift):
    bm = u.shape[0]
    ext_scr[H:H + bm, :] = u
    out = cw_ref[CONV_K - 1:CONV_K, :] * u
    for i in range(1, CONV_K):
        lo = H - i * shift
        out = out + cw_ref[CONV_K - 1 - i:CONV_K - i, :] * ext_scr[lo:lo + bm, :]
    return out


def _mix_body(*refs, H, shift, tiles_per_seq, has_state):
    if has_state:
        h_ref, gates_ref, sc_ref, yg_ref, yf_ref, cw_ref, wb_ref, wo_ref, halo_ref, hout_ref, tail_ref, ext_scr = refs
    else:
        h_ref, gates_ref, sc_ref, yg_ref, yf_ref, cw_ref, wb_ref, wo_ref, hout_ref, tail_ref, ext_scr = refs
    bm = h_ref.shape[0]
    if has_state:
        ext_scr[0:H, :] = halo_ref[...]
    else:
        seq_start = pl.program_id(0) % tiles_per_seq == 0

        @pl.when(seq_start)
        def _():
            ext_scr[0:H, :] = jnp.zeros((H, SC_WIDTH), F32)

        @pl.when(jnp.logical_not(seq_start))
        def _():
            ext_scr[0:H, :] = ext_scr[bm:bm + H, :]

    sc = sc_ref[...].astype(F32)
    u = sc[:, SC_WIDTH:2 * SC_WIDTH] * sc[:, 2 * SC_WIDTH:]
    y_sc = (sc[:, :SC_WIDTH] * _causal_conv(ext_scr, u, cw_ref, H, shift)).astype(BF16)
    tail_ref[0] = ext_scr[bm:bm + H, :]

    acc = jnp.zeros((bm, D_MODEL), F32)
    for b, y in enumerate((yg_ref[...], yf_ref[...], y_sc)):
        g = jax.nn.sigmoid(gates_ref[:, b * D_MODEL:(b + 1) * D_MODEL].astype(F32))
        acc = acc + g * _dot(y, wb_ref[b])
    hout_ref[...] = h_ref[...] + _dot(acc.astype(BF16), wo_ref[...])


def _mix(h, z, yg, yf, cw, wb, wo, halo, bm, H, shift, tiles_per_seq):
    T = h.shape[0]
    has_state = halo is not None
    in_specs = [
        pl.BlockSpec((bm, D_MODEL), lambda i: (i, 0)),
        pl.BlockSpec((bm, N_BRANCH * D_MODEL), lambda i: (i, 0)),
        pl.BlockSpec((bm, 3 * SC_WIDTH), lambda i: (i, Z_SC // (3 * SC_WIDTH))),
        pl.BlockSpec((bm, SC_WIDTH), lambda i: (i, 0)),
        pl.BlockSpec((bm, SC_WIDTH), lambda i: (i, 0)),
        pl.BlockSpec((CONV_K, SC_WIDTH), lambda i: (0, 0)),
        pl.BlockSpec((N_BRANCH, SC_WIDTH, D_MODEL), lambda i: (0, 0, 0)),
        pl.BlockSpec((D_MODEL, D_MODEL), lambda i: (0, 0)),
    ]
    args = [h, z, z, yg, yf, cw, wb, wo]
    if has_state:
        in_specs.append(pl.BlockSpec((H, SC_WIDTH), lambda i: (0, 0)))
        args.append(halo)
    return pl.pallas_call(
        functools.partial(_mix_body, H=H, shift=shift, tiles_per_seq=tiles_per_seq, has_state=has_state),
        grid=(T // bm,),
        in_specs=in_specs,
        out_specs=[pl.BlockSpec((bm, D_MODEL), lambda i: (i, 0)),
                   pl.BlockSpec((1, H, SC_WIDTH), lambda i: (i, 0, 0))],
        out_shape=[jax.ShapeDtypeStruct((T, D_MODEL), F32),
                   jax.ShapeDtypeStruct((T // bm, H, SC_WIDTH), F32)],
        scratch_shapes=[pltpu.VMEM((H + bm, SC_WIDTH), F32)],
        compiler_params=_cparams(("arbitrary",), 48),
    )(*args)


FFN_CK = 256


def _ffn_body(*refs, H, shift, tiles_per_seq, has_state, final_norm):
    refs = list(refs)
    h_ref, g_ref, wa_ref, wg_ref, cw_ref, cb_ref, wd_ref = refs[:7]
    refs = refs[7:]
    halo_ref = refs.pop(0) if has_state else None
    fg_ref = refs.pop(0) if final_norm else None
    out_ref, tail_ref, xn_scr, acc_scr, ext_scr, halo_scr = refs
    i = pl.program_id(0)
    c = pl.program_id(1)
    bm = h_ref.shape[0]

    @pl.when(c == 0)
    def _():
        xn_scr[...] = _rms(h_ref[...], g_ref[...]).astype(BF16)
        acc_scr[...] = jnp.zeros_like(acc_scr)

    xn = xn_scr[...]
    a = _dot(xn, wa_ref[...])
    gate = _dot(xn, wg_ref[...])
    if has_state:
        ext_scr[0:H, :] = halo_ref[...]
    else:
        seq_start = i % tiles_per_seq == 0

        @pl.when(seq_start)
        def _():
            ext_scr[0:H, :] = jnp.zeros((H, a.shape[1]), F32)

        @pl.when(jnp.logical_not(seq_start))
        def _():
            ext_scr[0:H, :] = halo_scr[c]

    ac =_causal_conv(ext_scr, a, cw_ref, H, shift) + cb_ref[...]
    tail = ext_scr[bm:bm + H, :]
    tail_ref[0] = tail
    if not has_state:
        halo_scr[c] = tail
    act = 0.5 * ac * (1.0 + lax.erf(ac * (2.0 ** -0.5))) * gate
    acc_scr[...] += _dot(act.astype(BF16), wd_ref[...])

    @pl.when(c == pl.num_programs(1) - 1)
    def _():
        hn = h_ref[...] + acc_scr[...]
        out_ref[...] = _rms(hn, fg_ref[...]) if final_norm else hn


def _ffn(h, g, wa, wg, cw, cb, wd, halo, fg, bm, H, shift, tiles_per_seq):
    T = h.shape[0]
    ck = FFN_CK
    nc = D_FF // ck
    has_state = halo is not None
    final_norm = fg is not None
    in_specs = [
        pl.BlockSpec((bm, D_MODEL), lambda i, c: (i, 0)),
        pl.BlockSpec((1, D_MODEL), lambda i, c: (0, 0)),
        pl.BlockSpec((D_MODEL, ck), lambda i, c: (0, c)),
        pl.BlockSpec((D_MODEL, ck), lambda i, c: (0, c)),
        pl.BlockSpec((CONV_K, ck), lambda i, c: (0, c)),
        pl.BlockSpec((1, ck), lambda i, c: (0, c)),
        pl.BlockSpec((ck, D_MODEL), lambda i, c: (c, 0)),
    ]
    args = [h, g, wa, wg, cw, cb, wd]
    if has_state:
        in_specs.append(pl.BlockSpec((H, ck), lambda i, c: (0, c)))
        args.append(halo)
    if final_norm:
        in_specs.append(pl.BlockSpec((1, D_MODEL), lambda i, c: (0, 0)))
        args.append(fg)
    return pl.pallas_call(
        functools.partial(_ffn_body, H=H, shift=shift, tiles_per_seq=tiles_per_seq, has_state=has_state,
                          final_norm=final_norm),
        grid=(T // bm, nc),
        in_specs=in_specs,
        out_specs=[pl.BlockSpec((bm, D_MODEL), lambda i, c: (i, 0)),
                   pl.BlockSpec((1, H, ck), lambda i, c: (i, 0, c))],
        out_shape=[jax.ShapeDtypeStruct((T, D_MODEL), F32),
                   jax.ShapeDtypeStruct((T // bm, H, D_FF), F32)],
        scratch_shapes=[pltpu.VMEM((bm, D_MODEL), BF16), pltpu.VMEM((bm, D_MODEL), F32),
                        pltpu.VMEM((H + bm, ck), F32), pltpu.VMEM((nc, H, ck), F32)],
        compiler_params=_cparams(("arbitrary", "arbitrary"), 48),
    )(*args)


def _pack_in_proj(w_in):
    sizes = (GLA_HEADS * GLA_DK, GLA_HEADS * GLA_DK, GLA_HEADS * GLA_DV, GLA_LR, GLA_HEADS * GLA_DV,
             FOX_HEADS * FOX_HD, FOX_HEADS * FOX_HD, FOX_HEADS * FOX_HD, FOX_HEADS,
             SC_WIDTH, SC_WIDTH, SC_WIDTH, N_BRANCH * D_MODEL)
    offs = np.concatenate([[0], np.cumsum(sizes)])
    part = lambda n: w_in[:, :, offs[n]:offs[n + 1]]
    gq, gk, gv, glr, gg, fq, fk, fv, ff, scb, scc, sch, gates = [part(n) for n in range(len(sizes))]
    w_main = jnp.concatenate([gates, gq, gk, gv, gg, fq, fk, fv, scb, scc, sch], axis=-1).astype(BF16)
    pad = jnp.zeros(w_in.shape[:2] + (SMALL_COLS - GLA_LR - FOX_HEADS,), w_in.dtype)
    w_small = jnp.concatenate([glr, ff, pad], axis=-1).astype(BF16)
    w_fft = jnp.swapaxes(ff, 1, 2).astype(BF16)
    return w_main, w_small, w_fft


def kernel(x_prompt, x_sample, cache_fox_k, cache_fox_v, cache_fox_logf, state_gla, state_sconv, state_ffn_conv,
           page_table, norm1_g, w_in, w_alpha2, b_alpha, b_forget, gla_norm_g, sconv_w, w_branch, w_out, norm2_g,
           w_up, ffn_conv_w, ffn_conv_b, w_down, final_norm_g):
    depth = w_in.shape[0]
    B, L, _ = x_prompt.shape
    nb, nt, _ = x_sample.shape
    n_pool, page = cache_fox_k.shape[1], cache_fox_k.shape[2]
    n_pages = page_table.shape[1]
    W = FOX_HEADS * FOX_HD
    Tp = B * L
    Ts = nb * nt
    bm_p = min(512, L)
    fox_bq, fox_bk = min(256, L), min(512, L)
    H_p, H_s = SUBLANES, (CONV_K - 1) * nb

    w_main, w_small, w_fft = _pack_in_proj(w_in)
    wa_pad = jnp.concatenate(
        [w_alpha2, jnp.zeros((depth, SMALL_COLS - GLA_LR, GLA_HEADS * GLA_DK), w_alpha2.dtype)], axis=1).astype(BF16)
    wa_heads = wa_pad.reshape(depth, SMALL_COLS, GLA_HEADS, GLA_DK).transpose(0, 2, 1, 3)
    ba_heads = b_alpha.reshape(depth, GLA_HEADS, 1, GLA_DK)
    wb = w_branch.astype(BF16)
    wo = w_out.astype(BF16)
    wu_a = w_up[:, :, :D_FF].astype(BF16)
    wu_g = w_up[:, :, D_FF:].astype(BF16)
    wd = w_down.astype(BF16)

    cache_k = cache_fox_k.reshape(depth, n_pool, page, W)
    cache_v = cache_fox_v.reshape(depth, n_pool, page, W)
    cache_lft = jnp.swapaxes(cache_fox_logf, 2, 3)
    pt_flat = page_table.reshape(-1).astype(jnp.int32)

    def to_seq(a):
        return jnp.swapaxes(a.reshape((a.shape[0] // nb, nb) + a.shape[1:]), 0, 1)

    def to_time(a):
        return jnp.swapaxes(a, 0, 1).reshape((a.shape[0] * a.shape[1],) + a.shape[2:])

    hp = x_prompt.reshape(Tp, D_MODEL)
    hs = to_time(x_sample)
    outs_p, outs_s = [], []
    for l in range(depth):
        g1 = norm1_g[l].reshape(1, D_MODEL)
        g2 = norm2_g[l].reshape(1, D_MODEL)
        bf = b_forget[l].reshape(FOX_HEADS, 1)
        ba = b_alpha[l].reshape(1, -1)
        gn = gla_norm_g[l].reshape(1, GLA_DV)
        cb = ffn_conv_b[l].reshape(1, D_FF)
        fg = final_norm_g.reshape(1, D_MODEL) if l == depth - 1 else None

        z, fk, fv, small, fft = _in_proj(hp, g1, w_main[l], w_small[l], w_fft[l], bm_p)
        yg, st, logf_t, c_t = _gla_prompt(z, small, fft, wa_pad[l], ba, bf, gn, B, L)
        qt, ka, vt = _fox_operands(z[:, Z_FQ:Z_FK], z[:, Z_FK:Z_FV], z[:, Z_FV:Z_SC], c_t.T)
        yf = _fox_prompt(qt, ka, vt, B, L, fox_bq, fox_bk).T
        hp, sc_tail = _mix(hp, z, yg, yf, sconv_w[l], wb[l], wo[l], None, bm_p, H_p, 1, L // bm_p)
        hp, ffn_tail = _ffn(hp, g2, wu_a[l], wu_g[l], ffn_conv_w[l], cb, wd[l], None, fg, bm_p, H_p, 1, L // bm_p)
        last = lambda t: t.reshape(B, L // bm_p, H_p, -1)[:, -1, H_p - (CONV_K - 1):, :]
        p_gla = st.reshape(B, GLA_HEADS // 2, GLA_DV, 2, GLA_DK).transpose(0, 1, 3, 4, 2)
        outs_p.append((fk.reshape(B, L, FOX_HEADS, FOX_HD), fv.reshape(B, L, FOX_HEADS, FOX_HD),
                       logf_t.T.reshape(B, L, FOX_HEADS), p_gla.reshape(B, GLA_HEADS, GLA_DK, GLA_DV),
                       last(sc_tail), last(ffn_tail)))

        z, fk, fv, small, fft = _in_proj(hs, g1, w_main[l], w_small[l], w_fft[l], Ts)
        qk_h = z[:, Z_GQK:Z_GV].reshape(Ts, 2, GLA_HEADS, GLA_DK).transpose(1, 2, 0, 3)
        yg, s1 = _gla_sample(qk_h[0], qk_h[1], z, small, wa_heads[l], ba_heads[l], gn,
                             state_gla[l].reshape(nb, -1), nb, nt)
        yf_s, lfn = _fox_sample(pt_flat, to_seq(z[:, Z_FQ:Z_FK].astype(F32)), to_seq(fk), to_seq(fv),
                                jnp.swapaxes(fft.reshape(FOX_HEADS, nt, nb), 0, 2).swapaxes(1, 2), bf,
                                cache_k, cache_v, cache_lft, l, nb, nt, n_pages, page)
        yf = to_time(yf_s).astype(BF16)
        halo_sc = to_time(state_sconv[l])
        halo_ffn = to_time(state_ffn_conv[l])
        hs, sc_tail = _mix(hs, z, yg, yf, sconv_w[l], wb[l], wo[l], halo_sc, Ts, H_s, nb, 1)
        hs, ffn_tail = _ffn(hs, g2, wu_a[l], wu_g[l], ffn_conv_w[l], cb, wd[l], halo_ffn, fg, Ts, H_s, nb, 1)
        outs_s.append((to_seq(fk).reshape(nb, nt, FOX_HEADS, FOX_HD), to_seq(fv).reshape(nb, nt, FOX_HEADS, FOX_HD),
                       jnp.swapaxes(lfn, 1, 2), s1.reshape(nb, GLA_HEADS, GLA_DK, GLA_DV),
                       to_seq(sc_tail[0]), to_seq(ffn_tail[0])))

    p_states = [jnp.stack(t) for t in zip(*outs_p)]
    s_states = [jnp.stack(t) for t in zip(*outs_s)]
    y_prompt = hp.reshape(B, L, D_MODEL)
    y_sample = to_seq(hs)
    return (y_prompt, y_sample, *p_states, *s_states)
```

```python
import functools

import numpy as np
import jax
import jax.numpy as jnp
from jax import lax
from jax.experimental import pallas as pl
from jax.experimental.pallas import tpu as pltpu

F32 = jnp.float32
BF16 = jnp.bfloat16

D_MODEL = 1024
GLA_HEADS = 4
GLA_DK = 64
GLA_DV = 128
GLA_LR = 16
GLA_TAU = 16.0
FOX_HEADS = 8
FOX_HD = 64
SC_WIDTH = 512
CONV_K = 3
D_FF = 2816
N_BRANCH = 3
EPS = 1e-6

LANES = 128
SUBLANES = 8
VMEM_BYTES = 64 << 20

Z_GATES = 0
Z_GQK = 3072
Z_GV = 3584
Z_GG = 4096
Z_FQ = 4608
Z_FK = 5120
Z_FV = 5632
Z_SC = 6144
Z_COLS = 7680
IN_BN = 512
SMALL_COLS = 128

GLA_CHUNK = 128
NEG = -1e30


def _cparams(sem, vmem_mb):
    return pltpu.CompilerParams(dimension_semantics=sem, vmem_limit_bytes=min(vmem_mb << 20, VMEM_BYTES - (4 << 20)))


def _log_sigmoid(x):
    return jnp.minimum(x, 0.0) - jnp.log1p(jnp.exp(-jnp.abs(x)))


def _dot(a, b):
    return jnp.dot(a, b, preferred_element_type=F32)


def _dot_nt(a, b):
    return lax.dot_general(a, b, (((1,), (1,)), ((), ())), preferred_element_type=F32)


def _dot_tn(a, b):
    return lax.dot_general(a, b, (((0,), (0,)), ((), ())), preferred_element_type=F32)


def _rms(x, g):
    return x * lax.rsqrt(jnp.mean(x * x, axis=-1, keepdims=True) + EPS) * g


def _lane_prefix_sum(c, lane):
    sh = 1
    while sh < LANES:
        c = c + jnp.where(lane >= sh, pltpu.roll(c, sh, axis=1), 0.0)
        sh *= 2
    return c


def _in_proj_body(x_ref, g_ref, w_ref, ws_ref, wft_ref, z_ref, fk_ref, fv_ref, small_ref, fft_ref, xn_scr):
    j = pl.program_id(1)

    @pl.when(j == 0)
    def _():
        xb = _rms(x_ref[...], g_ref[...]).astype(BF16)
        xn_scr[...] = xb
        small_ref[...] = _dot(xb, ws_ref[...])
        fft_ref[...] = _dot_nt(wft_ref[...], xb)

    zt = _dot(xn_scr[...], w_ref[...])
    z_ref[...] = zt.astype(BF16)

    @pl.when(j == Z_FK // IN_BN)
    def _():
        fk_ref[...] = zt

    @pl.when(j == Z_FV // IN_BN)
    def _():
        fv_ref[...] = zt


def _in_proj(x, g, w, ws, wft, bm):
    T = x.shape[0]
    grid = (T // bm, Z_COLS // IN_BN)
    return pl.pallas_call(
        _in_proj_body,
        grid=grid,
        in_specs=[
            pl.BlockSpec((bm, D_MODEL), lambda i, j: (i, 0)),
            pl.BlockSpec((1, D_MODEL), lambda i, j: (0, 0)),
            pl.BlockSpec((D_MODEL, IN_BN), lambda i, j: (0, j)),
            pl.BlockSpec((D_MODEL, SMALL_COLS), lambda i, j: (0, 0)),
            pl.BlockSpec((FOX_HEADS, D_MODEL), lambda i, j: (0, 0)),
        ],
        out_specs=[
            pl.BlockSpec((bm, IN_BN), lambda i, j: (i, j)),
            pl.BlockSpec((bm, FOX_HEADS * FOX_HD), lambda i, j: (i, 0)),
            pl.BlockSpec((bm, FOX_HEADS * FOX_HD), lambda i, j: (i, 0)),
            pl.BlockSpec((bm, SMALL_COLS), lambda i, j: (i, 0)),
            pl.BlockSpec((FOX_HEADS, bm), lambda i, j: (0, i)),
        ],
        out_shape=[
            jax.ShapeDtypeStruct((T, Z_COLS), BF16),
            jax.ShapeDtypeStruct((T, FOX_HEADS * FOX_HD), F32),
            jax.ShapeDtypeStruct((T, FOX_HEADS * FOX_HD), F32),
            jax.ShapeDtypeStruct((T, SMALL_COLS), F32),
            jax.ShapeDtypeStruct((FOX_HEADS, T), F32),
        ],
        scratch_shapes=[pltpu.VMEM((bm, D_MODEL), BF16)],
        compiler_params=_cparams(("arbitrary", "arbitrary"), 40),
    )(x, g, w, ws, wft)


def _gla_consts(C):
    t = np.arange(C)
    r = t[None, :]
    mats = [r <= t[:, None], r > t[:, None]]
    masks = []
    n = C // 2
    while n >= 1:
        pair = t // (2 * n)
        second = (t // n) % 2 == 1
        mid = pair * 2 * n + n
        mats.append(np.where(second[:, None], (r >= mid[:, None]) & (r <= t[:, None]),
                             (r > t[:, None]) & (r < mid[:, None])))
        masks.append(second[:, None] & ~second[None, :] & (pair[:, None] == pair[None, :]))
        n //= 2
    masks.append(np.eye(C, dtype=bool))
    return (jnp.asarray(np.concatenate(mats, 0).astype(np.float32), BF16),
            jnp.asarray(np.stack(masks).astype(np.float32)))


def _gla_prompt_body(qk_ref, v_ref, gg_ref, small_ref, fft_ref, wa_ref, ba_ref, bf_ref, gn_ref, wall_ref,
                     masks_ref, y_ref, st_ref, logf_ref, c_ref, s_scr, c_scr, *, C, nlev):
    i = pl.program_id(1)

    @pl.when(i == 0)
    def _():
        s_scr[...] = jnp.zeros_like(s_scr)
        c_scr[...] = jnp.zeros_like(c_scr)

    lf = _log_sigmoid(fft_ref[...] + bf_ref[...])
    logf_ref[...] = lf
    lane = lax.broadcasted_iota(jnp.int32, lf.shape, 1)
    c = _lane_prefix_sum(lf, lane) + c_scr[...]
    c_ref[...] = c
    c_scr[...] = jnp.broadcast_to(c[:, C - 1:C], c_scr.shape)

    la = _log_sigmoid(_dot(small_ref[...].astype(BF16), wa_ref[...]) + ba_ref[...]) * (1.0 / GLA_TAU)
    hi = la.astype(BF16)
    lo = (la - hi.astype(F32)).astype(BF16)
    wall = wall_ref[...]
    P = jnp.exp(_dot(wall, hi) + _dot(wall, lo))

    W = GLA_HEADS * GLA_DK
    qk = qk_ref[...].astype(F32)
    q = qk[:, :W] * (GLA_DK ** -0.5)
    k = qk[:, W:]
    qd = (q * P[0:C]).astype(BF16)
    kd = (k * P[C:2 * C]).astype(BF16)
    a_last = P[C - 1:C, :]
    row = lax.broadcasted_iota(jnp.int32, (C, 1), 0)
    X = []
    n = C // 2
    for l in range(nlev):
        X.append((jnp.where((row & n) != 0, q, k) * P[(2 + l) * C:(3 + l) * C]).astype(BF16))
        n //= 2
    qb = q.astype(BF16)
    kb = k.astype(BF16)

    lane_p = lax.broadcasted_iota(jnp.int32, (1, LANES), 1)
    for p in range(GLA_HEADS // 2):
        sl = slice(LANES * p, LANES * (p + 1))
        ST = s_scr[p]
        STb = ST.astype(BF16)
        U = []
        for hh in range(2):
            h = 2 * p + hh
            hm = (lane_p >= GLA_DK) == bool(hh)

            def msk(a):
                return jnp.where(hm, a, jnp.zeros_like(a))

            sc = masks_ref[nlev] * _dot_nt(msk(qb[:, sl]), kb[:, sl])
            for l in range(nlev):
                xl = X[l][:, sl]
                sc = sc + masks_ref[l] * _dot_nt(msk(xl), xl)
            vh = v_ref[:, GLA_DV * h:GLA_DV * (h + 1)]
            o = _dot(sc.astype(BF16), vh) + _dot_nt(msk(qd[:, sl]), STb)
            U.append(_dot_tn(vh, kd[:, sl]))
            g = gg_ref[:, GLA_DV * h:GLA_DV * (h + 1)].astype(F32)
            y_ref[:, GLA_DV * h:GLA_DV * (h + 1)] = (_rms(o, gn_ref[...]) * (g * jax.nn.sigmoid(g))).astype(BF16)
        s_scr[p] = ST * a_last[:, sl] + jnp.where(lane_p < GLA_DK, U[0], U[1])

    @pl.when(i == pl.num_programs(1) - 1)
    def _():
        st_ref[0] = s_scr[...]


def _gla_prompt(z, small, fft, wa, ba, bf, gn, B, L):
    C = GLA_CHUNK
    nL = L // C
    wall, masks = _gla_consts(C)
    nlev = masks.shape[0] - 1
    T = B * L
    W = GLA_HEADS * GLA_DK
    row = lambda b, i: b * nL + i
    return pl.pallas_call(
        functools.partial(_gla_prompt_body, C=C, nlev=nlev),
        grid=(B, nL),
        in_specs=[
            pl.BlockSpec((C, 2 * W), lambda b, i: (row(b, i), Z_GQK // (2 * W))),
            pl.BlockSpec((C, GLA_HEADS * GLA_DV), lambda b, i: (row(b, i), Z_GV // (GLA_HEADS * GLA_DV))),
            pl.BlockSpec((C, GLA_HEADS * GLA_DV), lambda b, i: (row(b, i), Z_GG // (GLA_HEADS * GLA_DV))),
            pl.BlockSpec((C, SMALL_COLS), lambda b, i: (row(b, i), 0)),
            pl.BlockSpec((FOX_HEADS, C), lambda b, i: (0, row(b, i))),
            pl.BlockSpec((SMALL_COLS, W), lambda b, i: (0, 0)),
            pl.BlockSpec((1, W), lambda b, i: (0, 0)),
            pl.BlockSpec((FOX_HEADS, 1), lambda b, i: (0, 0)),
            pl.BlockSpec((1, GLA_DV), lambda b, i: (0, 0)),
            pl.BlockSpec(wall.shape, lambda b, i: (0, 0)),
            pl.BlockSpec(masks.shape, lambda b, i: (0, 0, 0)),
        ],
        out_specs=[
            pl.BlockSpec((C, GLA_HEADS * GLA_DV), lambda b, i: (row(b, i), 0)),
            pl.BlockSpec((1, GLA_HEADS // 2, GLA_DV, 2 * GLA_DK), lambda b, i: (b, 0, 0, 0)),
            pl.BlockSpec((FOX_HEADS, C), lambda b, i: (0, row(b, i))),
            pl.BlockSpec((FOX_HEADS, C), lambda b, i: (0, row(b, i))),
        ],
        out_shape=[
            jax.ShapeDtypeStruct((T, GLA_HEADS * GLA_DV), BF16),
            jax.ShapeDtypeStruct((B, GLA_HEADS // 2, GLA_DV, 2 * GLA_DK), F32),
            jax.ShapeDtypeStruct((FOX_HEADS, T), F32),
            jax.ShapeDtypeStruct((FOX_HEADS, T), F32),
        ],
        scratch_shapes=[pltpu.VMEM((GLA_HEADS // 2, GLA_DV, 2 * GLA_DK), F32),
                        pltpu.VMEM((FOX_HEADS, LANES), F32)],
        compiler_params=_cparams(("arbitrary", "arbitrary"), 32),
    )(z, z, z, small, fft, wa, ba, bf, gn, wall, masks)


def _gla_sample_body(q_ref, k_ref, v_ref, gg_ref, small_ref, wa_ref, ba_ref, gn_ref, s0_ref, y_ref, s1_ref,
                     *, nb, nt):
    la = _log_sigmoid(_dot(small_ref[...].astype(BF16), wa_ref[0]) + ba_ref[0]) * (1.0 / GLA_TAU)
    q = q_ref[0].astype(F32) * (GLA_DK ** -0.5)
    k = k_ref[0].astype(F32)
    rows = [slice(t * nb, (t + 1) * nb) for t in range(nt)]
    b = []
    for t in range(nt):
        b.append(la[rows[t]] if t == 0 else b[-1] + la[rows[t]])
    qt = [q[r] for r in rows]
    kt = [k[r] for r in rows]
    qd = [qt[t] * jnp.exp(b[t]) for t in range(nt)]
    kd = [kt[t] * jnp.exp(b[nt - 1] - b[t]) for t in range(nt)]
    a_last = jnp.exp(b[nt - 1])

    for d in range(GLA_DK):
        cs = slice(d * GLA_DV, (d + 1) * GLA_DV)
        acc = a_last[:, d:d + 1] * s0_ref[:, cs]
        for t in range(nt):
            acc = acc + kd[t][:, d:d + 1] * v_ref[rows[t], :].astype(F32)
        s1_ref[:, cs] = acc

    for t in range(nt):
        o = jnp.zeros((nb, GLA_DV), F32)
        for s in range(t + 1):
            w = jnp.sum(qt[t] * kt[s] * jnp.exp(b[t] - b[s]), axis=-1, keepdims=True)
            o = o + w * v_ref[rows[s], :].astype(F32)
        for d in range(GLA_DK):
            o = o + qd[t][:, d:d + 1] * s0_ref[:, d * GLA_DV:(d + 1) * GLA_DV]
        g = gg_ref[rows[t], :].astype(F32)
        y_ref[rows[t], :] = (_rms(o, gn_ref[...]) * (g * jax.nn.sigmoid(g))).astype(BF16)


def _gla_sample(qh, kh, z, small, wa_h, ba_h, gn, s0, nb, nt):
    T = nb * nt
    slab = GLA_DK * GLA_DV
    return pl.pallas_call(
        functools.partial(_gla_sample_body, nb=nb, nt=nt),
        grid=(GLA_HEADS,),
        in_specs=[
            pl.BlockSpec((1, T, GLA_DK), lambda h: (h, 0, 0)),
            pl.BlockSpec((1, T, GLA_DK), lambda h: (h, 0, 0)),
            pl.BlockSpec((T, GLA_DV), lambda h: (0, Z_GV // GLA_DV + h)),
            pl.BlockSpec((T, GLA_DV), lambda h: (0, Z_GG // GLA_DV + h)),
            pl.BlockSpec((T, SMALL_COLS), lambda h: (0, 0)),
            pl.BlockSpec((1, SMALL_COLS, GLA_DK), lambda h: (h, 0, 0)),
            pl.BlockSpec((1, 1, GLA_DK), lambda h: (h, 0, 0)),
            pl.BlockSpec((1, GLA_DV), lambda h: (0, 0)),
            pl.BlockSpec((nb, slab), lambda h: (0, h)),
        ],
        out_specs=[
            pl.BlockSpec((T, GLA_DV), lambda h: (0, h)),
            pl.BlockSpec((nb, slab), lambda h: (0, h)),
        ],
        out_shape=[
            jax.ShapeDtypeStruct((T, GLA_HEADS * GLA_DV), BF16),
            jax.ShapeDtypeStruct((nb, GLA_HEADS * slab), F32),
        ],
        compiler_params=_cparams(("arbitrary",), 40),
    )(qh, kh, z, z, small, wa_h, ba_h, gn, s0)


def _split3(c):
    def top(x):
        bits = lax.bitcast_convert_type(x, jnp.uint32) & jnp.uint32(0xFFFF0000)
        return lax.bitcast_convert_type(bits, F32)

    hi = top(c)
    mid = top(c - hi)
    return hi, mid, c - hi - mid


FOX_PREP_ROWS = 512


def _fox_prompt_body(q_ref, k_ref, v_ref, ck_ref, cq_ref, o_ref, k_scr, vt_scr, s_scr, *, bq, bk):
    i = pl.program_id(2)
    row0 = i * bq
    n_full = lax.div(row0, bk)
    lane = lax.broadcasted_iota(jnp.int32, (1, LANES), 1)
    NB = 3

    @pl.when(i == 0)
    def _():
        ch = min(FOX_PREP_ROWS, k_ref.shape[0])

        def prep(r, carry):
            start = pl.multiple_of(r * ch, ch)
            kc = k_ref[pl.ds(start, ch), :].astype(F32)
            cc = ck_ref[0, pl.ds(start, ch), :]
            vt = v_ref[pl.ds(start, ch), :].astype(F32).T
            ones = jnp.ones((LANES - FOX_HD, ch), F32)
            for hh in range(2):
                kh = kc if hh == 0 else pltpu.roll(kc, FOX_HD, axis=1)
                tile = jnp.where(lane < FOX_HD + NB, 1.0, 0.0)
                for n, term in enumerate(_split3(cc[:, hh:hh + 1])):
                    tile = jnp.where(lane == FOX_HD + NB + n, -term, tile)
                k_scr[hh, pl.ds(start, ch), :] = jnp.where(lane < FOX_HD, kh, tile).astype(BF16)
                vt_scr[hh, :, pl.ds(start, ch)] = jnp.concatenate(
                    [vt[hh * FOX_HD:(hh + 1) * FOX_HD], ones], axis=0).astype(BF16)
            return carry

        lax.fori_loop(0, k_ref.shape[0] // ch, prep, 0)

    qt = (q_ref[...].astype(F32) * (FOX_HD ** -0.5)).T
    row = lax.broadcasted_iota(jnp.int32, (LANES, 1), 0)
    qts = []
    for hh in range(2):
        tile = jnp.concatenate([qt[hh * FOX_HD:(hh + 1) * FOX_HD], jnp.zeros((LANES - FOX_HD, bq), F32)], axis=0)
        tile = jnp.where((row >= FOX_HD + NB) & (row < FOX_HD + 2 * NB), 1.0, tile)
        for n, term in enumerate(_split3(cq_ref[0, hh:hh + 1, :])):
            tile = jnp.where(row == FOX_HD + n, term, tile)
        qts.append(tile.astype(BF16))

    def scores(j):
        start = pl.multiple_of(j * bk, bk)
        return tuple(_dot(k_scr[hh, pl.ds(start, bk), :], qts[hh]) for hh in range(2))

    def softmax(j, s, m, masked):
        out = []
        for hh in range(2):
            sh = s[hh]
            if masked:
                key = j * bk + lax.broadcasted_iota(jnp.int32, (bk, bq), 0)
                qry = row0 + lax.broadcasted_iota(jnp.int32, (bk, bq), 1)
                sh = jnp.where(key <= qry, sh, NEG)
            m_new = jnp.maximum(m[hh], jnp.max(sh, axis=0, keepdims=True))
            out.append((m_new, jnp.exp(sh - m_new).astype(BF16), jnp.exp(m[hh] - m_new)))
        return tuple(zip(*out))

    def accumulate(j, p, alpha, acc):
        start = pl.multiple_of(j * bk, bk)
        return tuple(alpha[hh] * acc[hh] + _dot(vt_scr[hh, :, pl.ds(start, bk)], p[hh]) for hh in range(2))

    def put_scores(j):
        for hh, sh in enumerate(scores(j)):
            s_scr[hh] = sh

    def step(j, carry):
        m, acc = carry
        m, p, alpha = softmax(j, (s_scr[0], s_scr[1]), m, False)
        put_scores(j + 1)
        return m, accumulate(j, p, alpha, acc)

    two = lambda a: (a, a)
    put_scores(0)
    m, acc = lax.fori_loop(0, n_full, step, (two(jnp.full((1, bq), NEG, F32)), two(jnp.zeros((LANES, bq), F32))))
    m, p, alpha = softmax(n_full, (s_scr[0], s_scr[1]), m, True)
    acc = accumulate(n_full, p, alpha, acc)
    o = jnp.concatenate([a[:FOX_HD] / a[FOX_HD:FOX_HD + 1] for a in acc], axis=0)
    o_ref[...] = o.T.astype(BF16)


def _fox_prompt(z, c_cols, c_rows, B, L, bq, bk):
    nq = L // bq
    T = B * L
    npair = FOX_HEADS // 2
    return pl.pallas_call(
        functools.partial(_fox_prompt_body, bq=bq, bk=bk),
        grid=(B, npair, nq),
        in_specs=[
            pl.BlockSpec((bq, LANES), lambda b, p, i: (b * nq + i, Z_FQ // LANES + p)),
            pl.BlockSpec((L, LANES), lambda b, p, i: (b, Z_FK // LANES + p)),
            pl.BlockSpec((L, LANES), lambda b, p, i: (b, Z_FV // LANES + p)),
            pl.BlockSpec((1, L, 2), lambda b, p, i: (p, b, 0)),
            pl.BlockSpec((1, 2, bq), lambda b, p, i: (p, 0, b * nq + i)),
        ],
        out_specs=pl.BlockSpec((bq, LANES), lambda b, p, i: (b * nq + i, p)),
        out_shape=jax.ShapeDtypeStruct((T, FOX_HEADS * FOX_HD), BF16),
        scratch_shapes=[pltpu.VMEM((2, L, LANES), BF16), pltpu.VMEM((2, LANES, L), BF16),
                        pltpu.VMEM((2, bk, bq), F32)],
        compiler_params=_cparams(("arbitrary", "arbitrary", "arbitrary"), 56),
    )(z, z, z, c_cols, c_rows)


FOX_PAGES_PER_STEP = 8


def _fox_sample_body(pt_ref, q_ref, kn_ref, vn_ref, fft_ref, bf_ref, *rest, pg, nt, page):
    del pt_ref
    k_refs = rest[:pg]
    v_refs = rest[pg:2 * pg]
    lf_refs = rest[2 * pg:3 * pg]
    o_ref, lfo_ref, m_scr, den_scr, acc_scr, cc_scr = rest[3 * pg:]
    H = FOX_HEADS
    R = page * H
    NQ = nt * H
    g = pl.program_id(1)

    @pl.when(g == 0)
    def _():
        m_scr[...] = jnp.full(m_scr.shape, NEG, F32)
        den_scr[...] = jnp.zeros_like(den_scr)
        acc_scr[...] = jnp.zeros_like(acc_scr)
        cc_scr[...] = jnp.zeros_like(cc_scr)

    q32 = q_ref[0] * (FOX_HD ** -0.5)
    qb = q32.astype(BF16)
    lane = lax.broadcasted_iota(jnp.int32, (SUBLANES, LANES), 1)
    srow = lax.broadcasted_iota(jnp.int32, (SUBLANES, LANES), 0)
    valid = ((lax.broadcasted_iota(jnp.int32, (NQ, R), 1) & (H - 1))
             == (lax.broadcasted_iota(jnp.int32, (NQ, R), 0) & (H - 1)))

    carry = cc_scr[...]
    logits = []
    for j in range(pg):
        x = lf_refs[j][...]
        sh = H
        while sh < LANES:
            x = x + jnp.where(lane >= sh, pltpu.roll(x, sh, axis=1), 0.0)
            sh *= 2
        y = jnp.where(lane >= LANES - H, x, 0.0)
        sh = H
        while sh < LANES:
            y = y + pltpu.roll(y, LANES - sh, axis=1)
            sh *= 2
        inc = y
        sh = 1
        while sh < SUBLANES:
            inc = inc + jnp.where(srow >= sh, pltpu.roll(inc, sh, axis=0), 0.0)
            sh *= 2
        c = x + (inc - y) + carry
        carry = carry + inc[SUBLANES - 1:SUBLANES, :]
        ck = jnp.concatenate([c[r:r + 1, :] for r in range(SUBLANES)], axis=1)
        s = _dot_nt(qb, k_refs[j][...].reshape(R, FOX_HD).astype(BF16))
        logits.append(jnp.where(valid, s - ck, NEG))
    cc_scr[...] = carry

    m_old = m_scr[...]
    m_new = m_old
    for lg in logits:
        m_new = jnp.maximum(m_new, jnp.max(lg, axis=-1, keepdims=True))
    alpha = jnp.exp(m_old - m_new)
    den = alpha * den_scr[...]
    acc = alpha * acc_scr[...]
    for j in range(pg):
        p = jnp.exp(logits[j] - m_new)
        den = den + jnp.sum(p, axis=-1, keepdims=True)
        acc = acc + _dot(p.astype(BF16), v_refs[j][...].reshape(R, FOX_HD).astype(BF16))
    m_scr[...] = m_new
    den_scr[...] = den
    acc_scr[...] = acc

    @pl.when(g == pl.num_programs(1) - 1)
    def _():
        eye = lax.broadcasted_iota(jnp.int32, (H, H), 0) == lax.broadcasted_iota(jnp.int32, (H, H), 1)
        c_last = jnp.sum(jnp.where(eye, jnp.broadcast_to(carry[:, :H], (H, H)), 0.0), axis=-1, keepdims=True)
        lfn = _log_sigmoid(fft_ref[0] + bf_ref[...])
        lfo_ref[0] = lfn
        lane_t = lax.broadcasted_iota(jnp.int32, (H, nt), 1)
        cn = c_last
        for j in range(nt):
            cn = cn + jnp.where(lane_t >= j, lfn[:, j:j + 1], 0.0)
        cq = jnp.concatenate([cn[:, t:t + 1] for t in range(nt)], axis=0)
        trow = lax.broadcasted_iota(jnp.int32, (NQ, 1), 0) >> (H.bit_length() - 1)
        kn = kn_ref[0]
        vn = vn_ref[0]
        m_t = m_new + cq
        s_new = []
        m_f = m_t
        for tp in range(nt):
            kt = jnp.concatenate([kn[tp * H:(tp + 1) * H]] * nt, axis=0)
            ckn = jnp.concatenate([cn[:, tp:tp + 1]] * nt, axis=0)
            sn = jnp.sum(q32 * kt, axis=-1, keepdims=True) + (cq - ckn)
            sn = jnp.where(trow >= tp, sn, NEG)
            s_new.append(sn)
            m_f = jnp.maximum(m_f, sn)
        scale = jnp.exp(m_t - m_f)
        den_f = scale * den
        acc_f = scale * acc
        for tp in range(nt):
            pn = jnp.exp(s_new[tp] - m_f)
            den_f = den_f + pn
            acc_f = acc_f + pn * jnp.concatenate([vn[tp * H:(tp + 1) * H]] * nt, axis=0)
        o_ref[0] = acc_f / den_f


def _fox_sample(pt_flat, q_s, kn_s, vn_s, fft_s, bf, cache_k, cache_v, cache_lf, layer, nb, nt, n_pages, page):
    H = FOX_HEADS
    pg = min(FOX_PAGES_PER_STEP, n_pages)
    assert n_pages % pg == 0 and page * H == SUBLANES * LANES
    NQ = nt * H

    def page_spec(shape, j):
        zeros = (0,) * len(shape)
        return pl.BlockSpec((None, None) + shape, lambda s, g, pt: (layer, pt[s * n_pages + g * pg + j]) + zeros)

    seq3 = lambda s, g, pt: (s, 0, 0)
    in_specs = [
        pl.BlockSpec((1, NQ, FOX_HD), seq3),
        pl.BlockSpec((1, NQ, FOX_HD), seq3),
        pl.BlockSpec((1, NQ, FOX_HD), seq3),
        pl.BlockSpec((1, H, nt), seq3),
        pl.BlockSpec((H, 1), lambda s, g, pt: (0, 0)),
    ]
    in_specs += [page_spec((page, H, FOX_HD), j) for j in range(pg)]
    in_specs += [page_spec((page, H, FOX_HD), j) for j in range(pg)]
    in_specs += [page_spec((SUBLANES, LANES), j) for j in range(pg)]
    grid_spec = pltpu.PrefetchScalarGridSpec(
        num_scalar_prefetch=1,
        grid=(nb, n_pages // pg),
        in_specs=in_specs,
        out_specs=[pl.BlockSpec((1, NQ, FOX_HD), seq3), pl.BlockSpec((1, H, nt), seq3)],
        scratch_shapes=[pltpu.VMEM((NQ, 1), F32), pltpu.VMEM((NQ, 1), F32), pltpu.VMEM((NQ, FOX_HD), F32),
                        pltpu.VMEM((1, LANES), F32)],
    )
    return pl.pallas_call(
        functools.partial(_fox_sample_body, pg=pg, nt=nt, page=page),
        grid_spec=grid_spec,
        out_shape=[jax.ShapeDtypeStruct((nb, NQ, FOX_HD), F32), jax.ShapeDtypeStruct((nb, H, nt), F32)],
        compiler_params=_cparams(("arbitrary", "arbitrary"), 48),
    )(pt_flat, q_s, kn_s, vn_s, fft_s, bf, *([cache_k] * pg), *([cache_v] * pg), *([cache_lf] * pg))


def _causal_conv(ext_scr, u, cw_ref, H, shift):
    bm = u.shape[0]
    ext_scr[H:H + bm, :] = u
    out = cw_ref[CONV_K - 1:CONV_K, :] * u
    for i in range(1, CONV_K):
        lo = H - i * shift
        out = out + cw_ref[CONV_K - 1 - i:CONV_K - i, :] * ext_scr[lo:lo + bm, :]
    return out


def _mix_body(*refs, H, shift, tiles_per_seq, has_state):
    if has_state:
        h_ref, gates_ref, sc_ref, yg_ref, yf_ref, cw_ref, wb_ref, wo_ref, halo_ref, hout_ref, tail_ref, ext_scr = refs
    else:
        h_ref, gates_ref, sc_ref, yg_ref, yf_ref, cw_ref, wb_ref, wo_ref, hout_ref, tail_ref, ext_scr = refs
    bm = h_ref.shape[0]
    if has_state:
        ext_scr[0:H, :] = halo_ref[...]
    else:
        seq_start = pl.program_id(0) % tiles_per_seq == 0

        @pl.when(seq_start)
        def _():
            ext_scr[0:H, :] = jnp.zeros((H, SC_WIDTH), F32)

        @pl.when(jnp.logical_not(seq_start))
        def _():
            ext_scr[0:H, :] = ext_scr[bm:bm + H, :]

    sc = sc_ref[...].astype(F32)
    u = sc[:, SC_WIDTH:2 * SC_WIDTH] * sc[:, 2 * SC_WIDTH:]
    y_sc = (sc[:, :SC_WIDTH] * _causal_conv(ext_scr, u, cw_ref, H, shift)).astype(BF16)
    tail_ref[0] = ext_scr[bm:bm + H, :]

    acc = jnp.zeros((bm, D_MODEL), F32)
    for b, y in enumerate((yg_ref[...], yf_ref[...], y_sc)):
        g = jax.nn.sigmoid(gates_ref[:, b * D_MODEL:(b + 1) * D_MODEL].astype(F32))
        acc = acc + g * _dot(y, wb_ref[b])
    hout_ref[...] = h_ref[...] + _dot(acc.astype(BF16), wo_ref[...])


def _mix(h, z, yg, yf, cw, wb, wo, halo, bm, H, shift, tiles_per_seq):
    T = h.shape[0]
    has_state = halo is not None
    in_specs = [
        pl.BlockSpec((bm, D_MODEL), lambda i: (i, 0)),
        pl.BlockSpec((bm, N_BRANCH * D_MODEL), lambda i: (i, 0)),
        pl.BlockSpec((bm, 3 * SC_WIDTH), lambda i: (i, Z_SC // (3 * SC_WIDTH))),
        pl.BlockSpec((bm, SC_WIDTH), lambda i: (i, 0)),
        pl.BlockSpec((bm, SC_WIDTH), lambda i: (i, 0)),
        pl.BlockSpec((CONV_K, SC_WIDTH), lambda i: (0, 0)),
        pl.BlockSpec((N_BRANCH, SC_WIDTH, D_MODEL), lambda i: (0, 0, 0)),
        pl.BlockSpec((D_MODEL, D_MODEL), lambda i: (0, 0)),
    ]
    args = [h, z, z, yg, yf, cw, wb, wo]
    if has_state:
        in_specs.append(pl.BlockSpec((H, SC_WIDTH), lambda i: (0, 0)))
        args.append(halo)
    return pl.pallas_call(
        functools.partial(_mix_body, H=H, shift=shift, tiles_per_seq=tiles_per_seq, has_state=has_state),
        grid=(T // bm,),
        in_specs=in_specs,
        out_specs=[pl.BlockSpec((bm, D_MODEL), lambda i: (i, 0)),
                   pl.BlockSpec((1, H, SC_WIDTH), lambda i: (i, 0, 0))],
        out_shape=[jax.ShapeDtypeStruct((T, D_MODEL), F32),
                   jax.ShapeDtypeStruct((T // bm, H, SC_WIDTH), F32)],
        scratch_shapes=[pltpu.VMEM((H + bm, SC_WIDTH), F32)],
        compiler_params=_cparams(("arbitrary",), 48),
    )(*args)


FFN_CK = 256


def _ffn_body(*refs, H, shift, tiles_per_seq, has_state, final_norm):
    refs = list(refs)
    h_ref, g_ref, wa_ref, wg_ref, cw_ref, cb_ref, wd_ref = refs[:7]
    refs = refs[7:]
    halo_ref = refs.pop(0) if has_state else None
    fg_ref = refs.pop(0) if final_norm else None
    out_ref, tail_ref, xn_scr, acc_scr, ext_scr, halo_scr = refs
    i = pl.program_id(0)
    c = pl.program_id(1)
    bm = h_ref.shape[0]

    @pl.when(c == 0)
    def _():
        xn_scr[...] = _rms(h_ref[...], g_ref[...]).astype(BF16)
        acc_scr[...] = jnp.zeros_like(acc_scr)

    xn = xn_scr[...]
    a = _dot(xn, wa_ref[...])
    gate = _dot(xn, wg_ref[...])
    if has_state:
        ext_scr[0:H, :] = halo_ref[...]
    else:
        seq_start = i % tiles_per_seq == 0

        @pl.when(seq_start)
        def _():
            ext_scr[0:H, :] = jnp.zeros((H, a.shape[1]), F32)

        @pl.when(jnp.logical_not(seq_start))
        def _():
            ext_scr[0:H, :] = halo_scr[c]

    ac =_causal_conv(ext_scr, a, cw_ref, H, shift) + cb_ref[...]
    tail = ext_scr[bm:bm + H, :]
    tail_ref[0] = tail
    if not has_state:
        halo_scr[c] = tail
    act = 0.5 * ac * (1.0 + lax.erf(ac * (2.0 ** -0.5))) * gate
    acc_scr[...] += _dot(act.astype(BF16), wd_ref[...])

    @pl.when(c == pl.num_programs(1) - 1)
    def _():
        hn = h_ref[...] + acc_scr[...]
        out_ref[...] = _rms(hn, fg_ref[...]) if final_norm else hn


def _ffn(h, g, wa, wg, cw, cb, wd, halo, fg, bm, H, shift, tiles_per_seq):
    T = h.shape[0]
    ck = FFN_CK
    nc = D_FF // ck
    has_state = halo is not None
    final_norm = fg is not None
    in_specs = [
        pl.BlockSpec((bm, D_MODEL), lambda i, c: (i, 0)),
        pl.BlockSpec((1, D_MODEL), lambda i, c: (0, 0)),
        pl.BlockSpec((D_MODEL, ck), lambda i, c: (0, c)),
        pl.BlockSpec((D_MODEL, ck), lambda i, c: (0, c)),
        pl.BlockSpec((CONV_K, ck), lambda i, c: (0, c)),
        pl.BlockSpec((1, ck), lambda i, c: (0, c)),
        pl.BlockSpec((ck, D_MODEL), lambda i, c: (c, 0)),
    ]
    args = [h, g, wa, wg, cw, cb, wd]
    if has_state:
        in_specs.append(pl.BlockSpec((H, ck), lambda i, c: (0, c)))
        args.append(halo)
    if final_norm:
        in_specs.append(pl.BlockSpec((1, D_MODEL), lambda i, c: (0, 0)))
        args.append(fg)
    return pl.pallas_call(
        functools.partial(_ffn_body, H=H, shift=shift, tiles_per_seq=tiles_per_seq, has_state=has_state,
                          final_norm=final_norm),
        grid=(T // bm, nc),
        in_specs=in_specs,
        out_specs=[pl.BlockSpec((bm, D_MODEL), lambda i, c: (i, 0)),
                   pl.BlockSpec((1, H, ck), lambda i, c: (i, 0, c))],
        out_shape=[jax.ShapeDtypeStruct((T, D_MODEL), F32),
                   jax.ShapeDtypeStruct((T // bm, H, D_FF), F32)],
        scratch_shapes=[pltpu.VMEM((bm, D_MODEL), BF16), pltpu.VMEM((bm, D_MODEL), F32),
                        pltpu.VMEM((H + bm, ck), F32), pltpu.VMEM((nc, H, ck), F32)],
        compiler_params=_cparams(("arbitrary", "arbitrary"), 48),
    )(*args)


def _pack_in_proj(w_in):
    sizes = (GLA_HEADS * GLA_DK, GLA_HEADS * GLA_DK, GLA_HEADS * GLA_DV, GLA_LR, GLA_HEADS * GLA_DV,
             FOX_HEADS * FOX_HD, FOX_HEADS * FOX_HD, FOX_HEADS * FOX_HD, FOX_HEADS,
             SC_WIDTH, SC_WIDTH, SC_WIDTH, N_BRANCH * D_MODEL)
    offs = np.concatenate([[0], np.cumsum(sizes)])
    part = lambda n: w_in[:, :, offs[n]:offs[n + 1]]
    gq, gk, gv, glr, gg, fq, fk, fv, ff, scb, scc, sch, gates = [part(n) for n in range(len(sizes))]
    w_main = jnp.concatenate([gates, gq, gk, gv, gg, fq, fk, fv, scb, scc, sch], axis=-1).astype(BF16)
    pad = jnp.zeros(w_in.shape[:2] + (SMALL_COLS - GLA_LR - FOX_HEADS,), w_in.dtype)
    w_small = jnp.concatenate([glr, ff, pad], axis=-1).astype(BF16)
    w_fft = jnp.swapaxes(ff, 1, 2).astype(BF16)
    return w_main, w_small, w_fft


def kernel(x_prompt, x_sample, cache_fox_k, cache_fox_v, cache_fox_logf, state_gla, state_sconv, state_ffn_conv,
           page_table, norm1_g, w_in, w_alpha2, b_alpha, b_forget, gla_norm_g, sconv_w, w_branch, w_out, norm2_g,
           w_up, ffn_conv_w, ffn_conv_b, w_down, final_norm_g):
    depth = w_in.shape[0]
    B, L, _ = x_prompt.shape
    nb, nt, _ = x_sample.shape
    n_pool, page = cache_fox_k.shape[1], cache_fox_k.shape[2]
    n_pages = page_table.shape[1]
    W = FOX_HEADS * FOX_HD
    Tp = B * L
    Ts = nb * nt
    bm_p = min(512, L)
    fox_bq, fox_bk = min(256, L), min(512, L)
    H_p, H_s = SUBLANES, (CONV_K - 1) * nb

    w_main, w_small, w_fft = _pack_in_proj(w_in)
    wa_pad = jnp.concatenate(
        [w_alpha2, jnp.zeros((depth, SMALL_COLS - GLA_LR, GLA_HEADS * GLA_DK), w_alpha2.dtype)], axis=1).astype(BF16)
    wa_heads = wa_pad.reshape(depth, SMALL_COLS, GLA_HEADS, GLA_DK).transpose(0, 2, 1, 3)
    ba_heads = b_alpha.reshape(depth, GLA_HEADS, 1, GLA_DK)
    wb = w_branch.astype(BF16)
    wo = w_out.astype(BF16)
    wu_a = w_up[:, :, :D_FF].astype(BF16)
    wu_g = w_up[:, :, D_FF:].astype(BF16)
    wd = w_down.astype(BF16)

    cache_lf = cache_fox_logf.reshape(depth, n_pool, page * FOX_HEADS // LANES, LANES)
    pt_flat = page_table.reshape(-1).astype(jnp.int32)

    def to_seq(a):
        return jnp.swapaxes(a.reshape((a.shape[0] // nb, nb) + a.shape[1:]), 0, 1)

    def to_time(a):
        return jnp.swapaxes(a, 0, 1).reshape((a.shape[0] * a.shape[1],) + a.shape[2:])

    hp = x_prompt.reshape(Tp, D_MODEL)
    hs = to_time(x_sample)
    outs_p, outs_s = [], []
    for l in range(depth):
        g1 = norm1_g[l].reshape(1, D_MODEL)
        g2 = norm2_g[l].reshape(1, D_MODEL)
        bf = b_forget[l].reshape(FOX_HEADS, 1)
        ba = b_alpha[l].reshape(1, -1)
        gn = gla_norm_g[l].reshape(1, GLA_DV)
        cb = ffn_conv_b[l].reshape(1, D_FF)
        fg = final_norm_g.reshape(1, D_MODEL) if l == depth - 1 else None

        z, fk, fv, small, fft = _in_proj(hp, g1, w_main[l], w_small[l], w_fft[l], bm_p)
        yg, st, logf_t, c_t = _gla_prompt(z, small, fft, wa_pad[l], ba, bf, gn, B, L)
        c_rows = c_t.reshape(FOX_HEADS // 2, 2, Tp)
        yf = _fox_prompt(z, jnp.swapaxes(c_rows, 1, 2), c_rows, B, L, fox_bq, fox_bk)
        hp, sc_tail = _mix(hp, z, yg, yf, sconv_w[l], wb[l], wo[l], None, bm_p, H_p, 1, L // bm_p)
        hp, ffn_tail = _ffn(hp, g2, wu_a[l], wu_g[l], ffn_conv_w[l], cb, wd[l], None, fg, bm_p, H_p, 1, L // bm_p)
        last = lambda t: t.reshape(B, L // bm_p, H_p, -1)[:, -1, H_p - (CONV_K - 1):, :]
        p_gla = st.reshape(B, GLA_HEADS // 2, GLA_DV, 2, GLA_DK).transpose(0, 1, 3, 4, 2)
        outs_p.append((fk.reshape(B, L, FOX_HEADS, FOX_HD), fv.reshape(B, L, FOX_HEADS, FOX_HD),
                       logf_t.T.reshape(B, L, FOX_HEADS), p_gla.reshape(B, GLA_HEADS, GLA_DK, GLA_DV),
                       last(sc_tail), last(ffn_tail)))

        z, fk, fv, small, fft = _in_proj(hs, g1, w_main[l], w_small[l], w_fft[l], Ts)
        qk_h = z[:, Z_GQK:Z_GV].reshape(Ts, 2, GLA_HEADS, GLA_DK).transpose(1, 2, 0, 3)
        yg, s1 = _gla_sample(qk_h[0], qk_h[1], z, small, wa_heads[l], ba_heads[l], gn,
                             state_gla[l].reshape(nb, -1), nb, nt)
        rows = lambda a: to_seq(a).reshape(nb, nt * FOX_HEADS, FOX_HD)
        yf_s, lfn = _fox_sample(pt_flat, rows(z[:, Z_FQ:Z_FK].astype(F32)), rows(fk), rows(fv),
                                jnp.swapaxes(fft.reshape(FOX_HEADS, nt, nb), 0, 2).swapaxes(1, 2), bf,
                                cache_fox_k, cache_fox_v, cache_lf, l, nb, nt, n_pages, page)
        yf = to_time(yf_s.reshape(nb, nt, W)).astype(BF16)
        halo_sc = to_time(state_sconv[l])
        halo_ffn = to_time(state_ffn_conv[l])
        hs, sc_tail = _mix(hs, z, yg, yf, sconv_w[l], wb[l], wo[l], halo_sc, Ts, H_s, nb, 1)
        hs, ffn_tail = _ffn(hs, g2, wu_a[l], wu_g[l], ffn_conv_w[l], cb, wd[l], halo_ffn, fg, Ts, H_s, nb, 1)
        outs_s.append((to_seq(fk).reshape(nb, nt, FOX_HEADS, FOX_HD), to_seq(fv).reshape(nb, nt, FOX_HEADS, FOX_HD),
                       jnp.swapaxes(lfn, 1, 2), s1.reshape(nb, GLA_HEADS, GLA_DK, GLA_DV),
                       to_seq(sc_tail[0]), to_seq(ffn_tail[0])))

    p_states = [jnp.stack(t) for t in zip(*outs_p)]
    s_states = [jnp.stack(t) for t in zip(*outs_s)]
    y_prompt = hp.reshape(B, L, D_MODEL)
    y_sample = to_seq(hs)
    return (y_prompt, y_sample, *p_states, *s_states)
```

```python
import functools

import numpy as np
import jax
import jax.numpy as jnp
from jax import lax
from jax.experimental import pallas as pl
from jax.experimental.pallas import tpu as pltpu

F32 = jnp.float32
BF16 = jnp.bfloat16

D_MODEL = 1024
GLA_HEADS = 4
GLA_DK = 64
GLA_DV = 128
GLA_LR = 16
GLA_TAU = 16.0
FOX_HEADS = 8
FOX_HD = 64
SC_WIDTH = 512
CONV_K = 3
D_FF = 2816
N_BRANCH = 3
EPS = 1e-6

LANES = 128
SUBLANES = 8
VMEM_BYTES = 64 << 20

Z_GATES = 0
Z_GQK = 3072
Z_GV = 3584
Z_GG = 4096
Z_FQ = 4608
Z_FK = 5120
Z_FV = 5632
Z_SC = 6144
Z_COLS = 7680
IN_BN = 512
SMALL_COLS = 128

GLA_CHUNK = 128
NEG = -1e30


def _cparams(sem, vmem_mb):
    return pltpu.CompilerParams(dimension_semantics=sem, vmem_limit_bytes=min(vmem_mb << 20, VMEM_BYTES - (4 << 20)))


def _log_sigmoid(x):
    return jnp.minimum(x, 0.0) - jnp.log1p(jnp.exp(-jnp.abs(x)))


def _dot(a, b):
    return jnp.dot(a, b, preferred_element_type=F32)


def _dot_nt(a, b):
    return lax.dot_general(a, b, (((1,), (1,)), ((), ())), preferred_element_type=F32)


def _dot_tn(a, b):
    return lax.dot_general(a, b, (((0,), (0,)), ((), ())), preferred_element_type=F32)


def _rms(x, g):
    return x * lax.rsqrt(jnp.mean(x * x, axis=-1, keepdims=True) + EPS) * g


def _lane_prefix_sum(c, lane):
    sh = 1
    while sh < LANES:
        c = c + jnp.where(lane >= sh, pltpu.roll(c, sh, axis=1), 0.0)
        sh *= 2
    return c


def _in_proj_body(x_ref, g_ref, w_ref, ws_ref, wft_ref, z_ref, fk_ref, fv_ref, small_ref, fft_ref, xn_scr):
    j = pl.program_id(1)

    @pl.when(j == 0)
    def _():
        xb = _rms(x_ref[...], g_ref[...]).astype(BF16)
        xn_scr[...] = xb
        small_ref[...] = _dot(xb, ws_ref[...])
        fft_ref[...] = _dot_nt(wft_ref[...], xb)

    zt = _dot(xn_scr[...], w_ref[...])
    z_ref[...] = zt.astype(BF16)

    @pl.when(j == Z_FK // IN_BN)
    def _():
        fk_ref[...] = zt

    @pl.when(j == Z_FV // IN_BN)
    def _():
        fv_ref[...] = zt


def _in_proj(x, g, w, ws, wft, bm):
    T = x.shape[0]
    grid = (T // bm, Z_COLS // IN_BN)
    return pl.pallas_call(
        _in_proj_body,
        grid=grid,
        in_specs=[
            pl.BlockSpec((bm, D_MODEL), lambda i, j: (i, 0)),
            pl.BlockSpec((1, D_MODEL), lambda i, j: (0, 0)),
            pl.BlockSpec((D_MODEL, IN_BN), lambda i, j: (0, j)),
            pl.BlockSpec((D_MODEL, SMALL_COLS), lambda i, j: (0, 0)),
            pl.BlockSpec((FOX_HEADS, D_MODEL), lambda i, j: (0, 0)),
        ],
        out_specs=[
            pl.BlockSpec((bm, IN_BN), lambda i, j: (i, j)),
            pl.BlockSpec((bm, FOX_HEADS * FOX_HD), lambda i, j: (i, 0)),
            pl.BlockSpec((bm, FOX_HEADS * FOX_HD), lambda i, j: (i, 0)),
            pl.BlockSpec((bm, SMALL_COLS), lambda i, j: (i, 0)),
            pl.BlockSpec((FOX_HEADS, bm), lambda i, j: (0, i)),
        ],
        out_shape=[
            jax.ShapeDtypeStruct((T, Z_COLS), BF16),
            jax.ShapeDtypeStruct((T, FOX_HEADS * FOX_HD), F32),
            jax.ShapeDtypeStruct((T, FOX_HEADS * FOX_HD), F32),
            jax.ShapeDtypeStruct((T, SMALL_COLS), F32),
            jax.ShapeDtypeStruct((FOX_HEADS, T), F32),
        ],
        scratch_shapes=[pltpu.VMEM((bm, D_MODEL), BF16)],
        compiler_params=_cparams(("arbitrary", "arbitrary"), 40),
    )(x, g, w, ws, wft)


def _gla_consts(C):
    t = np.arange(C)
    r = t[None, :]
    mats = [r <= t[:, None], r > t[:, None]]
    masks = []
    n = C // 2
    while n >= 1:
        pair = t // (2 * n)
        second = (t // n) % 2 == 1
        mid = pair * 2 * n + n
        mats.append(np.where(second[:, None], (r >= mid[:, None]) & (r <= t[:, None]),
                             (r > t[:, None]) & (r < mid[:, None])))
        masks.append(second[:, None] & ~second[None, :] & (pair[:, None] == pair[None, :]))
        n //= 2
    masks.append(np.eye(C, dtype=bool))
    return (jnp.asarray(np.concatenate(mats, 0).astype(np.float32), BF16),
            jnp.asarray(np.stack(masks).astype(np.float32)))


def _gla_prompt_body(qk_ref, v_ref, gg_ref, small_ref, fft_ref, wa_ref, ba_ref, bf_ref, gn_ref, wall_ref,
                     masks_ref, y_ref, st_ref, logf_ref, c_ref, s_scr, c_scr, *, C, nlev):
    i = pl.program_id(1)

    @pl.when(i == 0)
    def _():
        s_scr[...] = jnp.zeros_like(s_scr)
        c_scr[...] = jnp.zeros_like(c_scr)

    lf = _log_sigmoid(fft_ref[...] + bf_ref[...])
    logf_ref[...] = lf
    lane = lax.broadcasted_iota(jnp.int32, lf.shape, 1)
    c = _lane_prefix_sum(lf, lane) + c_scr[...]
    c_ref[...] = c
    c_scr[...] = jnp.broadcast_to(c[:, C - 1:C], c_scr.shape)

    la = _log_sigmoid(_dot(small_ref[...].astype(BF16), wa_ref[...]) + ba_ref[...]) * (1.0 / GLA_TAU)
    hi = la.astype(BF16)
    lo = (la - hi.astype(F32)).astype(BF16)
    wall = wall_ref[...]
    P = jnp.exp(_dot(wall, hi) + _dot(wall, lo))

    W = GLA_HEADS * GLA_DK
    qk = qk_ref[...].astype(F32)
    q = qk[:, :W] * (GLA_DK ** -0.5)
    k = qk[:, W:]
    qd = (q * P[0:C]).astype(BF16)
    kd = (k * P[C:2 * C]).astype(BF16)
    a_last = P[C - 1:C, :]
    row = lax.broadcasted_iota(jnp.int32, (C, 1), 0)
    X = []
    n = C // 2
    for l in range(nlev):
        X.append((jnp.where((row & n) != 0, q, k) * P[(2 + l) * C:(3 + l) * C]).astype(BF16))
        n //= 2
    qb = q.astype(BF16)
    kb = k.astype(BF16)

    lane_p = lax.broadcasted_iota(jnp.int32, (1, LANES), 1)
    for p in range(GLA_HEADS // 2):
        sl = slice(LANES * p, LANES * (p + 1))
        ST = s_scr[p]
        STb = ST.astype(BF16)
        U = []
        for hh in range(2):
            h = 2 * p + hh
            hm = (lane_p >= GLA_DK) == bool(hh)

            def msk(a):
                return jnp.where(hm, a, jnp.zeros_like(a))

            sc = masks_ref[nlev] * _dot_nt(msk(qb[:, sl]), kb[:, sl])
            for l in range(nlev):
                xl = X[l][:, sl]
                sc = sc + masks_ref[l] * _dot_nt(msk(xl), xl)
            vh = v_ref[:, GLA_DV * h:GLA_DV * (h + 1)]
            o = _dot(sc.astype(BF16), vh) + _dot_nt(msk(qd[:, sl]), STb)
            U.append(_dot_tn(vh, kd[:, sl]))
            g = gg_ref[:, GLA_DV * h:GLA_DV * (h + 1)].astype(F32)
            y_ref[:, GLA_DV * h:GLA_DV * (h + 1)] = (_rms(o, gn_ref[...]) * (g * jax.nn.sigmoid(g))).astype(BF16)
        s_scr[p] = ST * a_last[:, sl] + jnp.where(lane_p < GLA_DK, U[0], U[1])

    @pl.when(i == pl.num_programs(1) - 1)
    def _():
        st_ref[0] = s_scr[...]


def _gla_prompt(z, small, fft, wa, ba, bf, gn, B, L):
    C = GLA_CHUNK
    nL = L // C
    wall, masks = _gla_consts(C)
    nlev = masks.shape[0] - 1
    T = B * L
    W = GLA_HEADS * GLA_DK
    row = lambda b, i: b * nL + i
    return pl.pallas_call(
        functools.partial(_gla_prompt_body, C=C, nlev=nlev),
        grid=(B, nL),
        in_specs=[
            pl.BlockSpec((C, 2 * W), lambda b, i: (row(b, i), Z_GQK // (2 * W))),
            pl.BlockSpec((C, GLA_HEADS * GLA_DV), lambda b, i: (row(b, i), Z_GV // (GLA_HEADS * GLA_DV))),
            pl.BlockSpec((C, GLA_HEADS * GLA_DV), lambda b, i: (row(b, i), Z_GG // (GLA_HEADS * GLA_DV))),
            pl.BlockSpec((C, SMALL_COLS), lambda b, i: (row(b, i), 0)),
            pl.BlockSpec((FOX_HEADS, C), lambda b, i: (0, row(b, i))),
            pl.BlockSpec((SMALL_COLS, W), lambda b, i: (0, 0)),
            pl.BlockSpec((1, W), lambda b, i: (0, 0)),
            pl.BlockSpec((FOX_HEADS, 1), lambda b, i: (0, 0)),
            pl.BlockSpec((1, GLA_DV), lambda b, i: (0, 0)),
            pl.BlockSpec(wall.shape, lambda b, i: (0, 0)),
            pl.BlockSpec(masks.shape, lambda b, i: (0, 0, 0)),
        ],
        out_specs=[
            pl.BlockSpec((C, GLA_HEADS * GLA_DV), lambda b, i: (row(b, i), 0)),
            pl.BlockSpec((1, GLA_HEADS // 2, GLA_DV, 2 * GLA_DK), lambda b, i: (b, 0, 0, 0)),
            pl.BlockSpec((FOX_HEADS, C), lambda b, i: (0, row(b, i))),
            pl.BlockSpec((FOX_HEADS, C), lambda b, i: (0, row(b, i))),
        ],
        out_shape=[
            jax.ShapeDtypeStruct((T, GLA_HEADS * GLA_DV), BF16),
            jax.ShapeDtypeStruct((B, GLA_HEADS // 2, GLA_DV, 2 * GLA_DK), F32),
            jax.ShapeDtypeStruct((FOX_HEADS, T), F32),
            jax.ShapeDtypeStruct((FOX_HEADS, T), F32),
        ],
        scratch_shapes=[pltpu.VMEM((GLA_HEADS // 2, GLA_DV, 2 * GLA_DK), F32),
                        pltpu.VMEM((FOX_HEADS, LANES), F32)],
        compiler_params=_cparams(("arbitrary", "arbitrary"), 32),
    )(z, z, z, small, fft, wa, ba, bf, gn, wall, masks)


def _gla_sample_body(q_ref, k_ref, v_ref, gg_ref, small_ref, wa_ref, ba_ref, gn_ref, s0_ref, y_ref, s1_ref,
                     *, nb, nt):
    la = _log_sigmoid(_dot(small_ref[...].astype(BF16), wa_ref[0]) + ba_ref[0]) * (1.0 / GLA_TAU)
    q = q_ref[0].astype(F32) * (GLA_DK ** -0.5)
    k = k_ref[0].astype(F32)
    rows = [slice(t * nb, (t + 1) * nb) for t in range(nt)]
    b = []
    for t in range(nt):
        b.append(la[rows[t]] if t == 0 else b[-1] + la[rows[t]])
    qt = [q[r] for r in rows]
    kt = [k[r] for r in rows]
    qd = [qt[t] * jnp.exp(b[t]) for t in range(nt)]
    kd = [kt[t] * jnp.exp(b[nt - 1] - b[t]) for t in range(nt)]
    a_last = jnp.exp(b[nt - 1])

    for d in range(GLA_DK):
        cs = slice(d * GLA_DV, (d + 1) * GLA_DV)
        acc = a_last[:, d:d + 1] * s0_ref[:, cs]
        for t in range(nt):
            acc = acc + kd[t][:, d:d + 1] * v_ref[rows[t], :].astype(F32)
        s1_ref[:, cs] = acc

    for t in range(nt):
        o = jnp.zeros((nb, GLA_DV), F32)
        for s in range(t + 1):
            w = jnp.sum(qt[t] * kt[s] * jnp.exp(b[t] - b[s]), axis=-1, keepdims=True)
            o = o + w * v_ref[rows[s], :].astype(F32)
        for d in range(GLA_DK):
            o = o + qd[t][:, d:d + 1] * s0_ref[:, d * GLA_DV:(d + 1) * GLA_DV]
        g = gg_ref[rows[t], :].astype(F32)
        y_ref[rows[t], :] = (_rms(o, gn_ref[...]) * (g * jax.nn.sigmoid(g))).astype(BF16)


def _gla_sample(qh, kh, z, small, wa_h, ba_h, gn, s0, nb, nt):
    T = nb * nt
    slab = GLA_DK * GLA_DV
    return pl.pallas_call(
        functools.partial(_gla_sample_body, nb=nb, nt=nt),
        grid=(GLA_HEADS,),
        in_specs=[
            pl.BlockSpec((1, T, GLA_DK), lambda h: (h, 0, 0)),
            pl.BlockSpec((1, T, GLA_DK), lambda h: (h, 0, 0)),
            pl.BlockSpec((T, GLA_DV), lambda h: (0, Z_GV // GLA_DV + h)),
            pl.BlockSpec((T, GLA_DV), lambda h: (0, Z_GG // GLA_DV + h)),
            pl.BlockSpec((T, SMALL_COLS), lambda h: (0, 0)),
            pl.BlockSpec((1, SMALL_COLS, GLA_DK), lambda h: (h, 0, 0)),
            pl.BlockSpec((1, 1, GLA_DK), lambda h: (h, 0, 0)),
            pl.BlockSpec((1, GLA_DV), lambda h: (0, 0)),
            pl.BlockSpec((nb, slab), lambda h: (0, h)),
        ],
        out_specs=[
            pl.BlockSpec((T, GLA_DV), lambda h: (0, h)),
            pl.BlockSpec((nb, slab), lambda h: (0, h)),
        ],
        out_shape=[
            jax.ShapeDtypeStruct((T, GLA_HEADS * GLA_DV), BF16),
            jax.ShapeDtypeStruct((nb, GLA_HEADS * slab), F32),
        ],
        compiler_params=_cparams(("arbitrary",), 40),
    )(qh, kh, z, z, small, wa_h, ba_h, gn, s0)


def _split3(c):
    def top(x):
        bits = lax.bitcast_convert_type(x, jnp.uint32) & jnp.uint32(0xFFFF0000)
        return lax.bitcast_convert_type(bits, F32)

    hi = top(c)
    mid = top(c - hi)
    return hi, mid, c - hi - mid


FOX_PREP_ROWS = 512


def _fox_prompt_body(q_ref, k_ref, v_ref, ck_ref, cq_ref, o_ref, k_scr, vt_scr, s_scr, *, bq, bk):
    i = pl.program_id(2)
    row0 = i * bq
    n_full = lax.div(row0, bk)
    lane = lax.broadcasted_iota(jnp.int32, (1, LANES), 1)
    NB = 3

    @pl.when(i == 0)
    def _():
        ch = min(FOX_PREP_ROWS, k_ref.shape[0])

        def prep(r, carry):
            start = pl.multiple_of(r * ch, ch)
            kc = k_ref[pl.ds(start, ch), :].astype(F32)
            cc = ck_ref[0, pl.ds(start, ch), :]
            vt = v_ref[pl.ds(start, ch), :].astype(F32).T
            ones = jnp.ones((LANES - FOX_HD, ch), F32)
            for hh in range(2):
                kh = kc if hh == 0 else pltpu.roll(kc, FOX_HD, axis=1)
                tile = jnp.where(lane < FOX_HD + NB, 1.0, 0.0)
                for n, term in enumerate(_split3(cc[:, hh:hh + 1])):
                    tile = jnp.where(lane == FOX_HD + NB + n, -term, tile)
                k_scr[hh, pl.ds(start, ch), :] = jnp.where(lane < FOX_HD, kh, tile).astype(BF16)
                vt_scr[hh, :, pl.ds(start, ch)] = jnp.concatenate(
                    [vt[hh * FOX_HD:(hh + 1) * FOX_HD], ones], axis=0).astype(BF16)
            return carry

        lax.fori_loop(0, k_ref.shape[0] // ch, prep, 0)

    qt = (q_ref[...].astype(F32) * (FOX_HD ** -0.5)).T
    row = lax.broadcasted_iota(jnp.int32, (LANES, 1), 0)
    qts = []
    for hh in range(2):
        tile = jnp.concatenate([qt[hh * FOX_HD:(hh + 1) * FOX_HD], jnp.zeros((LANES - FOX_HD, bq), F32)], axis=0)
        tile = jnp.where((row >= FOX_HD + NB) & (row < FOX_HD + 2 * NB), 1.0, tile)
        for n, term in enumerate(_split3(cq_ref[0, hh:hh + 1, :])):
            tile = jnp.where(row == FOX_HD + n, term, tile)
        qts.append(tile.astype(BF16))

    def scores(j):
        start = pl.multiple_of(j * bk, bk)
        return tuple(_dot(k_scr[hh, pl.ds(start, bk), :], qts[hh]) for hh in range(2))

    def softmax(j, s, m, masked):
        out = []
        for hh in range(2):
            sh = s[hh]
            if masked:
                key = j * bk + lax.broadcasted_iota(jnp.int32, (bk, bq), 0)
                qry = row0 + lax.broadcasted_iota(jnp.int32, (bk, bq), 1)
                sh = jnp.where(key <= qry, sh, NEG)
            m_new = jnp.maximum(m[hh], jnp.max(sh, axis=0, keepdims=True))
            out.append((m_new, jnp.exp(sh - m_new).astype(BF16), jnp.exp(m[hh] - m_new)))
        return tuple(zip(*out))

    def accumulate(j, p, alpha, acc):
        start = pl.multiple_of(j * bk, bk)
        return tuple(alpha[hh] * acc[hh] + _dot(vt_scr[hh, :, pl.ds(start, bk)], p[hh]) for hh in range(2))

    def put_scores(j):
        for hh, sh in enumerate(scores(j)):
            s_scr[hh] = sh

    def step(j, carry):
        m, acc = carry
        m, p, alpha = softmax(j, (s_scr[0], s_scr[1]), m, False)
        put_scores(j + 1)
        return m, accumulate(j, p, alpha, acc)

    two = lambda a: (a, a)
    put_scores(0)
    m, acc = lax.fori_loop(0, n_full, step, (two(jnp.full((1, bq), NEG, F32)), two(jnp.zeros((LANES, bq), F32))))
    m, p, alpha = softmax(n_full, (s_scr[0], s_scr[1]), m, True)
    acc = accumulate(n_full, p, alpha, acc)
    o = jnp.concatenate([a[:FOX_HD] / a[FOX_HD:FOX_HD + 1] for a in acc], axis=0)
    o_ref[...] = o.T.astype(BF16)


def _fox_prompt(z, c_cols, c_rows, B, L, bq, bk):
    nq = L // bq
    T = B * L
    npair = FOX_HEADS // 2
    return pl.pallas_call(
        functools.partial(_fox_prompt_body, bq=bq, bk=bk),
        grid=(B, npair, nq),
        in_specs=[
            pl.BlockSpec((bq, LANES), lambda b, p, i: (b * nq + i, Z_FQ // LANES + p)),
            pl.BlockSpec((L, LANES), lambda b, p, i: (b, Z_FK // LANES + p)),
            pl.BlockSpec((L, LANES), lambda b, p, i: (b, Z_FV // LANES + p)),
            pl.BlockSpec((1, L, 2), lambda b, p, i: (p, b, 0)),
            pl.BlockSpec((1, 2, bq), lambda b, p, i: (p, 0, b * nq + i)),
        ],
        out_specs=pl.BlockSpec((bq, LANES), lambda b, p, i: (b * nq + i, p)),
        out_shape=jax.ShapeDtypeStruct((T, FOX_HEADS * FOX_HD), BF16),
        scratch_shapes=[pltpu.VMEM((2, L, LANES), BF16), pltpu.VMEM((2, LANES, L), BF16),
                        pltpu.VMEM((2, bk, bq), F32)],
        compiler_params=_cparams(("arbitrary", "arbitrary", "arbitrary"), 56),
    )(z, z, z, c_cols, c_rows)


def _fox_sample_body(pt_ref, q_ref, kn_ref, vn_ref, fft_ref, bf_ref, *rest, n_pages, nt, page):
    del pt_ref
    k_refs = rest[:n_pages]
    v_refs = rest[n_pages:2 * n_pages]
    lf_refs = rest[2 * n_pages:3 * n_pages]
    o_ref, lfo_ref = rest[3 * n_pages:]
    W = FOX_HEADS * FOX_HD
    H = FOX_HEADS

    q = q_ref[0] * (FOX_HD ** -0.5)
    head_shift = FOX_HD.bit_length() - 1
    hmask = (lax.broadcasted_iota(jnp.int32, (H, W), 1) >> head_shift) == lax.broadcasted_iota(jnp.int32, (H, W), 0)
    qbd = jnp.concatenate([jnp.where(hmask, jnp.broadcast_to(q[t:t + 1], (H, W)), 0.0) for t in range(nt)], axis=0)
    qbd_b = qbd.astype(BF16)

    lane = lax.broadcasted_iota(jnp.int32, (H, LANES), 1)
    carry = jnp.zeros((H, 1), F32)
    ck = []
    for j in range(n_pages):
        c = _lane_prefix_sum(lf_refs[j][...], lane) + carry
        carry = c[:, LANES - 1:LANES]
        ck.append(c)
    lfn = _log_sigmoid(fft_ref[0] + bf_ref[...])
    lfo_ref[0] = lfn
    lane_t = lax.broadcasted_iota(jnp.int32, (H, nt), 1)
    cn = carry
    for j in range(nt):
        cn = cn + jnp.where(lane_t >= j, lfn[:, j:j + 1], 0.0)
    cq = jnp.concatenate([cn[:, t:t + 1] for t in range(nt)], axis=0)

    ck_all = jnp.concatenate(ck, axis=1)
    s_past = jnp.concatenate([_dot(qbd_b, k_refs[j][...].reshape(W, page).astype(BF16)) for j in range(n_pages)],
                             axis=1)
    s_past = s_past + (cq - jnp.concatenate([ck_all] * nt, axis=0))

    kn = kn_ref[0]
    vn = vn_ref[0]
    trow = lax.broadcasted_iota(jnp.int32, (nt * H, 1), 0) >> (H.bit_length() - 1)
    s_new = []
    for tp in range(nt):
        ckn = jnp.concatenate([cn[:, tp:tp + 1]] * nt, axis=0)
        sn = jnp.sum(qbd * kn[tp:tp + 1], axis=-1, keepdims=True) + (cq - ckn)
        s_new.append(jnp.where(trow >= tp, sn, NEG))
    m = jnp.max(s_past, axis=-1, keepdims=True)
    for sn in s_new:
        m = jnp.maximum(m, sn)
    p_past = jnp.exp(s_past - m)
    den = jnp.sum(p_past, axis=-1, keepdims=True)
    out = jnp.zeros((nt * H, W), F32)
    for j in range(n_pages):
        out = out + _dot_nt(p_past[:, j * page:(j + 1) * page].astype(BF16),
                            v_refs[j][...].reshape(W, page).astype(BF16))
    for tp in range(nt):
        pn = jnp.exp(s_new[tp] - m)
        den = den + pn
        out = out + pn * vn[tp:tp + 1]
    out = out / den
    for t in range(nt):
        o_ref[0, t:t + 1, :] = jnp.sum(jnp.where(hmask, out[t * H:(t + 1) * H], 0.0), axis=0, keepdims=True)


def _fox_sample(pt_flat, q_s, kn_s, vn_s, fft_s, bf, cache_kt, cache_vt, cache_lft, layer, nb, nt, n_pages, page):
    W = FOX_HEADS * FOX_HD
    assert page == LANES

    def page_spec(shape, j):
        zeros = (0,) * len(shape)
        return pl.BlockSpec((None, None) + shape, lambda s, pt: (layer, pt[s * n_pages + j]) + zeros)

    seq3 = lambda s, pt: (s, 0, 0)
    in_specs = [
        pl.BlockSpec((1, nt, W), seq3),
        pl.BlockSpec((1, nt, W), seq3),
        pl.BlockSpec((1, nt, W), seq3),
        pl.BlockSpec((1, FOX_HEADS, nt), seq3),
        pl.BlockSpec((FOX_HEADS, 1), lambda s, pt: (0, 0)),
    ]
    in_specs += [page_spec((FOX_HEADS, FOX_HD, page), j) for j in range(n_pages)]
    in_specs += [page_spec((FOX_HEADS, FOX_HD, page), j) for j in range(n_pages)]
    in_specs += [page_spec((FOX_HEADS, page), j) for j in range(n_pages)]
    grid_spec = pltpu.PrefetchScalarGridSpec(
        num_scalar_prefetch=1,
        grid=(nb,),
        in_specs=in_specs,
        out_specs=[pl.BlockSpec((1, nt, W), seq3), pl.BlockSpec((1, FOX_HEADS, nt), seq3)],
    )
    return pl.pallas_call(
        functools.partial(_fox_sample_body, n_pages=n_pages, nt=nt, page=page),
        grid_spec=grid_spec,
        out_shape=[jax.ShapeDtypeStruct((nb, nt, W), F32), jax.ShapeDtypeStruct((nb, FOX_HEADS, nt), F32)],
        compiler_params=_cparams(("arbitrary",), 48),
    )(pt_flat, q_s, kn_s, vn_s, fft_s, bf, *([cache_kt] * n_pages), *([cache_vt] * n_pages),
      *([cache_lft] * n_pages))


def _causal_conv(ext_scr, u, cw_ref, H, shift):
    bm = u.shape[0]
    ext_scr[H:H + bm, :] = u
    out = cw_ref[CONV_K - 1:CONV_K, :] * u
    for i in range(1, CONV_K):
        lo = H - i * shift
        out = out + cw_ref[CONV_K - 1 - i:CONV_K - i, :] * ext_scr[lo:lo + bm, :]
    return out


def _mix_body(*refs, H, shift, tiles_per_seq, has_state):
    if has_state:
        h_ref, gates_ref, sc_ref, yg_ref, yf_ref, cw_ref, wb_ref, wo_ref, halo_ref, hout_ref, tail_ref, ext_scr = refs
    else:
        h_ref, gates_ref, sc_ref, yg_ref, yf_ref, cw_ref, wb_ref, wo_ref, hout_ref, tail_ref, ext_scr = refs
    bm = h_ref.shape[0]
    if has_state:
        ext_scr[0:H, :] = halo_ref[...]
    else:
        seq_start = pl.program_id(0) % tiles_per_seq == 0

        @pl.when(seq_start)
        def _():
            ext_scr[0:H, :] = jnp.zeros((H, SC_WIDTH), F32)

        @pl.when(jnp.logical_not(seq_start))
        def _():
            ext_scr[0:H, :] = ext_scr[bm:bm + H, :]

    sc = sc_ref[...].astype(F32)
    u = sc[:, SC_WIDTH:2 * SC_WIDTH] * sc[:, 2 * SC_WIDTH:]
    y_sc = (sc[:, :SC_WIDTH] * _causal_conv(ext_scr, u, cw_ref, H, shift)).astype(BF16)
    tail_ref[0] = ext_scr[bm:bm + H, :]

    acc = jnp.zeros((bm, D_MODEL), F32)
    for b, y in enumerate((yg_ref[...], yf_ref[...], y_sc)):
        g = jax.nn.sigmoid(gates_ref[:, b * D_MODEL:(b + 1) * D_MODEL].astype(F32))
        acc = acc + g * _dot(y, wb_ref[b])
    hout_ref[...] = h_ref[...] + _dot(acc.astype(BF16), wo_ref[...])


def _mix(h, z, yg, yf, cw, wb, wo, halo, bm, H, shift, tiles_per_seq):
    T = h.shape[0]
    has_state = halo is not None
    in_specs = [
        pl.BlockSpec((bm, D_MODEL), lambda i: (i, 0)),
        pl.BlockSpec((bm, N_BRANCH * D_MODEL), lambda i: (i, 0)),
        pl.BlockSpec((bm, 3 * SC_WIDTH), lambda i: (i, Z_SC // (3 * SC_WIDTH))),
        pl.BlockSpec((bm, SC_WIDTH), lambda i: (i, 0)),
        pl.BlockSpec((bm, SC_WIDTH), lambda i: (i, 0)),
        pl.BlockSpec((CONV_K, SC_WIDTH), lambda i: (0, 0)),
        pl.BlockSpec((N_BRANCH, SC_WIDTH, D_MODEL), lambda i: (0, 0, 0)),
        pl.BlockSpec((D_MODEL, D_MODEL), lambda i: (0, 0)),
    ]
    args = [h, z, z, yg, yf, cw, wb, wo]
    if has_state:
        in_specs.append(pl.BlockSpec((H, SC_WIDTH), lambda i: (0, 0)))
        args.append(halo)
    return pl.pallas_call(
        functools.partial(_mix_body, H=H, shift=shift, tiles_per_seq=tiles_per_seq, has_state=has_state),
        grid=(T // bm,),
        in_specs=in_specs,
        out_specs=[pl.BlockSpec((bm, D_MODEL), lambda i: (i, 0)),
                   pl.BlockSpec((1, H, SC_WIDTH), lambda i: (i, 0, 0))],
        out_shape=[jax.ShapeDtypeStruct((T, D_MODEL), F32),
                   jax.ShapeDtypeStruct((T // bm, H, SC_WIDTH), F32)],
        scratch_shapes=[pltpu.VMEM((H + bm, SC_WIDTH), F32)],
        compiler_params=_cparams(("arbitrary",), 48),
    )(*args)


FFN_CK = 256


def _ffn_body(*refs, H, shift, tiles_per_seq, has_state, final_norm):
    refs = list(refs)
    h_ref, g_ref, wa_ref, wg_ref, cw_ref, cb_ref, wd_ref = refs[:7]
    refs = refs[7:]
    halo_ref = refs.pop(0) if has_state else None
    fg_ref = refs.pop(0) if final_norm else None
    out_ref, tail_ref, xn_scr, acc_scr, ext_scr, halo_scr = refs
    i = pl.program_id(0)
    c = pl.program_id(1)
    bm = h_ref.shape[0]

    @pl.when(c == 0)
    def _():
        xn_scr[...] = _rms(h_ref[...], g_ref[...]).astype(BF16)
        acc_scr[...] = jnp.zeros_like(acc_scr)

    xn = xn_scr[...]
    a = _dot(xn, wa_ref[...])
    gate = _dot(xn, wg_ref[...])
    if has_state:
        ext_scr[0:H, :] = halo_ref[...]
    else:
        seq_start = i % tiles_per_seq == 0

        @pl.when(seq_start)
        def _():
            ext_scr[0:H, :] = jnp.zeros((H, a.shape[1]), F32)

        @pl.when(jnp.logical_not(seq_start))
        def _():
            ext_scr[0:H, :] = halo_scr[c]

    ac =_causal_conv(ext_scr, a, cw_ref, H, shift) + cb_ref[...]
    tail = ext_scr[bm:bm + H, :]
    tail_ref[0] = tail
    if not has_state:
        halo_scr[c] = tail
    act = 0.5 * ac * (1.0 + lax.erf(ac * (2.0 ** -0.5))) * gate
    acc_scr[...] += _dot(act.astype(BF16), wd_ref[...])

    @pl.when(c == pl.num_programs(1) - 1)
    def _():
        hn = h_ref[...] + acc_scr[...]
        out_ref[...] = _rms(hn, fg_ref[...]) if final_norm else hn


def _ffn(h, g, wa, wg, cw, cb, wd, halo, fg, bm, H, shift, tiles_per_seq):
    T = h.shape[0]
    ck = FFN_CK
    nc = D_FF // ck
    has_state = halo is not None
    final_norm = fg is not None
    in_specs = [
        pl.BlockSpec((bm, D_MODEL), lambda i, c: (i, 0)),
        pl.BlockSpec((1, D_MODEL), lambda i, c: (0, 0)),
        pl.BlockSpec((D_MODEL, ck), lambda i, c: (0, c)),
        pl.BlockSpec((D_MODEL, ck), lambda i, c: (0, c)),
        pl.BlockSpec((CONV_K, ck), lambda i, c: (0, c)),
        pl.BlockSpec((1, ck), lambda i, c: (0, c)),
        pl.BlockSpec((ck, D_MODEL), lambda i, c: (c, 0)),
    ]
    args = [h, g, wa, wg, cw, cb, wd]
    if has_state:
        in_specs.append(pl.BlockSpec((H, ck), lambda i, c: (0, c)))
        args.append(halo)
    if final_norm:
        in_specs.append(pl.BlockSpec((1, D_MODEL), lambda i, c: (0, 0)))
        args.append(fg)
    return pl.pallas_call(
        functools.partial(_ffn_body, H=H, shift=shift, tiles_per_seq=tiles_per_seq, has_state=has_state,
                          final_norm=final_norm),
        grid=(T // bm, nc),
        in_specs=in_specs,
        out_specs=[pl.BlockSpec((bm, D_MODEL), lambda i, c: (i, 0)),
                   pl.BlockSpec((1, H, ck), lambda i, c: (i, 0, c))],
        out_shape=[jax.ShapeDtypeStruct((T, D_MODEL), F32),
                   jax.ShapeDtypeStruct((T // bm, H, D_FF), F32)],
        scratch_shapes=[pltpu.VMEM((bm, D_MODEL), BF16), pltpu.VMEM((bm, D_MODEL), F32),
                        pltpu.VMEM((H + bm, ck), F32), pltpu.VMEM((nc, H, ck), F32)],
        compiler_params=_cparams(("arbitrary", "arbitrary"), 48),
    )(*args)


def _pack_in_proj(w_in):
    sizes = (GLA_HEADS * GLA_DK, GLA_HEADS * GLA_DK, GLA_HEADS * GLA_DV, GLA_LR, GLA_HEADS * GLA_DV,
             FOX_HEADS * FOX_HD, FOX_HEADS * FOX_HD, FOX_HEADS * FOX_HD, FOX_HEADS,
             SC_WIDTH, SC_WIDTH, SC_WIDTH, N_BRANCH * D_MODEL)
    offs = np.concatenate([[0], np.cumsum(sizes)])
    part = lambda n: w_in[:, :, offs[n]:offs[n + 1]]
    gq, gk, gv, glr, gg, fq, fk, fv, ff, scb, scc, sch, gates = [part(n) for n in range(len(sizes))]
    w_main = jnp.concatenate([gates, gq, gk, gv, gg, fq, fk, fv, scb, scc, sch], axis=-1).astype(BF16)
    pad = jnp.zeros(w_in.shape[:2] + (SMALL_COLS - GLA_LR - FOX_HEADS,), w_in.dtype)
    w_small = jnp.concatenate([glr, ff, pad], axis=-1).astype(BF16)
    w_fft = jnp.swapaxes(ff, 1, 2).astype(BF16)
    return w_main, w_small, w_fft


def kernel(x_prompt, x_sample, cache_fox_k, cache_fox_v, cache_fox_logf, state_gla, state_sconv, state_ffn_conv,
           page_table, norm1_g, w_in, w_alpha2, b_alpha, b_forget, gla_norm_g, sconv_w, w_branch, w_out, norm2_g,
           w_up, ffn_conv_w, ffn_conv_b, w_down, final_norm_g):
    depth = w_in.shape[0]
    B, L, _ = x_prompt.shape
    nb, nt, _ = x_sample.shape
    n_pool, page = cache_fox_k.shape[1], cache_fox_k.shape[2]
    n_pages = page_table.shape[1]
    W = FOX_HEADS * FOX_HD
    Tp = B * L
    Ts = nb * nt
    bm_p = min(512, L)
    fox_bq, fox_bk = min(256, L), min(512, L)
    H_p, H_s = SUBLANES, (CONV_K - 1) * nb

    w_main, w_small, w_fft = _pack_in_proj(w_in)
    wa_pad = jnp.concatenate(
        [w_alpha2, jnp.zeros((depth, SMALL_COLS - GLA_LR, GLA_HEADS * GLA_DK), w_alpha2.dtype)], axis=1).astype(BF16)
    wa_heads = wa_pad.reshape(depth, SMALL_COLS, GLA_HEADS, GLA_DK).transpose(0, 2, 1, 3)
    ba_heads = b_alpha.reshape(depth, GLA_HEADS, 1, GLA_DK)
    wb = w_branch.astype(BF16)
    wo = w_out.astype(BF16)
    wu_a = w_up[:, :, :D_FF].astype(BF16)
    wu_g = w_up[:, :, D_FF:].astype(BF16)
    wd = w_down.astype(BF16)

    cache_kt = jnp.transpose(cache_fox_k, (0, 1, 3, 4, 2))
    cache_vt = jnp.transpose(cache_fox_v, (0, 1, 3, 4, 2))
    cache_lft = jnp.swapaxes(cache_fox_logf, 2, 3)
    pt_flat = page_table.reshape(-1).astype(jnp.int32)

    def to_seq(a):
        return jnp.swapaxes(a.reshape((a.shape[0] // nb, nb) + a.shape[1:]), 0, 1)

    def to_time(a):
        return jnp.swapaxes(a, 0, 1).reshape((a.shape[0] * a.shape[1],) + a.shape[2:])

    hp = x_prompt.reshape(Tp, D_MODEL)
    hs = to_time(x_sample)
    outs_p, outs_s = [], []
    for l in range(depth):
        g1 = norm1_g[l].reshape(1, D_MODEL)
        g2 = norm2_g[l].reshape(1, D_MODEL)
        bf = b_forget[l].reshape(FOX_HEADS, 1)
        ba = b_alpha[l].reshape(1, -1)
        gn = gla_norm_g[l].reshape(1, GLA_DV)
        cb = ffn_conv_b[l].reshape(1, D_FF)
        fg = final_norm_g.reshape(1, D_MODEL) if l == depth - 1 else None

        z, fk, fv, small, fft = _in_proj(hp, g1, w_main[l], w_small[l], w_fft[l], bm_p)
        yg, st, logf_t, c_t = _gla_prompt(z, small, fft, wa_pad[l], ba, bf, gn, B, L)
        c_rows = c_t.reshape(FOX_HEADS // 2, 2, Tp)
        yf = _fox_prompt(z, jnp.swapaxes(c_rows, 1, 2), c_rows, B, L, fox_bq, fox_bk)
        hp, sc_tail = _mix(hp, z, yg, yf, sconv_w[l], wb[l], wo[l], None, bm_p, H_p, 1, L // bm_p)
        hp, ffn_tail = _ffn(hp, g2, wu_a[l], wu_g[l], ffn_conv_w[l], cb, wd[l], None, fg, bm_p, H_p, 1, L // bm_p)
        last = lambda t: t.reshape(B, L // bm_p, H_p, -1)[:, -1, H_p - (CONV_K - 1):, :]
        p_gla = st.reshape(B, GLA_HEADS // 2, GLA_DV, 2, GLA_DK).transpose(0, 1, 3, 4, 2)
        outs_p.append((fk.reshape(B, L, FOX_HEADS, FOX_HD), fv.reshape(B, L, FOX_HEADS, FOX_HD),
                       logf_t.T.reshape(B, L, FOX_HEADS), p_gla.reshape(B, GLA_HEADS, GLA_DK, GLA_DV),
                       last(sc_tail), last(ffn_tail)))

        z, fk, fv, small, fft = _in_proj(hs, g1, w_main[l], w_small[l], w_fft[l], Ts)
        qk_h = z[:, Z_GQK:Z_GV].reshape(Ts, 2, GLA_HEADS, GLA_DK).transpose(1, 2, 0, 3)
        yg, s1 = _gla_sample(qk_h[0], qk_h[1], z, small, wa_heads[l], ba_heads[l], gn,
                             state_gla[l].reshape(nb, -1), nb, nt)
        yf_s, lfn = _fox_sample(pt_flat, to_seq(z[:, Z_FQ:Z_FK].astype(F32)), to_seq(fk), to_seq(fv),
                                jnp.swapaxes(fft.reshape(FOX_HEADS, nt, nb), 0, 2).swapaxes(1, 2), bf,
                                cache_kt, cache_vt, cache_lft, l, nb, nt, n_pages, page)
        yf = to_time(yf_s).astype(BF16)
        halo_sc = to_time(state_sconv[l])
        halo_ffn = to_time(state_ffn_conv[l])
        hs, sc_tail = _mix(hs, z, yg, yf, sconv_w[l], wb[l], wo[l], halo_sc, Ts, H_s, nb, 1)
        hs, ffn_tail = _ffn(hs, g2, wu_a[l], wu_g[l], ffn_conv_w[l], cb, wd[l], halo_ffn, fg, Ts, H_s, nb, 1)
        outs_s.append((to_seq(fk).reshape(nb, nt, FOX_HEADS, FOX_HD), to_seq(fv).reshape(nb, nt, FOX_HEADS, FOX_HD),
                       jnp.swapaxes(lfn, 1, 2), s1.reshape(nb, GLA_HEADS, GLA_DK, GLA_DV),
                       to_seq(sc_tail[0]), to_seq(ffn_tail[0])))

    p_states = [jnp.stack(t) for t in zip(*outs_p)]
    s_states = [jnp.stack(t) for t in zip(*outs_s)]
    y_prompt = hp.reshape(B, L, D_MODEL)
    y_sample = to_seq(hs)
    return (y_prompt, y_sample, *p_states, *s_states)
```

```python
import functools

import numpy as np
import jax
import jax.numpy as jnp
from jax import lax
from jax.experimental import pallas as pl
from jax.experimental.pallas import tpu as pltpu

F32 = jnp.float32
BF16 = jnp.bfloat16

D_MODEL = 1024
GLA_HEADS = 4
GLA_DK = 64
GLA_DV = 128
GLA_LR = 16
GLA_TAU = 16.0
FOX_HEADS = 8
FOX_HD = 64
SC_WIDTH = 512
CONV_K = 3
D_FF = 2816
N_BRANCH = 3
EPS = 1e-6

LANES = 128
SUBLANES = 8
VMEM_BYTES = 64 << 20

Z_GATES = 0
Z_GQK = 3072
Z_GV = 3584
Z_GG = 4096
Z_FQ = 4608
Z_FK = 5120
Z_FV = 5632
Z_SC = 6144
Z_COLS = 7680
IN_BN = 1536
SMALL_COLS = 128

GLA_CHUNK = 128
NEG = -1e30


def _cparams(sem, vmem_mb):
    return pltpu.CompilerParams(dimension_semantics=sem, vmem_limit_bytes=min(vmem_mb << 20, VMEM_BYTES - (4 << 20)))


def _log_sigmoid(x):
    return jnp.minimum(x, 0.0) - jnp.log1p(jnp.exp(-jnp.abs(x)))


def _dot(a, b):
    return jnp.dot(a, b, preferred_element_type=F32)


def _dot_nt(a, b):
    return lax.dot_general(a, b, (((1,), (1,)), ((), ())), preferred_element_type=F32)


def _dot_tn(a, b):
    return lax.dot_general(a, b, (((0,), (0,)), ((), ())), preferred_element_type=F32)


def _rms(x, g):
    return x * lax.rsqrt(jnp.mean(x * x, axis=-1, keepdims=True) + EPS) * g


def _lane_prefix_sum(c, lane):
    sh = 1
    while sh < LANES:
        c = c + jnp.where(lane >= sh, pltpu.roll(c, sh, axis=1), 0.0)
        sh *= 2
    return c


def _in_proj_body(x_ref, g_ref, w_ref, ws_ref, wft_ref, z_ref, fk_ref, fv_ref, small_ref, fft_ref, xn_scr):
    j = pl.program_id(1)

    @pl.when(j == 0)
    def _():
        xb = _rms(x_ref[...], g_ref[...]).astype(BF16)
        xn_scr[...] = xb
        small_ref[...] = _dot(xb, ws_ref[...])
        fft_ref[...] = _dot_nt(wft_ref[...], xb)

    zt = _dot(xn_scr[...], w_ref[...])
    z_ref[...] = zt.astype(BF16)

    W = FOX_HEADS * FOX_HD

    @pl.when(j == Z_FK // IN_BN)
    def _():
        fk_ref[...] = zt[:, Z_FK % IN_BN:Z_FK % IN_BN + W]

    @pl.when(j == Z_FV // IN_BN)
    def _():
        fv_ref[...] = zt[:, Z_FV % IN_BN:Z_FV % IN_BN + W]


def _in_proj(x, g, w, ws, wft, bm):
    T = x.shape[0]
    grid = (T // bm, Z_COLS // IN_BN)
    return pl.pallas_call(
        _in_proj_body,
        grid=grid,
        in_specs=[
            pl.BlockSpec((bm, D_MODEL), lambda i, j: (i, 0)),
            pl.BlockSpec((1, D_MODEL), lambda i, j: (0, 0)),
            pl.BlockSpec((D_MODEL, IN_BN), lambda i, j: (0, j)),
            pl.BlockSpec((D_MODEL, SMALL_COLS), lambda i, j: (0, 0)),
            pl.BlockSpec((FOX_HEADS, D_MODEL), lambda i, j: (0, 0)),
        ],
        out_specs=[
            pl.BlockSpec((bm, IN_BN), lambda i, j: (i, j)),
            pl.BlockSpec((bm, FOX_HEADS * FOX_HD), lambda i, j: (i, 0)),
            pl.BlockSpec((bm, FOX_HEADS * FOX_HD), lambda i, j: (i, 0)),
            pl.BlockSpec((bm, SMALL_COLS), lambda i, j: (i, 0)),
            pl.BlockSpec((FOX_HEADS, bm), lambda i, j: (0, i)),
        ],
        out_shape=[
            jax.ShapeDtypeStruct((T, Z_COLS), BF16),
            jax.ShapeDtypeStruct((T, FOX_HEADS * FOX_HD), F32),
            jax.ShapeDtypeStruct((T, FOX_HEADS * FOX_HD), F32),
            jax.ShapeDtypeStruct((T, SMALL_COLS), F32),
            jax.ShapeDtypeStruct((FOX_HEADS, T), F32),
        ],
        scratch_shapes=[pltpu.VMEM((bm, D_MODEL), BF16)],
        compiler_params=_cparams(("arbitrary", "arbitrary"), 56),
    )(x, g, w, ws, wft)


def _gla_consts(C):
    t = np.arange(C)
    r = t[None, :]
    mats = [r <= t[:, None], r > t[:, None]]
    masks = []
    n = C // 2
    while n >= 1:
        pair = t // (2 * n)
        second = (t // n) % 2 == 1
        mid = pair * 2 * n + n
        mats.append(np.where(second[:, None], (r >= mid[:, None]) & (r <= t[:, None]),
                             (r > t[:, None]) & (r < mid[:, None])))
        masks.append(second[:, None] & ~second[None, :] & (pair[:, None] == pair[None, :]))
        n //= 2
    masks.append(np.eye(C, dtype=bool))
    return (jnp.asarray(np.concatenate(mats, 0).astype(np.float32), BF16),
            jnp.asarray(np.stack(masks).astype(np.float32)))


def _gla_prompt_body(qk_ref, v_ref, gg_ref, small_ref, fft_ref, wa_ref, ba_ref, bf_ref, gn_ref, wall_ref,
                     masks_ref, y_ref, st_ref, logf_ref, c_ref, s_scr, c_scr, *, C, nlev):
    i = pl.program_id(1)

    @pl.when(i == 0)
    def _():
        s_scr[...] = jnp.zeros_like(s_scr)
        c_scr[...] = jnp.zeros_like(c_scr)

    lf = _log_sigmoid(fft_ref[...] + bf_ref[...])
    logf_ref[...] = lf
    lane = lax.broadcasted_iota(jnp.int32, lf.shape, 1)
    c = _lane_prefix_sum(lf, lane) + c_scr[...]
    c_ref[...] = c
    c_scr[...] = jnp.broadcast_to(c[:, C - 1:C], c_scr.shape)

    la = _log_sigmoid(_dot(small_ref[...].astype(BF16), wa_ref[...]) + ba_ref[...]) * (1.0 / GLA_TAU)
    hi = la.astype(BF16)
    lo = (la - hi.astype(F32)).astype(BF16)
    wall = wall_ref[...]
    P = jnp.exp(_dot(wall, hi) + _dot(wall, lo))

    W = GLA_HEADS * GLA_DK
    qk = qk_ref[...].astype(F32)
    q = qk[:, :W] * (GLA_DK ** -0.5)
    k = qk[:, W:]
    qd = (q * P[0:C]).astype(BF16)
    kd = (k * P[C:2 * C]).astype(BF16)
    a_last = P[C - 1:C, :]
    row = lax.broadcasted_iota(jnp.int32, (C, 1), 0)
    X = []
    n = C // 2
    for l in range(nlev):
        X.append((jnp.where((row & n) != 0, q, k) * P[(2 + l) * C:(3 + l) * C]).astype(BF16))
        n //= 2
    qb = q.astype(BF16)
    kb = k.astype(BF16)

    lane_p = lax.broadcasted_iota(jnp.int32, (1, LANES), 1)
    for p in range(GLA_HEADS // 2):
        sl = slice(LANES * p, LANES * (p + 1))
        ST = s_scr[p]
        STb = ST.astype(BF16)
        U = []
        for hh in range(2):
            h = 2 * p + hh
            hm = (lane_p >= GLA_DK) == bool(hh)

            def msk(a):
                return jnp.where(hm, a, jnp.zeros_like(a))

            sc = masks_ref[nlev] * _dot_nt(msk(qb[:, sl]), kb[:, sl])
            for l in range(nlev):
                xl = X[l][:, sl]
                sc = sc + masks_ref[l] * _dot_nt(msk(xl), xl)
            vh = v_ref[:, GLA_DV * h:GLA_DV * (h + 1)]
            o = _dot(sc.astype(BF16), vh) + _dot_nt(msk(qd[:, sl]), STb)
            U.append(_dot_tn(vh, kd[:, sl]))
            g = gg_ref[:, GLA_DV * h:GLA_DV * (h + 1)].astype(F32)
            y_ref[:, GLA_DV * h:GLA_DV * (h + 1)] = (_rms(o, gn_ref[...]) * (g * jax.nn.sigmoid(g))).astype(BF16)
        s_scr[p] = ST * a_last[:, sl] + jnp.where(lane_p < GLA_DK, U[0], U[1])

    @pl.when(i == pl.num_programs(1) - 1)
    def _():
        st_ref[0] = s_scr[...]


def _gla_prompt(z, small, fft, wa, ba, bf, gn, B, L):
    C = GLA_CHUNK
    nL = L // C
    wall, masks = _gla_consts(C)
    nlev = masks.shape[0] - 1
    T = B * L
    W = GLA_HEADS * GLA_DK
    row = lambda b, i: b * nL + i
    return pl.pallas_call(
        functools.partial(_gla_prompt_body, C=C, nlev=nlev),
        grid=(B, nL),
        in_specs=[
            pl.BlockSpec((C, 2 * W), lambda b, i: (row(b, i), Z_GQK // (2 * W))),
            pl.BlockSpec((C, GLA_HEADS * GLA_DV), lambda b, i: (row(b, i), Z_GV // (GLA_HEADS * GLA_DV))),
            pl.BlockSpec((C, GLA_HEADS * GLA_DV), lambda b, i: (row(b, i), Z_GG // (GLA_HEADS * GLA_DV))),
            pl.BlockSpec((C, SMALL_COLS), lambda b, i: (row(b, i), 0)),
            pl.BlockSpec((FOX_HEADS, C), lambda b, i: (0, row(b, i))),
            pl.BlockSpec((SMALL_COLS, W), lambda b, i: (0, 0)),
            pl.BlockSpec((1, W), lambda b, i: (0, 0)),
            pl.BlockSpec((FOX_HEADS, 1), lambda b, i: (0, 0)),
            pl.BlockSpec((1, GLA_DV), lambda b, i: (0, 0)),
            pl.BlockSpec(wall.shape, lambda b, i: (0, 0)),
            pl.BlockSpec(masks.shape, lambda b, i: (0, 0, 0)),
        ],
        out_specs=[
            pl.BlockSpec((C, GLA_HEADS * GLA_DV), lambda b, i: (row(b, i), 0)),
            pl.BlockSpec((1, GLA_HEADS // 2, GLA_DV, 2 * GLA_DK), lambda b, i: (b, 0, 0, 0)),
            pl.BlockSpec((FOX_HEADS, C), lambda b, i: (0, row(b, i))),
            pl.BlockSpec((FOX_HEADS, C), lambda b, i: (0, row(b, i))),
        ],
        out_shape=[
            jax.ShapeDtypeStruct((T, GLA_HEADS * GLA_DV), BF16),
            jax.ShapeDtypeStruct((B, GLA_HEADS // 2, GLA_DV, 2 * GLA_DK), F32),
            jax.ShapeDtypeStruct((FOX_HEADS, T), F32),
            jax.ShapeDtypeStruct((FOX_HEADS, T), F32),
        ],
        scratch_shapes=[pltpu.VMEM((GLA_HEADS // 2, GLA_DV, 2 * GLA_DK), F32),
                        pltpu.VMEM((FOX_HEADS, LANES), F32)],
        compiler_params=_cparams(("arbitrary", "arbitrary"), 32),
    )(z, z, z, small, fft, wa, ba, bf, gn, wall, masks)


def _gla_sample_body(q_ref, k_ref, v_ref, gg_ref, small_ref, wa_ref, ba_ref, gn_ref, s0_ref, y_ref, s1_ref,
                     *, nb, nt):
    la = _log_sigmoid(_dot(small_ref[...].astype(BF16), wa_ref[0]) + ba_ref[0]) * (1.0 / GLA_TAU)
    q = q_ref[0].astype(F32) * (GLA_DK ** -0.5)
    k = k_ref[0].astype(F32)
    rows = [slice(t * nb, (t + 1) * nb) for t in range(nt)]
    b = []
    for t in range(nt):
        b.append(la[rows[t]] if t == 0 else b[-1] + la[rows[t]])
    qt = [q[r] for r in rows]
    kt = [k[r] for r in rows]
    qd = [qt[t] * jnp.exp(b[t]) for t in range(nt)]
    kd = [kt[t] * jnp.exp(b[nt - 1] - b[t]) for t in range(nt)]
    a_last = jnp.exp(b[nt - 1])

    for d in range(GLA_DK):
        cs = slice(d * GLA_DV, (d + 1) * GLA_DV)
        acc = a_last[:, d:d + 1] * s0_ref[:, cs]
        for t in range(nt):
            acc = acc + kd[t][:, d:d + 1] * v_ref[rows[t], :].astype(F32)
        s1_ref[:, cs] = acc

    for t in range(nt):
        o = jnp.zeros((nb, GLA_DV), F32)
        for s in range(t + 1):
            w = jnp.sum(qt[t] * kt[s] * jnp.exp(b[t] - b[s]), axis=-1, keepdims=True)
            o = o + w * v_ref[rows[s], :].astype(F32)
        for d in range(GLA_DK):
            o = o + qd[t][:, d:d + 1] * s0_ref[:, d * GLA_DV:(d + 1) * GLA_DV]
        g = gg_ref[rows[t], :].astype(F32)
        y_ref[rows[t], :] = (_rms(o, gn_ref[...]) * (g * jax.nn.sigmoid(g))).astype(BF16)


def _gla_sample(qh, kh, z, small, wa_h, ba_h, gn, s0, nb, nt):
    T = nb * nt
    slab = GLA_DK * GLA_DV
    return pl.pallas_call(
        functools.partial(_gla_sample_body, nb=nb, nt=nt),
        grid=(GLA_HEADS,),
        in_specs=[
            pl.BlockSpec((1, T, GLA_DK), lambda h: (h, 0, 0)),
            pl.BlockSpec((1, T, GLA_DK), lambda h: (h, 0, 0)),
            pl.BlockSpec((T, GLA_DV), lambda h: (0, Z_GV // GLA_DV + h)),
            pl.BlockSpec((T, GLA_DV), lambda h: (0, Z_GG // GLA_DV + h)),
            pl.BlockSpec((T, SMALL_COLS), lambda h: (0, 0)),
            pl.BlockSpec((1, SMALL_COLS, GLA_DK), lambda h: (h, 0, 0)),
            pl.BlockSpec((1, 1, GLA_DK), lambda h: (h, 0, 0)),
            pl.BlockSpec((1, GLA_DV), lambda h: (0, 0)),
            pl.BlockSpec((nb, slab), lambda h: (0, h)),
        ],
        out_specs=[
            pl.BlockSpec((T, GLA_DV), lambda h: (0, h)),
            pl.BlockSpec((nb, slab), lambda h: (0, h)),
        ],
        out_shape=[
            jax.ShapeDtypeStruct((T, GLA_HEADS * GLA_DV), BF16),
            jax.ShapeDtypeStruct((nb, GLA_HEADS * slab), F32),
        ],
        compiler_params=_cparams(("arbitrary",), 40),
    )(qh, kh, z, z, small, wa_h, ba_h, gn, s0)


def _split3(c):
    def top(x):
        bits = lax.bitcast_convert_type(x, jnp.uint32) & jnp.uint32(0xFFFF0000)
        return lax.bitcast_convert_type(bits, F32)

    hi = top(c)
    mid = top(c - hi)
    return hi, mid, c - hi - mid


FOX_PREP_ROWS = 512


def _fox_prompt_body(q_ref, k_ref, v_ref, ck_ref, cq_ref, o_ref, k_scr, vt_scr, s_scr, *, bq, bk):
    i = pl.program_id(2)
    row0 = i * bq
    n_full = lax.div(row0, bk)
    lane = lax.broadcasted_iota(jnp.int32, (1, LANES), 1)
    NB = 3

    @pl.when(i == 0)
    def _():
        ch = min(FOX_PREP_ROWS, k_ref.shape[0])

        def prep(r, carry):
            start = pl.multiple_of(r * ch, ch)
            kc = k_ref[pl.ds(start, ch), :].astype(F32)
            cc = ck_ref[0, pl.ds(start, ch), :]
            vt = v_ref[pl.ds(start, ch), :].astype(F32).T
            ones = jnp.ones((LANES - FOX_HD, ch), F32)
            for hh in range(2):
                kh = kc if hh == 0 else pltpu.roll(kc, FOX_HD, axis=1)
                tile = jnp.where(lane < FOX_HD + NB, 1.0, 0.0)
                for n, term in enumerate(_split3(cc[:, hh:hh + 1])):
                    tile = jnp.where(lane == FOX_HD + NB + n, -term, tile)
                k_scr[hh, pl.ds(start, ch), :] = jnp.where(lane < FOX_HD, kh, tile).astype(BF16)
                vt_scr[hh, :, pl.ds(start, ch)] = jnp.concatenate(
                    [vt[hh * FOX_HD:(hh + 1) * FOX_HD], ones], axis=0).astype(BF16)
            return carry

        lax.fori_loop(0, k_ref.shape[0] // ch, prep, 0)

    qt = (q_ref[...].astype(F32) * (FOX_HD ** -0.5)).T
    row = lax.broadcasted_iota(jnp.int32, (LANES, 1), 0)
    qts = []
    for hh in range(2):
        tile = jnp.concatenate([qt[hh * FOX_HD:(hh + 1) * FOX_HD], jnp.zeros((LANES - FOX_HD, bq), F32)], axis=0)
        tile = jnp.where((row >= FOX_HD + NB) & (row < FOX_HD + 2 * NB), 1.0, tile)
        for n, term in enumerate(_split3(cq_ref[0, hh:hh + 1, :])):
            tile = jnp.where(row == FOX_HD + n, term, tile)
        qts.append(tile.astype(BF16))

    def scores(j):
        start = pl.multiple_of(j * bk, bk)
        return tuple(_dot(k_scr[hh, pl.ds(start, bk), :], qts[hh]) for hh in range(2))

    def softmax(j, s, m, masked):
        out = []
        for hh in range(2):
            sh = s[hh]
            if masked:
                key = j * bk + lax.broadcasted_iota(jnp.int32, (bk, bq), 0)
                qry = row0 + lax.broadcasted_iota(jnp.int32, (bk, bq), 1)
                sh = jnp.where(key <= qry, sh, NEG)
            m_new = jnp.maximum(m[hh], jnp.max(sh, axis=0, keepdims=True))
            out.append((m_new, jnp.exp(sh - m_new).astype(BF16), jnp.exp(m[hh] - m_new)))
        return tuple(zip(*out))

    def accumulate(j, p, alpha, acc):
        start = pl.multiple_of(j * bk, bk)
        return tuple(alpha[hh] * acc[hh] + _dot(vt_scr[hh, :, pl.ds(start, bk)], p[hh]) for hh in range(2))

    def put_scores(j):
        for hh, sh in enumerate(scores(j)):
            s_scr[hh] = sh

    def step(j, carry):
        m, acc = carry
        m, p, alpha = softmax(j, (s_scr[0], s_scr[1]), m, False)
        put_scores(j + 1)
        return m, accumulate(j, p, alpha, acc)

    two = lambda a: (a, a)
    put_scores(0)
    m, acc = lax.fori_loop(0, n_full, step, (two(jnp.full((1, bq), NEG, F32)), two(jnp.zeros((LANES, bq), F32))))
    m, p, alpha = softmax(n_full, (s_scr[0], s_scr[1]), m, True)
    acc = accumulate(n_full, p, alpha, acc)
    o = jnp.concatenate([a[:FOX_HD] / a[FOX_HD:FOX_HD + 1] for a in acc], axis=0)
    o_ref[...] = o.T.astype(BF16)


def _fox_prompt(z, c_cols, c_rows, B, L, bq, bk):
    nq = L // bq
    T = B * L
    npair = FOX_HEADS // 2
    return pl.pallas_call(
        functools.partial(_fox_prompt_body, bq=bq, bk=bk),
        grid=(B, npair, nq),
        in_specs=[
            pl.BlockSpec((bq, LANES), lambda b, p, i: (b * nq + i, Z_FQ // LANES + p)),
            pl.BlockSpec((L, LANES), lambda b, p, i: (b, Z_FK // LANES + p)),
            pl.BlockSpec((L, LANES), lambda b, p, i: (b, Z_FV // LANES + p)),
            pl.BlockSpec((1, L, 2), lambda b, p, i: (p, b, 0)),
            pl.BlockSpec((1, 2, bq), lambda b, p, i: (p, 0, b * nq + i)),
        ],
        out_specs=pl.BlockSpec((bq, LANES), lambda b, p, i: (b * nq + i, p)),
        out_shape=jax.ShapeDtypeStruct((T, FOX_HEADS * FOX_HD), BF16),
        scratch_shapes=[pltpu.VMEM((2, L, LANES), BF16), pltpu.VMEM((2, LANES, L), BF16),
                        pltpu.VMEM((2, bk, bq), F32)],
        compiler_params=_cparams(("arbitrary", "arbitrary", "arbitrary"), 56),
    )(z, z, z, c_cols, c_rows)


def _fox_sample_body(pt_ref, q_ref, kn_ref, vn_ref, fft_ref, bf_ref, *rest, n_pages, nt, page):
    del pt_ref
    k_refs = rest[:n_pages]
    v_refs = rest[n_pages:2 * n_pages]
    lf_refs = rest[2 * n_pages:3 * n_pages]
    o_ref, lfo_ref, k_scr, v_scr = rest[3 * n_pages:]
    W = FOX_HEADS * FOX_HD
    H = FOX_HEADS

    q = q_ref[0] * (FOX_HD ** -0.5)
    head_shift = FOX_HD.bit_length() - 1
    hmask = (lax.broadcasted_iota(jnp.int32, (H, W), 1) >> head_shift) == lax.broadcasted_iota(jnp.int32, (H, W), 0)
    qbd = jnp.concatenate([jnp.where(hmask, jnp.broadcast_to(q[t:t + 1], (H, W)), 0.0) for t in range(nt)], axis=0)
    qbd_b = qbd.astype(BF16)

    lane = lax.broadcasted_iota(jnp.int32, (H, LANES), 1)
    carry = jnp.zeros((H, 1), F32)
    ck = []
    for j in range(n_pages):
        c = _lane_prefix_sum(lf_refs[j][...], lane) + carry
        carry = c[:, LANES - 1:LANES]
        ck.append(c)
    lfn = _log_sigmoid(fft_ref[0] + bf_ref[...])
    lfo_ref[0] = lfn
    lane_t = lax.broadcasted_iota(jnp.int32, (H, nt), 1)
    cn = carry
    for j in range(nt):
        cn = cn + jnp.where(lane_t >= j, lfn[:, j:j + 1], 0.0)
    cq = jnp.concatenate([cn[:, t:t + 1] for t in range(nt)], axis=0)

    for j in range(n_pages):
        k_scr[:, j * page:(j + 1) * page] = k_refs[j][...].reshape(W, page).astype(BF16)
        v_scr[:, j * page:(j + 1) * page] = v_refs[j][...].reshape(W, page).astype(BF16)
    ck_all = jnp.concatenate(ck, axis=1)
    s_past = _dot(qbd_b, k_scr[...]) + (cq - jnp.concatenate([ck_all] * nt, axis=0))

    kn = kn_ref[0]
    vn = vn_ref[0]
    trow = lax.broadcasted_iota(jnp.int32, (nt * H, 1), 0) >> (H.bit_length() - 1)
    s_new = []
    for tp in range(nt):
        ckn = jnp.concatenate([cn[:, tp:tp + 1]] * nt, axis=0)
        sn = jnp.sum(qbd * kn[tp:tp + 1], axis=-1, keepdims=True) + (cq - ckn)
        s_new.append(jnp.where(trow >= tp, sn, NEG))
    m = jnp.max(s_past, axis=-1, keepdims=True)
    for sn in s_new:
        m = jnp.maximum(m, sn)
    p_past = jnp.exp(s_past - m)
    den = jnp.sum(p_past, axis=-1, keepdims=True)
    out = _dot_nt(p_past.astype(BF16), v_scr[...])
    for tp in range(nt):
        pn = jnp.exp(s_new[tp] - m)
        den = den + pn
        out = out + pn * vn[tp:tp + 1]
    out = out / den
    for t in range(nt):
        o_ref[0, t:t + 1, :] = jnp.sum(jnp.where(hmask, out[t * H:(t + 1) * H], 0.0), axis=0, keepdims=True)


def _fox_sample(pt_flat, q_s, kn_s, vn_s, fft_s, bf, cache_kt, cache_vt, cache_lft, layer, nb, nt, n_pages, page):
    W = FOX_HEADS * FOX_HD
    assert page == LANES

    def page_spec(shape, j):
        zeros = (0,) * len(shape)
        return pl.BlockSpec((None, None) + shape, lambda s, pt: (layer, pt[s * n_pages + j]) + zeros)

    seq3 = lambda s, pt: (s, 0, 0)
    in_specs = [
        pl.BlockSpec((1, nt, W), seq3),
        pl.BlockSpec((1, nt, W), seq3),
        pl.BlockSpec((1, nt, W), seq3),
        pl.BlockSpec((1, FOX_HEADS, nt), seq3),
        pl.BlockSpec((FOX_HEADS, 1), lambda s, pt: (0, 0)),
    ]
    in_specs += [page_spec((FOX_HEADS, FOX_HD, page), j) for j in range(n_pages)]
    in_specs += [page_spec((FOX_HEADS, FOX_HD, page), j) for j in range(n_pages)]
    in_specs += [page_spec((FOX_HEADS, page), j) for j in range(n_pages)]
    grid_spec = pltpu.PrefetchScalarGridSpec(
        num_scalar_prefetch=1,
        grid=(nb,),
        in_specs=in_specs,
        out_specs=[pl.BlockSpec((1, nt, W), seq3), pl.BlockSpec((1, FOX_HEADS, nt), seq3)],
        scratch_shapes=[pltpu.VMEM((W, n_pages * page), BF16), pltpu.VMEM((W, n_pages * page), BF16)],
    )
    return pl.pallas_call(
        functools.partial(_fox_sample_body, n_pages=n_pages, nt=nt, page=page),
        grid_spec=grid_spec,
        out_shape=[jax.ShapeDtypeStruct((nb, nt, W), F32), jax.ShapeDtypeStruct((nb, FOX_HEADS, nt), F32)],
        compiler_params=_cparams(("arbitrary",), 48),
    )(pt_flat, q_s, kn_s, vn_s, fft_s, bf, *([cache_kt] * n_pages), *([cache_vt] * n_pages),
      *([cache_lft] * n_pages))


def _causal_conv(ext_scr, u, cw_ref, H, shift):
    bm = u.shape[0]
    ext_scr[H:H + bm, :] = u
    out = cw_ref[CONV_K - 1:CONV_K, :] * u
    for i in range(1, CONV_K):
        lo = H - i * shift
        out = out + cw_ref[CONV_K - 1 - i:CONV_K - i, :] * ext_scr[lo:lo + bm, :]
    return out


def _mix_body(*refs, H, shift, tiles_per_seq, has_state):
    if has_state:
        h_ref, gates_ref, sc_ref, yg_ref, yf_ref, cw_ref, wb_ref, wo_ref, halo_ref, hout_ref, tail_ref, ext_scr = refs
    else:
        h_ref, gates_ref, sc_ref, yg_ref, yf_ref, cw_ref, wb_ref, wo_ref, hout_ref, tail_ref, ext_scr = refs
    bm = h_ref.shape[0]
    if has_state:
        ext_scr[0:H, :] = halo_ref[...]
    else:
        seq_start = pl.program_id(0) % tiles_per_seq == 0

        @pl.when(seq_start)
        def _():
            ext_scr[0:H, :] = jnp.zeros((H, SC_WIDTH), F32)

        @pl.when(jnp.logical_not(seq_start))
        def _():
            ext_scr[0:H, :] = ext_scr[bm:bm + H, :]

    sc = sc_ref[...].astype(F32)
    u = sc[:, SC_WIDTH:2 * SC_WIDTH] * sc[:, 2 * SC_WIDTH:]
    y_sc = (sc[:, :SC_WIDTH] * _causal_conv(ext_scr, u, cw_ref, H, shift)).astype(BF16)
    tail_ref[0] = ext_scr[bm:bm + H, :]

    acc = jnp.zeros((bm, D_MODEL), F32)
    for b, y in enumerate((yg_ref[...], yf_ref[...], y_sc)):
        g = jax.nn.sigmoid(gates_ref[:, b * D_MODEL:(b + 1) * D_MODEL].astype(F32))
        acc = acc + g * _dot(y, wb_ref[b])
    hout_ref[...] = h_ref[...] + _dot(acc.astype(BF16), wo_ref[...])


def _mix(h, z, yg, yf, cw, wb, wo, halo, bm, H, shift, tiles_per_seq):
    T = h.shape[0]
    has_state = halo is not None
    in_specs = [
        pl.BlockSpec((bm, D_MODEL), lambda i: (i, 0)),
        pl.BlockSpec((bm, N_BRANCH * D_MODEL), lambda i: (i, 0)),
        pl.BlockSpec((bm, 3 * SC_WIDTH), lambda i: (i, Z_SC // (3 * SC_WIDTH))),
        pl.BlockSpec((bm, SC_WIDTH), lambda i: (i, 0)),
        pl.BlockSpec((bm, SC_WIDTH), lambda i: (i, 0)),
        pl.BlockSpec((CONV_K, SC_WIDTH), lambda i: (0, 0)),
        pl.BlockSpec((N_BRANCH, SC_WIDTH, D_MODEL), lambda i: (0, 0, 0)),
        pl.BlockSpec((D_MODEL, D_MODEL), lambda i: (0, 0)),
    ]
    args = [h, z, z, yg, yf, cw, wb, wo]
    if has_state:
        in_specs.append(pl.BlockSpec((H, SC_WIDTH), lambda i: (0, 0)))
        args.append(halo)
    return pl.pallas_call(
        functools.partial(_mix_body, H=H, shift=shift, tiles_per_seq=tiles_per_seq, has_state=has_state),
        grid=(T // bm,),
        in_specs=in_specs,
        out_specs=[pl.BlockSpec((bm, D_MODEL), lambda i: (i, 0)),
                   pl.BlockSpec((1, H, SC_WIDTH), lambda i: (i, 0, 0))],
        out_shape=[jax.ShapeDtypeStruct((T, D_MODEL), F32),
                   jax.ShapeDtypeStruct((T // bm, H, SC_WIDTH), F32)],
        scratch_shapes=[pltpu.VMEM((H + bm, SC_WIDTH), F32)],
        compiler_params=_cparams(("arbitrary",), 48),
    )(*args)


FFN_CK = 1408


def _ffn_body(*refs, H, shift, tiles_per_seq, has_state, final_norm):
    refs = list(refs)
    h_ref, g_ref, wa_ref, wg_ref, cw_ref, cb_ref, wd_ref = refs[:7]
    refs = refs[7:]
    halo_ref = refs.pop(0) if has_state else None
    fg_ref = refs.pop(0) if final_norm else None
    out_ref, tail_ref, xn_scr, acc_scr, ext_scr, halo_scr = refs
    i = pl.program_id(0)
    c = pl.program_id(1)
    bm = h_ref.shape[0]

    @pl.when(c == 0)
    def _():
        xn_scr[...] = _rms(h_ref[...], g_ref[...]).astype(BF16)
        acc_scr[...] = jnp.zeros_like(acc_scr)

    xn = xn_scr[...]
    a = _dot(xn, wa_ref[...])
    gate = _dot(xn, wg_ref[...])
    if has_state:
        ext_scr[0:H, :] = halo_ref[...]
    else:
        seq_start = i % tiles_per_seq == 0

        @pl.when(seq_start)
        def _():
            ext_scr[0:H, :] = jnp.zeros((H, a.shape[1]), F32)

        @pl.when(jnp.logical_not(seq_start))
        def _():
            ext_scr[0:H, :] = halo_scr[c]

    ac =_causal_conv(ext_scr, a, cw_ref, H, shift) + cb_ref[...]
    tail = ext_scr[bm:bm + H, :]
    tail_ref[0] = tail
    if not has_state:
        halo_scr[c] = tail
    act = 0.5 * ac * (1.0 + lax.erf(ac * (2.0 ** -0.5))) * gate
    acc_scr[...] += _dot(act.astype(BF16), wd_ref[...])

    @pl.when(c == pl.num_programs(1) - 1)
    def _():
        hn = h_ref[...] + acc_scr[...]
        out_ref[...] = _rms(hn, fg_ref[...]) if final_norm else hn


def _ffn(h, g, wa, wg, cw, cb, wd, halo, fg, bm, H, shift, tiles_per_seq):
    T = h.shape[0]
    ck = FFN_CK
    nc = D_FF // ck
    has_state = halo is not None
    final_norm = fg is not None
    in_specs = [
        pl.BlockSpec((bm, D_MODEL), lambda i, c: (i, 0)),
        pl.BlockSpec((1, D_MODEL), lambda i, c: (0, 0)),
        pl.BlockSpec((D_MODEL, ck), lambda i, c: (0, c)),
        pl.BlockSpec((D_MODEL, ck), lambda i, c: (0, c)),
        pl.BlockSpec((CONV_K, ck), lambda i, c: (0, c)),
        pl.BlockSpec((1, ck), lambda i, c: (0, c)),
        pl.BlockSpec((ck, D_MODEL), lambda i, c: (c, 0)),
    ]
    args = [h, g, wa, wg, cw, cb, wd]
    if has_state:
        in_specs.append(pl.BlockSpec((H, ck), lambda i, c: (0, c)))
        args.append(halo)
    if final_norm:
        in_specs.append(pl.BlockSpec((1, D_MODEL), lambda i, c: (0, 0)))
        args.append(fg)
    return pl.pallas_call(
        functools.partial(_ffn_body, H=H, shift=shift, tiles_per_seq=tiles_per_seq, has_state=has_state,
                          final_norm=final_norm),
        grid=(T // bm, nc),
        in_specs=in_specs,
        out_specs=[pl.BlockSpec((bm, D_MODEL), lambda i, c: (i, 0)),
                   pl.BlockSpec((1, H, ck), lambda i, c: (i, 0, c))],
        out_shape=[jax.ShapeDtypeStruct((T, D_MODEL), F32),
                   jax.ShapeDtypeStruct((T // bm, H, D_FF), F32)],
        scratch_shapes=[pltpu.VMEM((bm, D_MODEL), BF16), pltpu.VMEM((bm, D_MODEL), F32),
                        pltpu.VMEM((H + bm, ck), F32), pltpu.VMEM((nc, H, ck), F32)],
        compiler_params=_cparams(("arbitrary", "arbitrary"), 56),
    )(*args)


def _pack_in_proj(w_in):
    sizes = (GLA_HEADS * GLA_DK, GLA_HEADS * GLA_DK, GLA_HEADS * GLA_DV, GLA_LR, GLA_HEADS * GLA_DV,
             FOX_HEADS * FOX_HD, FOX_HEADS * FOX_HD, FOX_HEADS * FOX_HD, FOX_HEADS,
             SC_WIDTH, SC_WIDTH, SC_WIDTH, N_BRANCH * D_MODEL)
    offs = np.concatenate([[0], np.cumsum(sizes)])
    part = lambda n: w_in[:, :, offs[n]:offs[n + 1]]
    gq, gk, gv, glr, gg, fq, fk, fv, ff, scb, scc, sch, gates = [part(n) for n in range(len(sizes))]
    w_main = jnp.concatenate([gates, gq, gk, gv, gg, fq, fk, fv, scb, scc, sch], axis=-1).astype(BF16)
    pad = jnp.zeros(w_in.shape[:2] + (SMALL_COLS - GLA_LR - FOX_HEADS,), w_in.dtype)
    w_small = jnp.concatenate([glr, ff, pad], axis=-1).astype(BF16)
    w_fft = jnp.swapaxes(ff, 1, 2).astype(BF16)
    return w_main, w_small, w_fft


def kernel(x_prompt, x_sample, cache_fox_k, cache_fox_v, cache_fox_logf, state_gla, state_sconv, state_ffn_conv,
           page_table, norm1_g, w_in, w_alpha2, b_alpha, b_forget, gla_norm_g, sconv_w, w_branch, w_out, norm2_g,
           w_up, ffn_conv_w, ffn_conv_b, w_down, final_norm_g):
    depth = w_in.shape[0]
    B, L, _ = x_prompt.shape
    nb, nt, _ = x_sample.shape
    n_pool, page = cache_fox_k.shape[1], cache_fox_k.shape[2]
    n_pages = page_table.shape[1]
    W = FOX_HEADS * FOX_HD
    Tp = B * L
    Ts = nb * nt
    bm_p = min(512, L)
    bm_in = min(1024, L)
    fox_bq, fox_bk = min(512, L), min(512, L)
    H_p, H_s = SUBLANES, (CONV_K - 1) * nb

    w_main, w_small, w_fft = _pack_in_proj(w_in)
    wa_pad = jnp.concatenate(
        [w_alpha2, jnp.zeros((depth, SMALL_COLS - GLA_LR, GLA_HEADS * GLA_DK), w_alpha2.dtype)], axis=1).astype(BF16)
    wa_heads = wa_pad.reshape(depth, SMALL_COLS, GLA_HEADS, GLA_DK).transpose(0, 2, 1, 3)
    ba_heads = b_alpha.reshape(depth, GLA_HEADS, 1, GLA_DK)
    wb = w_branch.astype(BF16)
    wo = w_out.astype(BF16)
    wu_a = w_up[:, :, :D_FF].astype(BF16)
    wu_g = w_up[:, :, D_FF:].astype(BF16)
    wd = w_down.astype(BF16)

    cache_kt = jnp.transpose(cache_fox_k, (0, 1, 3, 4, 2))
    cache_vt = jnp.transpose(cache_fox_v, (0, 1, 3, 4, 2))
    cache_lft = jnp.swapaxes(cache_fox_logf, 2, 3)
    pt_flat = page_table.reshape(-1).astype(jnp.int32)

    def to_seq(a):
        return jnp.swapaxes(a.reshape((a.shape[0] // nb, nb) + a.shape[1:]), 0, 1)

    def to_time(a):
        return jnp.swapaxes(a, 0, 1).reshape((a.shape[0] * a.shape[1],) + a.shape[2:])

    hp = x_prompt.reshape(Tp, D_MODEL)
    hs = to_time(x_sample)
    outs_p, outs_s = [], []
    for l in range(depth):
        g1 = norm1_g[l].reshape(1, D_MODEL)
        g2 = norm2_g[l].reshape(1, D_MODEL)
        bf = b_forget[l].reshape(FOX_HEADS, 1)
        ba = b_alpha[l].reshape(1, -1)
        gn = gla_norm_g[l].reshape(1, GLA_DV)
        cb = ffn_conv_b[l].reshape(1, D_FF)
        fg = final_norm_g.reshape(1, D_MODEL) if l == depth - 1 else None

        z, fk, fv, small, fft = _in_proj(hp, g1, w_main[l], w_small[l], w_fft[l], bm_in)
        yg, st, logf_t, c_t = _gla_prompt(z, small, fft, wa_pad[l], ba, bf, gn, B, L)
        c_rows = c_t.reshape(FOX_HEADS // 2, 2, Tp)
        yf = _fox_prompt(z, jnp.swapaxes(c_rows, 1, 2), c_rows, B, L, fox_bq, fox_bk)
        hp, sc_tail = _mix(hp, z, yg, yf, sconv_w[l], wb[l], wo[l], None, bm_p, H_p, 1, L // bm_p)
        hp, ffn_tail = _ffn(hp, g2, wu_a[l], wu_g[l], ffn_conv_w[l], cb, wd[l], None, fg, bm_p, H_p, 1, L // bm_p)
        last = lambda t: t.reshape(B, L // bm_p, H_p, -1)[:, -1, H_p - (CONV_K - 1):, :]
        p_gla = st.reshape(B, GLA_HEADS // 2, GLA_DV, 2, GLA_DK).transpose(0, 1, 3, 4, 2)
        outs_p.append((fk.reshape(B, L, FOX_HEADS, FOX_HD), fv.reshape(B, L, FOX_HEADS, FOX_HD),
                       logf_t.T.reshape(B, L, FOX_HEADS), p_gla.reshape(B, GLA_HEADS, GLA_DK, GLA_DV),
                       last(sc_tail), last(ffn_tail)))

        z, fk, fv, small, fft = _in_proj(hs, g1, w_main[l], w_small[l], w_fft[l], Ts)
        qk_h = z[:, Z_GQK:Z_GV].reshape(Ts, 2, GLA_HEADS, GLA_DK).transpose(1, 2, 0, 3)
        yg, s1 = _gla_sample(qk_h[0], qk_h[1], z, small, wa_heads[l], ba_heads[l], gn,
                             state_gla[l].reshape(nb, -1), nb, nt)
        yf_s, lfn = _fox_sample(pt_flat, to_seq(z[:, Z_FQ:Z_FK].astype(F32)), to_seq(fk), to_seq(fv),
                                jnp.swapaxes(fft.reshape(FOX_HEADS, nt, nb), 0, 2).swapaxes(1, 2), bf,
                                cache_kt, cache_vt, cache_lft, l, nb, nt, n_pages, page)
        yf = to_time(yf_s).astype(BF16)
        halo_sc = to_time(state_sconv[l])
        halo_ffn = to_time(state_ffn_conv[l])
        hs, sc_tail = _mix(hs, z, yg, yf, sconv_w[l], wb[l], wo[l], halo_sc, Ts, H_s, nb, 1)
        hs, ffn_tail = _ffn(hs, g2, wu_a[l], wu_g[l], ffn_conv_w[l], cb, wd[l], halo_ffn, fg, Ts, H_s, nb, 1)
        outs_s.append((to_seq(fk).reshape(nb, nt, FOX_HEADS, FOX_HD), to_seq(fv).reshape(nb, nt, FOX_HEADS, FOX_HD),
                       jnp.swapaxes(lfn, 1, 2), s1.reshape(nb, GLA_HEADS, GLA_DK, GLA_DV),
                       to_seq(sc_tail[0]), to_seq(ffn_tail[0])))

    p_states = [jnp.stack(t) for t in zip(*outs_p)]
    s_states = [jnp.stack(t) for t in zip(*outs_s)]
    y_prompt = hp.reshape(B, L, D_MODEL)
    y_sample = to_seq(hs)
    return (y_prompt, y_sample, *p_states, *s_states)
```

```python
import functools

import numpy as np
import jax
import jax.numpy as jnp
from jax import lax
from jax.experimental import pallas as pl
from jax.experimental.pallas import tpu as pltpu

F32 = jnp.float32
BF16 = jnp.bfloat16

D_MODEL = 1024
GLA_HEADS = 4
GLA_DK = 64
GLA_DV = 128
GLA_LR = 16
GLA_TAU = 16.0
FOX_HEADS = 8
FOX_HD = 64
SC_WIDTH = 512
CONV_K = 3
D_FF = 2816
N_BRANCH = 3
EPS = 1e-6

LANES = 128
SUBLANES = 8
VMEM_BYTES = 64 << 20

Z_GATES = 0
Z_GQK = 3072
Z_GV = 3584
Z_GG = 4096
Z_FQ = 4608
Z_FK = 5120
Z_FV = 5632
Z_SC = 6144
Z_COLS = 7680
IN_BN = 1536
SMALL_COLS = 128

GLA_CHUNK = 128
NEG = -1e30


def _cparams(sem, vmem_mb):
    return pltpu.CompilerParams(dimension_semantics=sem, vmem_limit_bytes=min(vmem_mb << 20, VMEM_BYTES - (4 << 20)))


def _log_sigmoid(x):
    return jnp.minimum(x, 0.0) - jnp.log1p(jnp.exp(-jnp.abs(x)))


def _dot(a, b):
    return jnp.dot(a, b, preferred_element_type=F32)


def _dot_nt(a, b):
    return lax.dot_general(a, b, (((1,), (1,)), ((), ())), preferred_element_type=F32)


def _dot_tn(a, b):
    return lax.dot_general(a, b, (((0,), (0,)), ((), ())), preferred_element_type=F32)


def _rms(x, g):
    return x * lax.rsqrt(jnp.mean(x * x, axis=-1, keepdims=True) + EPS) * g


def _lane_prefix_sum(c, lane):
    sh = 1
    while sh < LANES:
        c = c + jnp.where(lane >= sh, pltpu.roll(c, sh, axis=1), 0.0)
        sh *= 2
    return c


def _in_proj_body(x_ref, g_ref, w_ref, ws_ref, wft_ref, z_ref, fk_ref, fv_ref, small_ref, fft_ref, xn_scr):
    j = pl.program_id(1)

    @pl.when(j == 0)
    def _():
        xb = _rms(x_ref[...], g_ref[...]).astype(BF16)
        xn_scr[...] = xb
        small_ref[...] = _dot(xb, ws_ref[...])
        fft_ref[...] = _dot_nt(wft_ref[...], xb)

    zt = _dot(xn_scr[...], w_ref[...])
    z_ref[...] = zt.astype(BF16)

    W = FOX_HEADS * FOX_HD

    @pl.when(j == Z_FK // IN_BN)
    def _():
        fk_ref[...] = zt[:, Z_FK % IN_BN:Z_FK % IN_BN + W]

    @pl.when(j == Z_FV // IN_BN)
    def _():
        fv_ref[...] = zt[:, Z_FV % IN_BN:Z_FV % IN_BN + W]


def _in_proj(x, g, w, ws, wft, bm):
    T = x.shape[0]
    grid = (T // bm, Z_COLS // IN_BN)
    return pl.pallas_call(
        _in_proj_body,
        grid=grid,
        in_specs=[
            pl.BlockSpec((bm, D_MODEL), lambda i, j: (i, 0)),
            pl.BlockSpec((1, D_MODEL), lambda i, j: (0, 0)),
            pl.BlockSpec((D_MODEL, IN_BN), lambda i, j: (0, j)),
            pl.BlockSpec((D_MODEL, SMALL_COLS), lambda i, j: (0, 0)),
            pl.BlockSpec((FOX_HEADS, D_MODEL), lambda i, j: (0, 0)),
        ],
        out_specs=[
            pl.BlockSpec((bm, IN_BN), lambda i, j: (i, j)),
            pl.BlockSpec((bm, FOX_HEADS * FOX_HD), lambda i, j: (i, 0)),
            pl.BlockSpec((bm, FOX_HEADS * FOX_HD), lambda i, j: (i, 0)),
            pl.BlockSpec((bm, SMALL_COLS), lambda i, j: (i, 0)),
            pl.BlockSpec((FOX_HEADS, bm), lambda i, j: (0, i)),
        ],
        out_shape=[
            jax.ShapeDtypeStruct((T, Z_COLS), BF16),
            jax.ShapeDtypeStruct((T, FOX_HEADS * FOX_HD), F32),
            jax.ShapeDtypeStruct((T, FOX_HEADS * FOX_HD), F32),
            jax.ShapeDtypeStruct((T, SMALL_COLS), F32),
            jax.ShapeDtypeStruct((FOX_HEADS, T), F32),
        ],
        scratch_shapes=[pltpu.VMEM((bm, D_MODEL), BF16)],
        compiler_params=_cparams(("arbitrary", "arbitrary"), 56),
    )(x, g, w, ws, wft)


def _gla_consts(C):
    t = np.arange(C)
    r = t[None, :]
    mats = [r <= t[:, None], r > t[:, None]]
    masks = []
    n = C // 2
    while n >= 1:
        pair = t // (2 * n)
        second = (t // n) % 2 == 1
        mid = pair * 2 * n + n
        mats.append(np.where(second[:, None], (r >= mid[:, None]) & (r <= t[:, None]),
                             (r > t[:, None]) & (r < mid[:, None])))
        masks.append(second[:, None] & ~second[None, :] & (pair[:, None] == pair[None, :]))
        n //= 2
    masks.append(np.eye(C, dtype=bool))
    return (jnp.asarray(np.concatenate(mats, 0).astype(np.float32), BF16),
            jnp.asarray(np.stack(masks).astype(np.float32)))


def _gla_prompt_body(qk_ref, v_ref, gg_ref, small_ref, fft_ref, wa_ref, ba_ref, bf_ref, gn_ref, wall_ref,
                     masks_ref, y_ref, st_ref, logf_ref, c_ref, s_scr, c_scr, *, C, nlev):
    i = pl.program_id(1)

    @pl.when(i == 0)
    def _():
        s_scr[...] = jnp.zeros_like(s_scr)
        c_scr[...] = jnp.zeros_like(c_scr)

    lf = _log_sigmoid(fft_ref[...] + bf_ref[...])
    logf_ref[...] = lf
    lane = lax.broadcasted_iota(jnp.int32, lf.shape, 1)
    c = _lane_prefix_sum(lf, lane) + c_scr[...]
    c_ref[...] = c
    c_scr[...] = jnp.broadcast_to(c[:, C - 1:C], c_scr.shape)

    la = _log_sigmoid(_dot(small_ref[...].astype(BF16), wa_ref[...]) + ba_ref[...]) * (1.0 / GLA_TAU)
    hi = la.astype(BF16)
    lo = (la - hi.astype(F32)).astype(BF16)
    wall = wall_ref[...]
    P = jnp.exp(_dot(wall, hi) + _dot(wall, lo))

    W = GLA_HEADS * GLA_DK
    qk = qk_ref[...].astype(F32)
    q = qk[:, :W] * (GLA_DK ** -0.5)
    k = qk[:, W:]
    qd = (q * P[0:C]).astype(BF16)
    kd = (k * P[C:2 * C]).astype(BF16)
    a_last = P[C - 1:C, :]
    row = lax.broadcasted_iota(jnp.int32, (C, 1), 0)
    X = []
    n = C // 2
    for l in range(nlev):
        X.append((jnp.where((row & n) != 0, q, k) * P[(2 + l) * C:(3 + l) * C]).astype(BF16))
        n //= 2
    qb = q.astype(BF16)
    kb = k.astype(BF16)

    lane_p = lax.broadcasted_iota(jnp.int32, (1, LANES), 1)
    for p in range(GLA_HEADS // 2):
        sl = slice(LANES * p, LANES * (p + 1))
        ST = s_scr[p]
        STb = ST.astype(BF16)
        U = []
        for hh in range(2):
            h = 2 * p + hh
            hm = (lane_p >= GLA_DK) == bool(hh)

            def msk(a):
                return jnp.where(hm, a, jnp.zeros_like(a))

            sc = masks_ref[nlev] * _dot_nt(msk(qb[:, sl]), kb[:, sl])
            for l in range(nlev):
                xl = X[l][:, sl]
                sc = sc + masks_ref[l] * _dot_nt(msk(xl), xl)
            vh = v_ref[:, GLA_DV * h:GLA_DV * (h + 1)]
            o = _dot(sc.astype(BF16), vh) + _dot_nt(msk(qd[:, sl]), STb)
            U.append(_dot_tn(vh, kd[:, sl]))
            g = gg_ref[:, GLA_DV * h:GLA_DV * (h + 1)].astype(F32)
            y_ref[:, GLA_DV * h:GLA_DV * (h + 1)] = (_rms(o, gn_ref[...]) * (g * jax.nn.sigmoid(g))).astype(BF16)
        s_scr[p] = ST * a_last[:, sl] + jnp.where(lane_p < GLA_DK, U[0], U[1])

    @pl.when(i == pl.num_programs(1) - 1)
    def _():
        st_ref[0] = s_scr[...]


def _gla_prompt(z, small, fft, wa, ba, bf, gn, B, L):
    C = GLA_CHUNK
    nL = L // C
    wall, masks = _gla_consts(C)
    nlev = masks.shape[0] - 1
    T = B * L
    W = GLA_HEADS * GLA_DK
    row = lambda b, i: b * nL + i
    return pl.pallas_call(
        functools.partial(_gla_prompt_body, C=C, nlev=nlev),
        grid=(B, nL),
        in_specs=[
            pl.BlockSpec((C, 2 * W), lambda b, i: (row(b, i), Z_GQK // (2 * W))),
            pl.BlockSpec((C, GLA_HEADS * GLA_DV), lambda b, i: (row(b, i), Z_GV // (GLA_HEADS * GLA_DV))),
            pl.BlockSpec((C, GLA_HEADS * GLA_DV), lambda b, i: (row(b, i), Z_GG // (GLA_HEADS * GLA_DV))),
            pl.BlockSpec((C, SMALL_COLS), lambda b, i: (row(b, i), 0)),
            pl.BlockSpec((FOX_HEADS, C), lambda b, i: (0, row(b, i))),
            pl.BlockSpec((SMALL_COLS, W), lambda b, i: (0, 0)),
            pl.BlockSpec((1, W), lambda b, i: (0, 0)),
            pl.BlockSpec((FOX_HEADS, 1), lambda b, i: (0, 0)),
            pl.BlockSpec((1, GLA_DV), lambda b, i: (0, 0)),
            pl.BlockSpec(wall.shape, lambda b, i: (0, 0)),
            pl.BlockSpec(masks.shape, lambda b, i: (0, 0, 0)),
        ],
        out_specs=[
            pl.BlockSpec((C, GLA_HEADS * GLA_DV), lambda b, i: (row(b, i), 0)),
            pl.BlockSpec((1, GLA_HEADS // 2, GLA_DV, 2 * GLA_DK), lambda b, i: (b, 0, 0, 0)),
            pl.BlockSpec((FOX_HEADS, C), lambda b, i: (0, row(b, i))),
            pl.BlockSpec((FOX_HEADS, C), lambda b, i: (0, row(b, i))),
        ],
        out_shape=[
            jax.ShapeDtypeStruct((T, GLA_HEADS * GLA_DV), BF16),
            jax.ShapeDtypeStruct((B, GLA_HEADS // 2, GLA_DV, 2 * GLA_DK), F32),
            jax.ShapeDtypeStruct((FOX_HEADS, T), F32),
            jax.ShapeDtypeStruct((FOX_HEADS, T), F32),
        ],
        scratch_shapes=[pltpu.VMEM((GLA_HEADS // 2, GLA_DV, 2 * GLA_DK), F32),
                        pltpu.VMEM((FOX_HEADS, LANES), F32)],
        compiler_params=_cparams(("arbitrary", "arbitrary"), 32),
    )(z, z, z, small, fft, wa, ba, bf, gn, wall, masks)


def _gla_sample_body(q_ref, k_ref, v_ref, gg_ref, small_ref, wa_ref, ba_ref, gn_ref, s0_ref, y_ref, s1_ref,
                     *, nb, nt):
    la = _log_sigmoid(_dot(small_ref[...].astype(BF16), wa_ref[0]) + ba_ref[0]) * (1.0 / GLA_TAU)
    q = q_ref[0].astype(F32) * (GLA_DK ** -0.5)
    k = k_ref[0].astype(F32)
    rows = [slice(t * nb, (t + 1) * nb) for t in range(nt)]
    b = []
    for t in range(nt):
        b.append(la[rows[t]] if t == 0 else b[-1] + la[rows[t]])
    qt = [q[r] for r in rows]
    kt = [k[r] for r in rows]
    qd = [qt[t] * jnp.exp(b[t]) for t in range(nt)]
    kd = [kt[t] * jnp.exp(b[nt - 1] - b[t]) for t in range(nt)]
    a_last = jnp.exp(b[nt - 1])

    for d in range(GLA_DK):
        cs = slice(d * GLA_DV, (d + 1) * GLA_DV)
        acc = a_last[:, d:d + 1] * s0_ref[:, cs]
        for t in range(nt):
            acc = acc + kd[t][:, d:d + 1] * v_ref[rows[t], :].astype(F32)
        s1_ref[:, cs] = acc

    for t in range(nt):
        o = jnp.zeros((nb, GLA_DV), F32)
        for s in range(t + 1):
            w = jnp.sum(qt[t] * kt[s] * jnp.exp(b[t] - b[s]), axis=-1, keepdims=True)
            o = o + w * v_ref[rows[s], :].astype(F32)
        for d in range(GLA_DK):
            o = o + qd[t][:, d:d + 1] * s0_ref[:, d * GLA_DV:(d + 1) * GLA_DV]
        g = gg_ref[rows[t], :].astype(F32)
        y_ref[rows[t], :] = (_rms(o, gn_ref[...]) * (g * jax.nn.sigmoid(g))).astype(BF16)


def _gla_sample(qh, kh, z, small, wa_h, ba_h, gn, s0, nb, nt):
    T = nb * nt
    slab = GLA_DK * GLA_DV
    return pl.pallas_call(
        functools.partial(_gla_sample_body, nb=nb, nt=nt),
        grid=(GLA_HEADS,),
        in_specs=[
            pl.BlockSpec((1, T, GLA_DK), lambda h: (h, 0, 0)),
            pl.BlockSpec((1, T, GLA_DK), lambda h: (h, 0, 0)),
            pl.BlockSpec((T, GLA_DV), lambda h: (0, Z_GV // GLA_DV + h)),
            pl.BlockSpec((T, GLA_DV), lambda h: (0, Z_GG // GLA_DV + h)),
            pl.BlockSpec((T, SMALL_COLS), lambda h: (0, 0)),
            pl.BlockSpec((1, SMALL_COLS, GLA_DK), lambda h: (h, 0, 0)),
            pl.BlockSpec((1, 1, GLA_DK), lambda h: (h, 0, 0)),
            pl.BlockSpec((1, GLA_DV), lambda h: (0, 0)),
            pl.BlockSpec((nb, slab), lambda h: (0, h)),
        ],
        out_specs=[
            pl.BlockSpec((T, GLA_DV), lambda h: (0, h)),
            pl.BlockSpec((nb, slab), lambda h: (0, h)),
        ],
        out_shape=[
            jax.ShapeDtypeStruct((T, GLA_HEADS * GLA_DV), BF16),
            jax.ShapeDtypeStruct((nb, GLA_HEADS * slab), F32),
        ],
        compiler_params=_cparams(("arbitrary",), 40),
    )(qh, kh, z, z, small, wa_h, ba_h, gn, s0)


def _split3(c):
    def top(x):
        bits = lax.bitcast_convert_type(x, jnp.uint32) & jnp.uint32(0xFFFF0000)
        return lax.bitcast_convert_type(bits, F32)

    hi = top(c)
    mid = top(c - hi)
    return hi, mid, c - hi - mid


FOX_PREP_ROWS = 512
FOX_SUM_ROWS = 16
LOG2E = 1.4426950408889634


def _fox_prompt_body(q_ref, k_ref, v_ref, ck_ref, cq_ref, o_ref, k_scr, vt_scr, s_scr, *, bq, bk):
    i = pl.program_id(2)
    row0 = i * bq
    n_full = lax.div(row0, bk)
    lane = lax.broadcasted_iota(jnp.int32, (1, LANES), 1)
    NB = 3

    @pl.when(i == 0)
    def _():
        ch = min(FOX_PREP_ROWS, k_ref.shape[0])

        def prep(r, carry):
            start = pl.multiple_of(r * ch, ch)
            kc = k_ref[pl.ds(start, ch), :].astype(F32)
            cc = ck_ref[0, pl.ds(start, ch), :]
            vt = v_ref[pl.ds(start, ch), :].astype(F32).T
            ones = jnp.ones((FOX_SUM_ROWS, ch), F32)
            for hh in range(2):
                kh = kc if hh == 0 else pltpu.roll(kc, FOX_HD, axis=1)
                tile = jnp.where(lane < FOX_HD + NB, 1.0, 0.0)
                for n, term in enumerate(_split3(cc[:, hh:hh + 1] * LOG2E)):
                    tile = jnp.where(lane == FOX_HD + NB + n, -term, tile)
                k_scr[hh, pl.ds(start, ch), :] = jnp.where(lane < FOX_HD, kh, tile).astype(BF16)
                vt_scr[hh, :, pl.ds(start, ch)] = jnp.concatenate(
                    [vt[hh * FOX_HD:(hh + 1) * FOX_HD], ones], axis=0).astype(BF16)
            return carry

        lax.fori_loop(0, k_ref.shape[0] // ch, prep, 0)

    qt = (q_ref[...].astype(F32) * (FOX_HD ** -0.5 * LOG2E)).T
    row = lax.broadcasted_iota(jnp.int32, (LANES, 1), 0)
    qts = []
    for hh in range(2):
        tile = jnp.concatenate([qt[hh * FOX_HD:(hh + 1) * FOX_HD], jnp.zeros((LANES - FOX_HD, bq), F32)], axis=0)
        tile = jnp.where((row >= FOX_HD + NB) & (row < FOX_HD + 2 * NB), 1.0, tile)
        for n, term in enumerate(_split3(cq_ref[0, hh:hh + 1, :] * LOG2E)):
            tile = jnp.where(row == FOX_HD + n, term, tile)
        qts.append(tile.astype(BF16))

    def scores(j):
        start = pl.multiple_of(j * bk, bk)
        return tuple(_dot(k_scr[hh, pl.ds(start, bk), :], qts[hh]) for hh in range(2))

    def softmax(j, s, m, masked):
        out = []
        for hh in range(2):
            sh = s[hh]
            if masked:
                key = j * bk + lax.broadcasted_iota(jnp.int32, (bk, bq), 0)
                qry = row0 + lax.broadcasted_iota(jnp.int32, (bk, bq), 1)
                sh = jnp.where(key <= qry, sh, NEG)
            m_new = jnp.maximum(m[hh], jnp.max(sh, axis=0, keepdims=True))
            out.append((m_new, jnp.exp2(sh - m_new).astype(BF16), jnp.exp2(m[hh] - m_new)))
        return tuple(zip(*out))

    def accumulate(j, p, alpha, acc):
        start = pl.multiple_of(j * bk, bk)
        return tuple(alpha[hh] * acc[hh] + _dot(vt_scr[hh, :, pl.ds(start, bk)], p[hh]) for hh in range(2))

    def put_scores(j):
        for hh, sh in enumerate(scores(j)):
            s_scr[hh] = sh

    def step(j, carry):
        m, acc = carry
        m, p, alpha = softmax(j, (s_scr[0], s_scr[1]), m, False)
        put_scores(j + 1)
        return m, accumulate(j, p, alpha, acc)

    two = lambda a: (a, a)
    put_scores(0)
    acc0 = jnp.zeros((FOX_HD + FOX_SUM_ROWS, bq), F32)
    m, acc = lax.fori_loop(0, n_full, step, (two(jnp.full((1, bq), NEG, F32)), two(acc0)))
    m, p, alpha = softmax(n_full, (s_scr[0], s_scr[1]), m, True)
    acc = accumulate(n_full, p, alpha, acc)
    o = jnp.concatenate([a[:FOX_HD] / a[FOX_HD:FOX_HD + 1] for a in acc], axis=0)
    o_ref[...] = o.T.astype(BF16)


def _fox_prompt(z, c_cols, c_rows, B, L, bq, bk):
    nq = L // bq
    T = B * L
    npair = FOX_HEADS // 2
    return pl.pallas_call(
        functools.partial(_fox_prompt_body, bq=bq, bk=bk),
        grid=(B, npair, nq),
        in_specs=[
            pl.BlockSpec((bq, LANES), lambda b, p, i: (b * nq + i, Z_FQ // LANES + p)),
            pl.BlockSpec((L, LANES), lambda b, p, i: (b, Z_FK // LANES + p)),
            pl.BlockSpec((L, LANES), lambda b, p, i: (b, Z_FV // LANES + p)),
            pl.BlockSpec((1, L, 2), lambda b, p, i: (p, b, 0)),
            pl.BlockSpec((1, 2, bq), lambda b, p, i: (p, 0, b * nq + i)),
        ],
        out_specs=pl.BlockSpec((bq, LANES), lambda b, p, i: (b * nq + i, p)),
        out_shape=jax.ShapeDtypeStruct((T, FOX_HEADS * FOX_HD), BF16),
        scratch_shapes=[pltpu.VMEM((2, L, LANES), BF16), pltpu.VMEM((2, FOX_HD + FOX_SUM_ROWS, L), BF16),
                        pltpu.VMEM((2, bk, bq), F32)],
        compiler_params=_cparams(("arbitrary", "arbitrary", "arbitrary"), 56),
    )(z, z, z, c_cols, c_rows)


def _fox_sample_body(pt_ref, q_ref, kn_ref, vn_ref, fft_ref, bf_ref, kt_hbm, vt_hbm, lft_hbm, o_ref, lfo_ref,
                     kbuf, vbuf, lfbuf, k_scr, v_scr, sem, *, layer, n_pages, nt, page):
    W = FOX_HEADS * FOX_HD
    H = FOX_HEADS
    s = pl.program_id(0)
    slot = lax.rem(s, 2)

    def page_copies(seq, slot_):
        out = []
        for j in range(n_pages):
            pid = pt_ref[seq * n_pages + j]
            out.append(pltpu.make_async_copy(kt_hbm.at[layer, pid], kbuf.at[slot_, j], sem.at[slot_, 0]))
            out.append(pltpu.make_async_copy(vt_hbm.at[layer, pid], vbuf.at[slot_, j], sem.at[slot_, 1]))
            out.append(pltpu.make_async_copy(lft_hbm.at[layer, pid], lfbuf.at[slot_, j], sem.at[slot_, 2]))
        return out

    @pl.when(s == 0)
    def _():
        for c in page_copies(s, slot):
            c.start()

    @pl.when(s + 1 < pl.num_programs(0))
    def _():
        for c in page_copies(s + 1, 1 - slot):
            c.start()

    for c in page_copies(s, slot):
        c.wait()
    k_refs = [kbuf.at[slot, j] for j in range(n_pages)]
    v_refs = [vbuf.at[slot, j] for j in range(n_pages)]
    lf_refs = [lfbuf.at[slot, j] for j in range(n_pages)]

    q = q_ref[0] * (FOX_HD ** -0.5)
    head_shift = FOX_HD.bit_length() - 1
    hmask = (lax.broadcasted_iota(jnp.int32, (H, W), 1) >> head_shift) == lax.broadcasted_iota(jnp.int32, (H, W), 0)
    qbd = jnp.concatenate([jnp.where(hmask, jnp.broadcast_to(q[t:t + 1], (H, W)), 0.0) for t in range(nt)], axis=0)
    qbd_b = qbd.astype(BF16)

    lane = lax.broadcasted_iota(jnp.int32, (H, LANES), 1)
    carry = jnp.zeros((H, 1), F32)
    ck = []
    for j in range(n_pages):
        c = _lane_prefix_sum(lf_refs[j][...], lane) + carry
        carry = c[:, LANES - 1:LANES]
        ck.append(c)
    lfn = _log_sigmoid(fft_ref[0] + bf_ref[...])
    lfo_ref[0] = lfn
    lane_t = lax.broadcasted_iota(jnp.int32, (H, nt), 1)
    cn = carry
    for j in range(nt):
        cn = cn + jnp.where(lane_t >= j, lfn[:, j:j + 1], 0.0)
    cq = jnp.concatenate([cn[:, t:t + 1] for t in range(nt)], axis=0)

    for j in range(n_pages):
        k_scr[:, j * page:(j + 1) * page] = k_refs[j][...].reshape(W, page).astype(BF16)
        v_scr[:, j * page:(j + 1) * page] = v_refs[j][...].reshape(W, page).astype(BF16)
    ck_all = jnp.concatenate(ck, axis=1)
    s_past = _dot(qbd_b, k_scr[...]) + (cq - jnp.concatenate([ck_all] * nt, axis=0))

    kn = kn_ref[0]
    vn = vn_ref[0]
    trow = lax.broadcasted_iota(jnp.int32, (nt * H, 1), 0) >> (H.bit_length() - 1)
    s_new = []
    for tp in range(nt):
        ckn = jnp.concatenate([cn[:, tp:tp + 1]] * nt, axis=0)
        sn = jnp.sum(qbd * kn[tp:tp + 1], axis=-1, keepdims=True) + (cq - ckn)
        s_new.append(jnp.where(trow >= tp, sn, NEG))
    m = jnp.max(s_past, axis=-1, keepdims=True)
    for sn in s_new:
        m = jnp.maximum(m, sn)
    p_past = jnp.exp(s_past - m)
    den = jnp.sum(p_past, axis=-1, keepdims=True)
    out = _dot_nt(p_past.astype(BF16), v_scr[...])
    for tp in range(nt):
        pn = jnp.exp(s_new[tp] - m)
        den = den + pn
        out = out + pn * vn[tp:tp + 1]
    out = out / den
    for t in range(nt):
        o_ref[0, t:t + 1, :] = jnp.sum(jnp.where(hmask, out[t * H:(t + 1) * H], 0.0), axis=0, keepdims=True)


def _fox_sample(pt_flat, q_s, kn_s, vn_s, fft_s, bf, cache_kt, cache_vt, cache_lft, layer, nb, nt, n_pages, page):
    W = FOX_HEADS * FOX_HD
    assert page == LANES

    seq3 = lambda s, pt: (s, 0, 0)
    in_specs = [
        pl.BlockSpec((1, nt, W), seq3),
        pl.BlockSpec((1, nt, W), seq3),
        pl.BlockSpec((1, nt, W), seq3),
        pl.BlockSpec((1, FOX_HEADS, nt), seq3),
        pl.BlockSpec((FOX_HEADS, 1), lambda s, pt: (0, 0)),
        pl.BlockSpec(memory_space=pl.ANY),
        pl.BlockSpec(memory_space=pl.ANY),
        pl.BlockSpec(memory_space=pl.ANY),
    ]
    grid_spec = pltpu.PrefetchScalarGridSpec(
        num_scalar_prefetch=1,
        grid=(nb,),
        in_specs=in_specs,
        out_specs=[pl.BlockSpec((1, nt, W), seq3), pl.BlockSpec((1, FOX_HEADS, nt), seq3)],
        scratch_shapes=[pltpu.VMEM((2, n_pages, FOX_HEADS, FOX_HD, page), F32),
                        pltpu.VMEM((2, n_pages, FOX_HEADS, FOX_HD, page), F32),
                        pltpu.VMEM((2, n_pages, FOX_HEADS, page), F32),
                        pltpu.VMEM((W, n_pages * page), BF16), pltpu.VMEM((W, n_pages * page), BF16),
                        pltpu.SemaphoreType.DMA((2, 3))],
    )
    return pl.pallas_call(
        functools.partial(_fox_sample_body, layer=layer, n_pages=n_pages, nt=nt, page=page),
        grid_spec=grid_spec,
        out_shape=[jax.ShapeDtypeStruct((nb, nt, W), F32), jax.ShapeDtypeStruct((nb, FOX_HEADS, nt), F32)],
        compiler_params=_cparams(("arbitrary",), 48),
    )(pt_flat, q_s, kn_s, vn_s, fft_s, bf, cache_kt, cache_vt, cache_lft)


def _causal_conv(ext_scr, u, cw_ref, H, shift):
    bm = u.shape[0]
    ext_scr[H:H + bm, :] = u
    out = cw_ref[CONV_K - 1:CONV_K, :] * u
    for i in range(1, CONV_K):
        lo = H - i * shift
        out = out + cw_ref[CONV_K - 1 - i:CONV_K - i, :] * ext_scr[lo:lo + bm, :]
    return out


def _mix_body(*refs, H, shift, tiles_per_seq, has_state):
    if has_state:
        h_ref, gates_ref, sc_ref, yg_ref, yf_ref, cw_ref, wb_ref, wo_ref, halo_ref, hout_ref, tail_ref, ext_scr = refs
    else:
        h_ref, gates_ref, sc_ref, yg_ref, yf_ref, cw_ref, wb_ref, wo_ref, hout_ref, tail_ref, ext_scr = refs
    bm = h_ref.shape[0]
    if has_state:
        ext_scr[0:H, :] = halo_ref[...]
    else:
        seq_start = pl.program_id(0) % tiles_per_seq == 0

        @pl.when(seq_start)
        def _():
            ext_scr[0:H, :] = jnp.zeros((H, SC_WIDTH), F32)

        @pl.when(jnp.logical_not(seq_start))
        def _():
            ext_scr[0:H, :] = ext_scr[bm:bm + H, :]

    sc = sc_ref[...].astype(F32)
    u = sc[:, SC_WIDTH:2 * SC_WIDTH] * sc[:, 2 * SC_WIDTH:]
    y_sc = (sc[:, :SC_WIDTH] * _causal_conv(ext_scr, u, cw_ref, H, shift)).astype(BF16)
    tail_ref[0] = ext_scr[bm:bm + H, :]

    acc = jnp.zeros((bm, D_MODEL), F32)
    for b, y in enumerate((yg_ref[...], yf_ref[...], y_sc)):
        g = jax.nn.sigmoid(gates_ref[:, b * D_MODEL:(b + 1) * D_MODEL].astype(F32))
        acc = acc + g * _dot(y, wb_ref[b])
    hout_ref[...] = h_ref[...] + _dot(acc.astype(BF16), wo_ref[...])


def _mix(h, z, yg, yf, cw, wb, wo, halo, bm, H, shift, tiles_per_seq):
    T = h.shape[0]
    has_state = halo is not None
    in_specs = [
        pl.BlockSpec((bm, D_MODEL), lambda i: (i, 0)),
        pl.BlockSpec((bm, N_BRANCH * D_MODEL), lambda i: (i, 0)),
        pl.BlockSpec((bm, 3 * SC_WIDTH), lambda i: (i, Z_SC // (3 * SC_WIDTH))),
        pl.BlockSpec((bm, SC_WIDTH), lambda i: (i, 0)),
        pl.BlockSpec((bm, SC_WIDTH), lambda i: (i, 0)),
        pl.BlockSpec((CONV_K, SC_WIDTH), lambda i: (0, 0)),
        pl.BlockSpec((N_BRANCH, SC_WIDTH, D_MODEL), lambda i: (0, 0, 0)),
        pl.BlockSpec((D_MODEL, D_MODEL), lambda i: (0, 0)),
    ]
    args = [h, z, z, yg, yf, cw, wb, wo]
    if has_state:
        in_specs.append(pl.BlockSpec((H, SC_WIDTH), lambda i: (0, 0)))
        args.append(halo)
    return pl.pallas_call(
        functools.partial(_mix_body, H=H, shift=shift, tiles_per_seq=tiles_per_seq, has_state=has_state),
        grid=(T // bm,),
        in_specs=in_specs,
        out_specs=[pl.BlockSpec((bm, D_MODEL), lambda i: (i, 0)),
                   pl.BlockSpec((1, H, SC_WIDTH), lambda i: (i, 0, 0))],
        out_shape=[jax.ShapeDtypeStruct((T, D_MODEL), F32),
                   jax.ShapeDtypeStruct((T // bm, H, SC_WIDTH), F32)],
        scratch_shapes=[pltpu.VMEM((H + bm, SC_WIDTH), F32)],
        compiler_params=_cparams(("arbitrary",), 48),
    )(*args)


FFN_CK = 1408


def _ffn_body(*refs, H, shift, tiles_per_seq, has_state, final_norm):
    refs = list(refs)
    h_ref, g_ref, wa_ref, wg_ref, cw_ref, cb_ref, wd_ref = refs[:7]
    refs = refs[7:]
    halo_ref = refs.pop(0) if has_state else None
    fg_ref = refs.pop(0) if final_norm else None
    out_ref, tail_ref, xn_scr, acc_scr, ext_scr, halo_scr = refs
    i = pl.program_id(0)
    c = pl.program_id(1)
    bm = h_ref.shape[0]

    @pl.when(c == 0)
    def _():
        xn_scr[...] = _rms(h_ref[...], g_ref[...]).astype(BF16)
        acc_scr[...] = jnp.zeros_like(acc_scr)

    xn = xn_scr[...]
    a = _dot(xn, wa_ref[...])
    gate = _dot(xn, wg_ref[...])
    if has_state:
        ext_scr[0:H, :] = halo_ref[...]
    else:
        seq_start = i % tiles_per_seq == 0

        @pl.when(seq_start)
        def _():
            ext_scr[0:H, :] = jnp.zeros((H, a.shape[1]), F32)

        @pl.when(jnp.logical_not(seq_start))
        def _():
            ext_scr[0:H, :] = halo_scr[c]

    ac =_causal_conv(ext_scr, a, cw_ref, H, shift) + cb_ref[...]
    tail = ext_scr[bm:bm + H, :]
    tail_ref[0] = tail
    if not has_state:
        halo_scr[c] = tail
    act = 0.5 * ac * (1.0 + lax.erf(ac * (2.0 ** -0.5))) * gate
    acc_scr[...] += _dot(act.astype(BF16), wd_ref[...])

    @pl.when(c == pl.num_programs(1) - 1)
    def _():
        hn = h_ref[...] + acc_scr[...]
        out_ref[...] = _rms(hn, fg_ref[...]) if final_norm else hn


def _ffn(h, g, wa, wg, cw, cb, wd, halo, fg, bm, H, shift, tiles_per_seq):
    T = h.shape[0]
    ck = FFN_CK
    nc = D_FF // ck
    has_state = halo is not None
    final_norm = fg is not None
    in_specs = [
        pl.BlockSpec((bm, D_MODEL), lambda i, c: (i, 0)),
        pl.BlockSpec((1, D_MODEL), lambda i, c: (0, 0)),
        pl.BlockSpec((D_MODEL, ck), lambda i, c: (0, c)),
        pl.BlockSpec((D_MODEL, ck), lambda i, c: (0, c)),
        pl.BlockSpec((CONV_K, ck), lambda i, c: (0, c)),
        pl.BlockSpec((1, ck), lambda i, c: (0, c)),
        pl.BlockSpec((ck, D_MODEL), lambda i, c: (c, 0)),
    ]
    args = [h, g, wa, wg, cw, cb, wd]
    if has_state:
        in_specs.append(pl.BlockSpec((H, ck), lambda i, c: (0, c)))
        args.append(halo)
    if final_norm:
        in_specs.append(pl.BlockSpec((1, D_MODEL), lambda i, c: (0, 0)))
        args.append(fg)
    return pl.pallas_call(
        functools.partial(_ffn_body, H=H, shift=shift, tiles_per_seq=tiles_per_seq, has_state=has_state,
                          final_norm=final_norm),
        grid=(T // bm, nc),
        in_specs=in_specs,
        out_specs=[pl.BlockSpec((bm, D_MODEL), lambda i, c: (i, 0)),
                   pl.BlockSpec((1, H, ck), lambda i, c: (i, 0, c))],
        out_shape=[jax.ShapeDtypeStruct((T, D_MODEL), F32),
                   jax.ShapeDtypeStruct((T // bm, H, D_FF), F32)],
        scratch_shapes=[pltpu.VMEM((bm, D_MODEL), BF16), pltpu.VMEM((bm, D_MODEL), F32),
                        pltpu.VMEM((H + bm, ck), F32), pltpu.VMEM((nc, H, ck), F32)],
        compiler_params=_cparams(("arbitrary", "arbitrary"), 56),
    )(*args)


def _pack_in_proj(w_in):
    sizes = (GLA_HEADS * GLA_DK, GLA_HEADS * GLA_DK, GLA_HEADS * GLA_DV, GLA_LR, GLA_HEADS * GLA_DV,
             FOX_HEADS * FOX_HD, FOX_HEADS * FOX_HD, FOX_HEADS * FOX_HD, FOX_HEADS,
             SC_WIDTH, SC_WIDTH, SC_WIDTH, N_BRANCH * D_MODEL)
    offs = np.concatenate([[0], np.cumsum(sizes)])
    part = lambda n: w_in[:, :, offs[n]:offs[n + 1]]
    gq, gk, gv, glr, gg, fq, fk, fv, ff, scb, scc, sch, gates = [part(n) for n in range(len(sizes))]
    w_main = jnp.concatenate([gates, gq, gk, gv, gg, fq, fk, fv, scb, scc, sch], axis=-1).astype(BF16)
    pad = jnp.zeros(w_in.shape[:2] + (SMALL_COLS - GLA_LR - FOX_HEADS,), w_in.dtype)
    w_small = jnp.concatenate([glr, ff, pad], axis=-1).astype(BF16)
    w_fft = jnp.swapaxes(ff, 1, 2).astype(BF16)
    return w_main, w_small, w_fft


def kernel(x_prompt, x_sample, cache_fox_k, cache_fox_v, cache_fox_logf, state_gla, state_sconv, state_ffn_conv,
           page_table, norm1_g, w_in, w_alpha2, b_alpha, b_forget, gla_norm_g, sconv_w, w_branch, w_out, norm2_g,
           w_up, ffn_conv_w, ffn_conv_b, w_down, final_norm_g):
    depth = w_in.shape[0]
    B, L, _ = x_prompt.shape
    nb, nt, _ = x_sample.shape
    n_pool, page = cache_fox_k.shape[1], cache_fox_k.shape[2]
    n_pages = page_table.shape[1]
    W = FOX_HEADS * FOX_HD
    Tp = B * L
    Ts = nb * nt
    bm_p = min(512, L)
    bm_in = min(1024, L)
    fox_bq, fox_bk = min(512, L), min(512, L)
    H_p, H_s = SUBLANES, (CONV_K - 1) * nb

    w_main, w_small, w_fft = _pack_in_proj(w_in)
    wa_pad = jnp.concatenate(
        [w_alpha2, jnp.zeros((depth, SMALL_COLS - GLA_LR, GLA_HEADS * GLA_DK), w_alpha2.dtype)], axis=1).astype(BF16)
    wa_heads = wa_pad.reshape(depth, SMALL_COLS, GLA_HEADS, GLA_DK).transpose(0, 2, 1, 3)
    ba_heads = b_alpha.reshape(depth, GLA_HEADS, 1, GLA_DK)
    wb = w_branch.astype(BF16)
    wo = w_out.astype(BF16)
    wu_a = w_up[:, :, :D_FF].astype(BF16)
    wu_g = w_up[:, :, D_FF:].astype(BF16)
    wd = w_down.astype(BF16)

    cache_kt = jnp.transpose(cache_fox_k, (0, 1, 3, 4, 2))
    cache_vt = jnp.transpose(cache_fox_v, (0, 1, 3, 4, 2))
    cache_lft = jnp.swapaxes(cache_fox_logf, 2, 3)
    pt_flat = page_table.reshape(-1).astype(jnp.int32)

    def to_seq(a):
        return jnp.swapaxes(a.reshape((a.shape[0] // nb, nb) + a.shape[1:]), 0, 1)

    def to_time(a):
        return jnp.swapaxes(a, 0, 1).reshape((a.shape[0] * a.shape[1],) + a.shape[2:])

    hp = x_prompt.reshape(Tp, D_MODEL)
    hs = to_time(x_sample)
    outs_p, outs_s = [], []
    for l in range(depth):
        g1 = norm1_g[l].reshape(1, D_MODEL)
        g2 = norm2_g[l].reshape(1, D_MODEL)
        bf = b_forget[l].reshape(FOX_HEADS, 1)
        ba = b_alpha[l].reshape(1, -1)
        gn = gla_norm_g[l].reshape(1, GLA_DV)
        cb = ffn_conv_b[l].reshape(1, D_FF)
        fg = final_norm_g.reshape(1, D_MODEL) if l == depth - 1 else None

        z, fk, fv, small, fft = _in_proj(hp, g1, w_main[l], w_small[l], w_fft[l], bm_in)
        yg, st, logf_t, c_t = _gla_prompt(z, small, fft, wa_pad[l], ba, bf, gn, B, L)
        c_rows = c_t.reshape(FOX_HEADS // 2, 2, Tp)
        yf = _fox_prompt(z, jnp.swapaxes(c_rows, 1, 2), c_rows, B, L, fox_bq, fox_bk)
        hp, sc_tail = _mix(hp, z, yg, yf, sconv_w[l], wb[l], wo[l], None, bm_p, H_p, 1, L // bm_p)
        hp, ffn_tail = _ffn(hp, g2, wu_a[l], wu_g[l], ffn_conv_w[l], cb, wd[l], None, fg, bm_p, H_p, 1, L // bm_p)
        last = lambda t: t.reshape(B, L // bm_p, H_p, -1)[:, -1, H_p - (CONV_K - 1):, :]
        p_gla = st.reshape(B, GLA_HEADS // 2, GLA_DV, 2, GLA_DK).transpose(0, 1, 3, 4, 2)
        outs_p.append((fk.reshape(B, L, FOX_HEADS, FOX_HD), fv.reshape(B, L, FOX_HEADS, FOX_HD),
                       logf_t.T.reshape(B, L, FOX_HEADS), p_gla.reshape(B, GLA_HEADS, GLA_DK, GLA_DV),
                       last(sc_tail), last(ffn_tail)))

        z, fk, fv, small, fft = _in_proj(hs, g1, w_main[l], w_small[l], w_fft[l], Ts)
        qk_h = z[:, Z_GQK:Z_GV].reshape(Ts, 2, GLA_HEADS, GLA_DK).transpose(1, 2, 0, 3)
        yg, s1 = _gla_sample(qk_h[0], qk_h[1], z, small, wa_heads[l], ba_heads[l], gn,
                             state_gla[l].reshape(nb, -1), nb, nt)
        yf_s, lfn = _fox_sample(pt_flat, to_seq(z[:, Z_FQ:Z_FK].astype(F32)), to_seq(fk), to_seq(fv),
                                jnp.swapaxes(fft.reshape(FOX_HEADS, nt, nb), 0, 2).swapaxes(1, 2), bf,
                                cache_kt, cache_vt, cache_lft, l, nb, nt, n_pages, page)
        yf = to_time(yf_s).astype(BF16)
        halo_sc = to_time(state_sconv[l])
        halo_ffn = to_time(state_ffn_conv[l])
        hs, sc_tail = _mix(hs, z, yg, yf, sconv_w[l], wb[l], wo[l], halo_sc, Ts, H_s, nb, 1)
        hs, ffn_tail = _ffn(hs, g2, wu_a[l], wu_g[l], ffn_conv_w[l], cb, wd[l], halo_ffn, fg, Ts, H_s, nb, 1)
        outs_s.append((to_seq(fk).reshape(nb, nt, FOX_HEADS, FOX_HD), to_seq(fv).reshape(nb, nt, FOX_HEADS, FOX_HD),
                       jnp.swapaxes(lfn, 1, 2), s1.reshape(nb, GLA_HEADS, GLA_DK, GLA_DV),
                       to_seq(sc_tail[0]), to_seq(ffn_tail[0])))

    p_states = [jnp.stack(t) for t in zip(*outs_p)]
    s_states = [jnp.stack(t) for t in zip(*outs_s)]
    y_prompt = hp.reshape(B, L, D_MODEL)
    y_sample = to_seq(hs)
    return (y_prompt, y_sample, *p_states, *s_states)
```

```python
import functools

import numpy as np
import jax
import jax.numpy as jnp
from jax import lax
from jax.experimental import pallas as pl
from jax.experimental.pallas import tpu as pltpu

F32 = jnp.float32
BF16 = jnp.bfloat16

D_MODEL = 1024
GLA_HEADS = 4
GLA_DK = 64
GLA_DV = 128
GLA_LR = 16
GLA_TAU = 16.0
FOX_HEADS = 8
FOX_HD = 64
SC_WIDTH = 512
CONV_K = 3
D_FF = 2816
N_BRANCH = 3
EPS = 1e-6

LANES = 128
SUBLANES = 8
VMEM_BYTES = 64 << 20

Z_GATES = 0
Z_GQK = 3072
Z_GV = 3584
Z_GG = 4096
Z_FQ = 4608
Z_FK = 5120
Z_FV = 5632
Z_SC = 6144
Z_COLS = 7680
IN_BN = 1536
SMALL_COLS = 128

GLA_CHUNK = 128
NEG = -1e30


def _cparams(sem, vmem_mb):
    return pltpu.CompilerParams(dimension_semantics=sem, vmem_limit_bytes=min(vmem_mb << 20, VMEM_BYTES - (4 << 20)))


def _log_sigmoid(x):
    return jnp.minimum(x, 0.0) - jnp.log1p(jnp.exp(-jnp.abs(x)))


def _dot(a, b):
    return jnp.dot(a, b, preferred_element_type=F32)


def _dot_nt(a, b):
    return lax.dot_general(a, b, (((1,), (1,)), ((), ())), preferred_element_type=F32)


def _dot_tn(a, b):
    return lax.dot_general(a, b, (((0,), (0,)), ((), ())), preferred_element_type=F32)


def _rms(x, g):
    return x * lax.rsqrt(jnp.mean(x * x, axis=-1, keepdims=True) + EPS) * g


def _lane_prefix_sum(c, lane):
    sh = 1
    while sh < LANES:
        c = c + jnp.where(lane >= sh, pltpu.roll(c, sh, axis=1), 0.0)
        sh *= 2
    return c


def _in_proj_body(x_ref, g_ref, w_ref, ws_ref, wft_ref, z_ref, fk_ref, fv_ref, small_ref, fft_ref, xn_scr,
                  *, transposed):
    j = pl.program_id(1)

    @pl.when(j == 0)
    def _():
        xb = _rms(x_ref[...], g_ref[...]).astype(BF16)
        xn_scr[...] = xb
        small_ref[...] = _dot(xb, ws_ref[...])
        fft_ref[...] = _dot_nt(wft_ref[...], xb)

    zt = _dot(xn_scr[...], w_ref[...])
    z_ref[...] = zt.astype(BF16)

    W = FOX_HEADS * FOX_HD

    def part(off):
        blk = zt[:, off % IN_BN:off % IN_BN + W]
        return blk.T if transposed else blk

    @pl.when(j == Z_FK // IN_BN)
    def _():
        fk_ref[...] = part(Z_FK)

    @pl.when(j == Z_FV // IN_BN)
    def _():
        fv_ref[...] = part(Z_FV)


def _in_proj(x, g, w, ws, wft, bm, seqs=None):
    T = x.shape[0]
    W = FOX_HEADS * FOX_HD
    grid = (T // bm, Z_COLS // IN_BN)
    if seqs is None:
        kv_spec = pl.BlockSpec((bm, W), lambda i, j: (i, 0))
        kv_shape = jax.ShapeDtypeStruct((T, W), F32)
    else:
        B, L = seqs
        per_seq = L // bm
        kv_spec = pl.BlockSpec((None, W, bm), lambda i, j: (i // per_seq, 0, i % per_seq))
        kv_shape = jax.ShapeDtypeStruct((B, W, L), F32)
    return pl.pallas_call(
        functools.partial(_in_proj_body, transposed=seqs is not None),
        grid=grid,
        in_specs=[
            pl.BlockSpec((bm, D_MODEL), lambda i, j: (i, 0)),
            pl.BlockSpec((1, D_MODEL), lambda i, j: (0, 0)),
            pl.BlockSpec((D_MODEL, IN_BN), lambda i, j: (0, j)),
            pl.BlockSpec((D_MODEL, SMALL_COLS), lambda i, j: (0, 0)),
            pl.BlockSpec((FOX_HEADS, D_MODEL), lambda i, j: (0, 0)),
        ],
        out_specs=[
            pl.BlockSpec((bm, IN_BN), lambda i, j: (i, j)),
            kv_spec,
            kv_spec,
            pl.BlockSpec((bm, SMALL_COLS), lambda i, j: (i, 0)),
            pl.BlockSpec((FOX_HEADS, bm), lambda i, j: (0, i)),
        ],
        out_shape=[
            jax.ShapeDtypeStruct((T, Z_COLS), BF16),
            kv_shape,
            kv_shape,
            jax.ShapeDtypeStruct((T, SMALL_COLS), F32),
            jax.ShapeDtypeStruct((FOX_HEADS, T), F32),
        ],
        scratch_shapes=[pltpu.VMEM((bm, D_MODEL), BF16)],
        compiler_params=_cparams(("arbitrary", "arbitrary"), 56),
    )(x, g, w, ws, wft)


def _gla_consts(C):
    t = np.arange(C)
    r = t[None, :]
    mats = [r <= t[:, None], r > t[:, None]]
    masks = []
    n = C // 2
    while n >= 1:
        pair = t // (2 * n)
        second = (t // n) % 2 == 1
        mid = pair * 2 * n + n
        mats.append(np.where(second[:, None], (r >= mid[:, None]) & (r <= t[:, None]),
                             (r > t[:, None]) & (r < mid[:, None])))
        masks.append(second[:, None] & ~second[None, :] & (pair[:, None] == pair[None, :]))
        n //= 2
    masks.append(np.eye(C, dtype=bool))
    return (jnp.asarray(np.concatenate(mats, 0).astype(np.float32), BF16),
            jnp.asarray(np.stack(masks).astype(np.float32)))


def _gla_prompt_body(qk_ref, v_ref, gg_ref, small_ref, fft_ref, wa_ref, ba_ref, bf_ref, gn_ref, wall_ref,
                     masks_ref, y_ref, st_ref, logf_ref, c_ref, s_scr, c_scr, *, C, nlev):
    i = pl.program_id(1)

    @pl.when(i == 0)
    def _():
        s_scr[...] = jnp.zeros_like(s_scr)
        c_scr[...] = jnp.zeros_like(c_scr)

    lf = _log_sigmoid(fft_ref[...] + bf_ref[...])
    logf_ref[...] = lf
    lane = lax.broadcasted_iota(jnp.int32, lf.shape, 1)
    c = _lane_prefix_sum(lf, lane) + c_scr[...]
    c_ref[...] = c
    c_scr[...] = jnp.broadcast_to(c[:, C - 1:C], c_scr.shape)

    la = _log_sigmoid(_dot(small_ref[...].astype(BF16), wa_ref[...]) + ba_ref[...]) * (1.0 / GLA_TAU)
    hi = la.astype(BF16)
    lo = (la - hi.astype(F32)).astype(BF16)
    wall = wall_ref[...]
    P = jnp.exp(_dot(wall, hi) + _dot(wall, lo))

    W = GLA_HEADS * GLA_DK
    qk = qk_ref[...].astype(F32)
    q = qk[:, :W] * (GLA_DK ** -0.5)
    k = qk[:, W:]
    qd = (q * P[0:C]).astype(BF16)
    kd = (k * P[C:2 * C]).astype(BF16)
    a_last = P[C - 1:C, :]
    row = lax.broadcasted_iota(jnp.int32, (C, 1), 0)
    X = []
    n = C // 2
    for l in range(nlev):
        X.append((jnp.where((row & n) != 0, q, k) * P[(2 + l) * C:(3 + l) * C]).astype(BF16))
        n //= 2
    qb = q.astype(BF16)
    kb = k.astype(BF16)

    lane_p = lax.broadcasted_iota(jnp.int32, (1, LANES), 1)
    for p in range(GLA_HEADS // 2):
        sl = slice(LANES * p, LANES * (p + 1))
        ST = s_scr[p]
        STb = ST.astype(BF16)
        U = []
        for hh in range(2):
            h = 2 * p + hh
            hm = (lane_p >= GLA_DK) == bool(hh)

            def msk(a):
                return jnp.where(hm, a, jnp.zeros_like(a))

            sc = masks_ref[nlev] * _dot_nt(msk(qb[:, sl]), kb[:, sl])
            for l in range(nlev):
                xl = X[l][:, sl]
                sc = sc + masks_ref[l] * _dot_nt(msk(xl), xl)
            vh = v_ref[:, GLA_DV * h:GLA_DV * (h + 1)]
            o = _dot(sc.astype(BF16), vh) + _dot_nt(msk(qd[:, sl]), STb)
            U.append(_dot_tn(vh, kd[:, sl]))
            g = gg_ref[:, GLA_DV * h:GLA_DV * (h + 1)].astype(F32)
            y_ref[:, GLA_DV * h:GLA_DV * (h + 1)] = (_rms(o, gn_ref[...]) * (g * jax.nn.sigmoid(g))).astype(BF16)
        s_scr[p] = ST * a_last[:, sl] + jnp.where(lane_p < GLA_DK, U[0], U[1])

    @pl.when(i == pl.num_programs(1) - 1)
    def _():
        st_ref[0] = s_scr[...]


def _gla_prompt(z, small, fft, wa, ba, bf, gn, B, L):
    C = GLA_CHUNK
    nL = L // C
    wall, masks = _gla_consts(C)
    nlev = masks.shape[0] - 1
    T = B * L
    W = GLA_HEADS * GLA_DK
    row = lambda b, i: b * nL + i
    return pl.pallas_call(
        functools.partial(_gla_prompt_body, C=C, nlev=nlev),
        grid=(B, nL),
        in_specs=[
            pl.BlockSpec((C, 2 * W), lambda b, i: (row(b, i), Z_GQK // (2 * W))),
            pl.BlockSpec((C, GLA_HEADS * GLA_DV), lambda b, i: (row(b, i), Z_GV // (GLA_HEADS * GLA_DV))),
            pl.BlockSpec((C, GLA_HEADS * GLA_DV), lambda b, i: (row(b, i), Z_GG // (GLA_HEADS * GLA_DV))),
            pl.BlockSpec((C, SMALL_COLS), lambda b, i: (row(b, i), 0)),
            pl.BlockSpec((FOX_HEADS, C), lambda b, i: (0, row(b, i))),
            pl.BlockSpec((SMALL_COLS, W), lambda b, i: (0, 0)),
            pl.BlockSpec((1, W), lambda b, i: (0, 0)),
            pl.BlockSpec((FOX_HEADS, 1), lambda b, i: (0, 0)),
            pl.BlockSpec((1, GLA_DV), lambda b, i: (0, 0)),
            pl.BlockSpec(wall.shape, lambda b, i: (0, 0)),
            pl.BlockSpec(masks.shape, lambda b, i: (0, 0, 0)),
        ],
        out_specs=[
            pl.BlockSpec((C, GLA_HEADS * GLA_DV), lambda b, i: (row(b, i), 0)),
            pl.BlockSpec((1, GLA_HEADS // 2, GLA_DV, 2 * GLA_DK), lambda b, i: (b, 0, 0, 0)),
            pl.BlockSpec((FOX_HEADS, C), lambda b, i: (0, row(b, i))),
            pl.BlockSpec((FOX_HEADS, C), lambda b, i: (0, row(b, i))),
        ],
        out_shape=[
            jax.ShapeDtypeStruct((T, GLA_HEADS * GLA_DV), BF16),
            jax.ShapeDtypeStruct((B, GLA_HEADS // 2, GLA_DV, 2 * GLA_DK), F32),
            jax.ShapeDtypeStruct((FOX_HEADS, T), F32),
            jax.ShapeDtypeStruct((FOX_HEADS, T), F32),
        ],
        scratch_shapes=[pltpu.VMEM((GLA_HEADS // 2, GLA_DV, 2 * GLA_DK), F32),
                        pltpu.VMEM((FOX_HEADS, LANES), F32)],
        compiler_params=_cparams(("arbitrary", "arbitrary"), 32),
    )(z, z, z, small, fft, wa, ba, bf, gn, wall, masks)


def _gla_sample_body(q_ref, k_ref, v_ref, gg_ref, small_ref, wa_ref, ba_ref, gn_ref, s0_ref, y_ref, s1_ref,
                     *, nb, nt):
    la = _log_sigmoid(_dot(small_ref[...].astype(BF16), wa_ref[0]) + ba_ref[0]) * (1.0 / GLA_TAU)
    q = q_ref[0].astype(F32) * (GLA_DK ** -0.5)
    k = k_ref[0].astype(F32)
    rows = [slice(t * nb, (t + 1) * nb) for t in range(nt)]
    b = []
    for t in range(nt):
        b.append(la[rows[t]] if t == 0 else b[-1] + la[rows[t]])
    qt = [q[r] for r in rows]
    kt = [k[r] for r in rows]
    qd = [qt[t] * jnp.exp(b[t]) for t in range(nt)]
    kd = [kt[t] * jnp.exp(b[nt - 1] - b[t]) for t in range(nt)]
    a_last = jnp.exp(b[nt - 1])

    for d in range(GLA_DK):
        cs = slice(d * GLA_DV, (d + 1) * GLA_DV)
        acc = a_last[:, d:d + 1] * s0_ref[:, cs]
        for t in range(nt):
            acc = acc + kd[t][:, d:d + 1] * v_ref[rows[t], :].astype(F32)
        s1_ref[:, cs] = acc

    for t in range(nt):
        o = jnp.zeros((nb, GLA_DV), F32)
        for s in range(t + 1):
            w = jnp.sum(qt[t] * kt[s] * jnp.exp(b[t] - b[s]), axis=-1, keepdims=True)
            o = o + w * v_ref[rows[s], :].astype(F32)
        for d in range(GLA_DK):
            o = o + qd[t][:, d:d + 1] * s0_ref[:, d * GLA_DV:(d + 1) * GLA_DV]
        g = gg_ref[rows[t], :].astype(F32)
        y_ref[rows[t], :] = (_rms(o, gn_ref[...]) * (g * jax.nn.sigmoid(g))).astype(BF16)


def _gla_sample(qh, kh, z, small, wa_h, ba_h, gn, s0, nb, nt):
    T = nb * nt
    slab = GLA_DK * GLA_DV
    return pl.pallas_call(
        functools.partial(_gla_sample_body, nb=nb, nt=nt),
        grid=(GLA_HEADS,),
        in_specs=[
            pl.BlockSpec((1, T, GLA_DK), lambda h: (h, 0, 0)),
            pl.BlockSpec((1, T, GLA_DK), lambda h: (h, 0, 0)),
            pl.BlockSpec((T, GLA_DV), lambda h: (0, Z_GV // GLA_DV + h)),
            pl.BlockSpec((T, GLA_DV), lambda h: (0, Z_GG // GLA_DV + h)),
            pl.BlockSpec((T, SMALL_COLS), lambda h: (0, 0)),
            pl.BlockSpec((1, SMALL_COLS, GLA_DK), lambda h: (h, 0, 0)),
            pl.BlockSpec((1, 1, GLA_DK), lambda h: (h, 0, 0)),
            pl.BlockSpec((1, GLA_DV), lambda h: (0, 0)),
            pl.BlockSpec((nb, slab), lambda h: (0, h)),
        ],
        out_specs=[
            pl.BlockSpec((T, GLA_DV), lambda h: (0, h)),
            pl.BlockSpec((nb, slab), lambda h: (0, h)),
        ],
        out_shape=[
            jax.ShapeDtypeStruct((T, GLA_HEADS * GLA_DV), BF16),
            jax.ShapeDtypeStruct((nb, GLA_HEADS * slab), F32),
        ],
        compiler_params=_cparams(("arbitrary",), 40),
    )(qh, kh, z, z, small, wa_h, ba_h, gn, s0)


def _split3(c):
    def top(x):
        bits = lax.bitcast_convert_type(x, jnp.uint32) & jnp.uint32(0xFFFF0000)
        return lax.bitcast_convert_type(bits, F32)

    hi = top(c)
    mid = top(c - hi)
    return hi, mid, c - hi - mid


FOX_PREP_ROWS = 512
FOX_SUM_ROWS = 16
LOG2E = 1.4426950408889634


def _fox_prompt_body(q_ref, k_ref, v_ref, ck_ref, cq_ref, o_ref, k_scr, vt_scr, s_scr, *, bq, bk):
    i = pl.program_id(2)
    row0 = i * bq
    n_full = lax.div(row0, bk)
    lane = lax.broadcasted_iota(jnp.int32, (1, LANES), 1)
    NB = 3

    @pl.when(i == 0)
    def _():
        ch = min(FOX_PREP_ROWS, k_ref.shape[0])

        def prep(r, carry):
            start = pl.multiple_of(r * ch, ch)
            kc = k_ref[pl.ds(start, ch), :].astype(F32)
            cc = ck_ref[0, pl.ds(start, ch), :]
            vt = v_ref[pl.ds(start, ch), :].astype(F32).T
            ones = jnp.ones((FOX_SUM_ROWS, ch), F32)
            for hh in range(2):
                kh = kc if hh == 0 else pltpu.roll(kc, FOX_HD, axis=1)
                tile = jnp.where(lane < FOX_HD + NB, 1.0, 0.0)
                for n, term in enumerate(_split3(cc[:, hh:hh + 1] * LOG2E)):
                    tile = jnp.where(lane == FOX_HD + NB + n, -term, tile)
                k_scr[hh, pl.ds(start, ch), :] = jnp.where(lane < FOX_HD, kh, tile).astype(BF16)
                vt_scr[hh, :, pl.ds(start, ch)] = jnp.concatenate(
                    [vt[hh * FOX_HD:(hh + 1) * FOX_HD], ones], axis=0).astype(BF16)
            return carry

        lax.fori_loop(0, k_ref.shape[0] // ch, prep, 0)

    qt = (q_ref[...].astype(F32) * (FOX_HD ** -0.5 * LOG2E)).T
    row = lax.broadcasted_iota(jnp.int32, (LANES, 1), 0)
    qts = []
    for hh in range(2):
        tile = jnp.concatenate([qt[hh * FOX_HD:(hh + 1) * FOX_HD], jnp.zeros((LANES - FOX_HD, bq), F32)], axis=0)
        tile = jnp.where((row >= FOX_HD + NB) & (row < FOX_HD + 2 * NB), 1.0, tile)
        for n, term in enumerate(_split3(cq_ref[0, hh:hh + 1, :] * LOG2E)):
            tile = jnp.where(row == FOX_HD + n, term, tile)
        qts.append(tile.astype(BF16))

    def scores(j):
        start = pl.multiple_of(j * bk, bk)
        return tuple(_dot(k_scr[hh, pl.ds(start, bk), :], qts[hh]) for hh in range(2))

    def softmax(j, s, m, masked):
        out = []
        for hh in range(2):
            sh = s[hh]
            if masked:
                key = j * bk + lax.broadcasted_iota(jnp.int32, (bk, bq), 0)
                qry = row0 + lax.broadcasted_iota(jnp.int32, (bk, bq), 1)
                sh = jnp.where(key <= qry, sh, NEG)
            m_new = jnp.maximum(m[hh], jnp.max(sh, axis=0, keepdims=True))
            out.append((m_new, jnp.exp2(sh - m_new).astype(BF16), jnp.exp2(m[hh] - m_new)))
        return tuple(zip(*out))

    def accumulate(j, p, alpha, acc):
        start = pl.multiple_of(j * bk, bk)
        return tuple(alpha[hh] * acc[hh] + _dot(vt_scr[hh, :, pl.ds(start, bk)], p[hh]) for hh in range(2))

    def put_scores(j):
        for hh, sh in enumerate(scores(j)):
            s_scr[hh] = sh

    def step(j, carry):
        m, acc = carry
        m, p, alpha = softmax(j, (s_scr[0], s_scr[1]), m, False)
        put_scores(j + 1)
        return m, accumulate(j, p, alpha, acc)

    two = lambda a: (a, a)
    put_scores(0)
    acc0 = jnp.zeros((FOX_HD + FOX_SUM_ROWS, bq), F32)
    m, acc = lax.fori_loop(0, n_full, step, (two(jnp.full((1, bq), NEG, F32)), two(acc0)))
    m, p, alpha = softmax(n_full, (s_scr[0], s_scr[1]), m, True)
    acc = accumulate(n_full, p, alpha, acc)
    o = jnp.concatenate([a[:FOX_HD] / a[FOX_HD:FOX_HD + 1] for a in acc], axis=0)
    o_ref[...] = o.T.astype(BF16)


def _fox_prompt(z, c_cols, c_rows, B, L, bq, bk):
    nq = L // bq
    T = B * L
    npair = FOX_HEADS // 2
    return pl.pallas_call(
        functools.partial(_fox_prompt_body, bq=bq, bk=bk),
        grid=(B, npair, nq),
        in_specs=[
            pl.BlockSpec((bq, LANES), lambda b, p, i: (b * nq + i, Z_FQ // LANES + p)),
            pl.BlockSpec((L, LANES), lambda b, p, i: (b, Z_FK // LANES + p)),
            pl.BlockSpec((L, LANES), lambda b, p, i: (b, Z_FV // LANES + p)),
            pl.BlockSpec((1, L, 2), lambda b, p, i: (p, b, 0)),
            pl.BlockSpec((1, 2, bq), lambda b, p, i: (p, 0, b * nq + i)),
        ],
        out_specs=pl.BlockSpec((bq, LANES), lambda b, p, i: (b * nq + i, p)),
        out_shape=jax.ShapeDtypeStruct((T, FOX_HEADS * FOX_HD), BF16),
        scratch_shapes=[pltpu.VMEM((2, L, LANES), BF16), pltpu.VMEM((2, FOX_HD + FOX_SUM_ROWS, L), BF16),
                        pltpu.VMEM((2, bk, bq), F32)],
        compiler_params=_cparams(("arbitrary", "arbitrary", "arbitrary"), 56),
    )(z, z, z, c_cols, c_rows)


def _fox_sample_body(pt_ref, q_ref, kn_ref, vn_ref, fft_ref, bf_ref, kt_hbm, vt_hbm, lft_hbm, o_ref, lfo_ref,
                     kbuf, vbuf, lfbuf, k_scr, v_scr, sem, *, layer, n_pages, nt, page):
    W = FOX_HEADS * FOX_HD
    H = FOX_HEADS
    s = pl.program_id(0)
    slot = lax.rem(s, 2)

    def page_copies(seq, slot_):
        out = []
        for j in range(n_pages):
            pid = pt_ref[seq * n_pages + j]
            out.append(pltpu.make_async_copy(kt_hbm.at[layer, pid], kbuf.at[slot_, j], sem.at[slot_, 0]))
            out.append(pltpu.make_async_copy(vt_hbm.at[layer, pid], vbuf.at[slot_, j], sem.at[slot_, 1]))
            out.append(pltpu.make_async_copy(lft_hbm.at[layer, pid], lfbuf.at[slot_, j], sem.at[slot_, 2]))
        return out

    @pl.when(s == 0)
    def _():
        for c in page_copies(s, slot):
            c.start()

    @pl.when(s + 1 < pl.num_programs(0))
    def _():
        for c in page_copies(s + 1, 1 - slot):
            c.start()

    for n, (hbm, buf) in enumerate(((kt_hbm, kbuf), (vt_hbm, vbuf), (lft_hbm, lfbuf))):
        pltpu.make_async_copy(hbm.at[layer, pl.ds(0, n_pages)], buf.at[slot], sem.at[slot, n]).wait()
    k_refs = [kbuf.at[slot, j] for j in range(n_pages)]
    v_refs = [vbuf.at[slot, j] for j in range(n_pages)]
    lf_refs = [lfbuf.at[slot, j] for j in range(n_pages)]

    q = q_ref[0] * (FOX_HD ** -0.5)
    head_shift = FOX_HD.bit_length() - 1
    hmask = (lax.broadcasted_iota(jnp.int32, (H, W), 1) >> head_shift) == lax.broadcasted_iota(jnp.int32, (H, W), 0)
    qbd = jnp.concatenate([jnp.where(hmask, jnp.broadcast_to(q[t:t + 1], (H, W)), 0.0) for t in range(nt)], axis=0)
    qbd_b = qbd.astype(BF16)

    lane = lax.broadcasted_iota(jnp.int32, (H, LANES), 1)
    carry = jnp.zeros((H, 1), F32)
    ck = []
    for j in range(n_pages):
        c = _lane_prefix_sum(lf_refs[j][...], lane) + carry
        carry = c[:, LANES - 1:LANES]
        ck.append(c)
    lfn = _log_sigmoid(fft_ref[0] + bf_ref[...])
    lfo_ref[0] = lfn
    lane_t = lax.broadcasted_iota(jnp.int32, (H, nt), 1)
    cn = carry
    for j in range(nt):
        cn = cn + jnp.where(lane_t >= j, lfn[:, j:j + 1], 0.0)
    cq = jnp.concatenate([cn[:, t:t + 1] for t in range(nt)], axis=0)

    for j in range(n_pages):
        k_scr[:, j * page:(j + 1) * page] = k_refs[j][...].reshape(W, page).astype(BF16)
        v_scr[:, j * page:(j + 1) * page] = v_refs[j][...].reshape(W, page).astype(BF16)
    ck_all = jnp.concatenate(ck, axis=1)
    s_past = _dot(qbd_b, k_scr[...]) + (cq - jnp.concatenate([ck_all] * nt, axis=0))

    kn = kn_ref[0]
    vn = vn_ref[0]
    trow = lax.broadcasted_iota(jnp.int32, (nt * H, 1), 0) >> (H.bit_length() - 1)
    s_new = []
    for tp in range(nt):
        ckn = jnp.concatenate([cn[:, tp:tp + 1]] * nt, axis=0)
        sn = jnp.sum(qbd * kn[tp:tp + 1], axis=-1, keepdims=True) + (cq - ckn)
        s_new.append(jnp.where(trow >= tp, sn, NEG))
    m = jnp.max(s_past, axis=-1, keepdims=True)
    for sn in s_new:
        m = jnp.maximum(m, sn)
    p_past = jnp.exp(s_past - m)
    den = jnp.sum(p_past, axis=-1, keepdims=True)
    out = _dot_nt(p_past.astype(BF16), v_scr[...])
    for tp in range(nt):
        pn = jnp.exp(s_new[tp] - m)
        den = den + pn
        out = out + pn * vn[tp:tp + 1]
    out = out / den
    for t in range(nt):
        o_ref[0, t:t + 1, :] = jnp.sum(jnp.where(hmask, out[t * H:(t + 1) * H], 0.0), axis=0, keepdims=True)


def _fox_sample(pt_flat, q_s, kn_s, vn_s, fft_s, bf, cache_kt, cache_vt, cache_lft, layer, nb, nt, n_pages, page):
    W = FOX_HEADS * FOX_HD
    assert page == LANES

    seq3 = lambda s, pt: (s, 0, 0)
    in_specs = [
        pl.BlockSpec((1, nt, W), seq3),
        pl.BlockSpec((1, nt, W), seq3),
        pl.BlockSpec((1, nt, W), seq3),
        pl.BlockSpec((1, FOX_HEADS, nt), seq3),
        pl.BlockSpec((FOX_HEADS, 1), lambda s, pt: (0, 0)),
        pl.BlockSpec(memory_space=pl.ANY),
        pl.BlockSpec(memory_space=pl.ANY),
        pl.BlockSpec(memory_space=pl.ANY),
    ]
    grid_spec = pltpu.PrefetchScalarGridSpec(
        num_scalar_prefetch=1,
        grid=(nb,),
        in_specs=in_specs,
        out_specs=[pl.BlockSpec((1, nt, W), seq3), pl.BlockSpec((1, FOX_HEADS, nt), seq3)],
        scratch_shapes=[pltpu.VMEM((2, n_pages, FOX_HEADS, FOX_HD, page), F32),
                        pltpu.VMEM((2, n_pages, FOX_HEADS, FOX_HD, page), F32),
                        pltpu.VMEM((2, n_pages, FOX_HEADS, page), F32),
                        pltpu.VMEM((W, n_pages * page), BF16), pltpu.VMEM((W, n_pages * page), BF16),
                        pltpu.SemaphoreType.DMA((2, 3))],
    )
    return pl.pallas_call(
        functools.partial(_fox_sample_body, layer=layer, n_pages=n_pages, nt=nt, page=page),
        grid_spec=grid_spec,
        out_shape=[jax.ShapeDtypeStruct((nb, nt, W), F32), jax.ShapeDtypeStruct((nb, FOX_HEADS, nt), F32)],
        compiler_params=_cparams(("arbitrary",), 48),
    )(pt_flat, q_s, kn_s, vn_s, fft_s, bf, cache_kt, cache_vt, cache_lft)


def _causal_conv(ext_scr, u, cw_ref, H, shift):
    bm = u.shape[0]
    ext_scr[H:H + bm, :] = u
    out = cw_ref[CONV_K - 1:CONV_K, :] * u
    for i in range(1, CONV_K):
        lo = H - i * shift
        out = out + cw_ref[CONV_K - 1 - i:CONV_K - i, :] * ext_scr[lo:lo + bm, :]
    return out


def _mix_body(*refs, H, shift, tiles_per_seq, has_state):
    if has_state:
        h_ref, gates_ref, sc_ref, yg_ref, yf_ref, cw_ref, wb_ref, wo_ref, halo_ref, hout_ref, tail_ref, ext_scr = refs
    else:
        h_ref, gates_ref, sc_ref, yg_ref, yf_ref, cw_ref, wb_ref, wo_ref, hout_ref, tail_ref, ext_scr = refs
    bm = h_ref.shape[0]
    if has_state:
        ext_scr[0:H, :] = halo_ref[...]
    else:
        seq_start = pl.program_id(0) % tiles_per_seq == 0

        @pl.when(seq_start)
        def _():
            ext_scr[0:H, :] = jnp.zeros((H, SC_WIDTH), F32)

        @pl.when(jnp.logical_not(seq_start))
        def _():
            ext_scr[0:H, :] = ext_scr[bm:bm + H, :]

    sc = sc_ref[...].astype(F32)
    u = sc[:, SC_WIDTH:2 * SC_WIDTH] * sc[:, 2 * SC_WIDTH:]
    y_sc = (sc[:, :SC_WIDTH] * _causal_conv(ext_scr, u, cw_ref, H, shift)).astype(BF16)
    tail_ref[0] = ext_scr[bm:bm + H, :]

    acc = jnp.zeros((bm, D_MODEL), F32)
    for b, y in enumerate((yg_ref[...], yf_ref[...], y_sc)):
        g = jax.nn.sigmoid(gates_ref[:, b * D_MODEL:(b + 1) * D_MODEL].astype(F32))
        acc = acc + g * _dot(y, wb_ref[b])
    hout_ref[...] = h_ref[...] + _dot(acc.astype(BF16), wo_ref[...])


def _mix(h, z, yg, yf, cw, wb, wo, halo, bm, H, shift, tiles_per_seq):
    T = h.shape[0]
    has_state = halo is not None
    in_specs = [
        pl.BlockSpec((bm, D_MODEL), lambda i: (i, 0)),
        pl.BlockSpec((bm, N_BRANCH * D_MODEL), lambda i: (i, 0)),
        pl.BlockSpec((bm, 3 * SC_WIDTH), lambda i: (i, Z_SC // (3 * SC_WIDTH))),
        pl.BlockSpec((bm, SC_WIDTH), lambda i: (i, 0)),
        pl.BlockSpec((bm, SC_WIDTH), lambda i: (i, 0)),
        pl.BlockSpec((CONV_K, SC_WIDTH), lambda i: (0, 0)),
        pl.BlockSpec((N_BRANCH, SC_WIDTH, D_MODEL), lambda i: (0, 0, 0)),
        pl.BlockSpec((D_MODEL, D_MODEL), lambda i: (0, 0)),
    ]
    args = [h, z, z, yg, yf, cw, wb, wo]
    if has_state:
        in_specs.append(pl.BlockSpec((H, SC_WIDTH), lambda i: (0, 0)))
        args.append(halo)
    return pl.pallas_call(
        functools.partial(_mix_body, H=H, shift=shift, tiles_per_seq=tiles_per_seq, has_state=has_state),
        grid=(T // bm,),
        in_specs=in_specs,
        out_specs=[pl.BlockSpec((bm, D_MODEL), lambda i: (i, 0)),
                   pl.BlockSpec((1, H, SC_WIDTH), lambda i: (i, 0, 0))],
        out_shape=[jax.ShapeDtypeStruct((T, D_MODEL), F32),
                   jax.ShapeDtypeStruct((T // bm, H, SC_WIDTH), F32)],
        scratch_shapes=[pltpu.VMEM((H + bm, SC_WIDTH), F32)],
        compiler_params=_cparams(("arbitrary",), 48),
    )(*args)


FFN_CK = 1408


def _ffn_body(*refs, H, shift, tiles_per_seq, has_state, final_norm):
    refs = list(refs)
    h_ref, g_ref, wa_ref, wg_ref, cw_ref, cb_ref, wd_ref = refs[:7]
    refs = refs[7:]
    halo_ref = refs.pop(0) if has_state else None
    fg_ref = refs.pop(0) if final_norm else None
    out_ref, tail_ref, xn_scr, acc_scr, ext_scr, halo_scr = refs
    i = pl.program_id(0)
    c = pl.program_id(1)
    bm = h_ref.shape[0]

    @pl.when(c == 0)
    def _():
        xn_scr[...] = _rms(h_ref[...], g_ref[...]).astype(BF16)
        acc_scr[...] = jnp.zeros_like(acc_scr)

    xn = xn_scr[...]
    a = _dot(xn, wa_ref[...])
    gate = _dot(xn, wg_ref[...])
    if has_state:
        ext_scr[0:H, :] = halo_ref[...]
    else:
        seq_start = i % tiles_per_seq == 0

        @pl.when(seq_start)
        def _():
            ext_scr[0:H, :] = jnp.zeros((H, a.shape[1]), F32)

        @pl.when(jnp.logical_not(seq_start))
        def _():
            ext_scr[0:H, :] = halo_scr[c]

    ac =_causal_conv(ext_scr, a, cw_ref, H, shift) + cb_ref[...]
    tail = ext_scr[bm:bm + H, :]
    tail_ref[0] = tail
    if not has_state:
        halo_scr[c] = tail
    act = 0.5 * ac * (1.0 + lax.erf(ac * (2.0 ** -0.5))) * gate
    acc_scr[...] += _dot(act.astype(BF16), wd_ref[...])

    @pl.when(c == pl.num_programs(1) - 1)
    def _():
        hn = h_ref[...] + acc_scr[...]
        out_ref[...] = _rms(hn, fg_ref[...]) if final_norm else hn


def _ffn(h, g, wa, wg, cw, cb, wd, halo, fg, bm, H, shift, tiles_per_seq):
    T = h.shape[0]
    ck = FFN_CK
    nc = D_FF // ck
    has_state = halo is not None
    final_norm = fg is not None
    in_specs = [
        pl.BlockSpec((bm, D_MODEL), lambda i, c: (i, 0)),
        pl.BlockSpec((1, D_MODEL), lambda i, c: (0, 0)),
        pl.BlockSpec((D_MODEL, ck), lambda i, c: (0, c)),
        pl.BlockSpec((D_MODEL, ck), lambda i, c: (0, c)),
        pl.BlockSpec((CONV_K, ck), lambda i, c: (0, c)),
        pl.BlockSpec((1, ck), lambda i, c: (0, c)),
        pl.BlockSpec((ck, D_MODEL), lambda i, c: (c, 0)),
    ]
    args = [h, g, wa, wg, cw, cb, wd]
    if has_state:
        in_specs.append(pl.BlockSpec((H, ck), lambda i, c: (0, c)))
        args.append(halo)
    if final_norm:
        in_specs.append(pl.BlockSpec((1, D_MODEL), lambda i, c: (0, 0)))
        args.append(fg)
    return pl.pallas_call(
        functools.partial(_ffn_body, H=H, shift=shift, tiles_per_seq=tiles_per_seq, has_state=has_state,
                          final_norm=final_norm),
        grid=(T // bm, nc),
        in_specs=in_specs,
        out_specs=[pl.BlockSpec((bm, D_MODEL), lambda i, c: (i, 0)),
                   pl.BlockSpec((1, H, ck), lambda i, c: (i, 0, c))],
        out_shape=[jax.ShapeDtypeStruct((T, D_MODEL), F32),
                   jax.ShapeDtypeStruct((T // bm, H, D_FF), F32)],
        scratch_shapes=[pltpu.VMEM((bm, D_MODEL), BF16), pltpu.VMEM((bm, D_MODEL), F32),
                        pltpu.VMEM((H + bm, ck), F32), pltpu.VMEM((nc, H, ck), F32)],
        compiler_params=_cparams(("arbitrary", "arbitrary"), 56),
    )(*args)


def _pack_in_proj(w_in):
    sizes = (GLA_HEADS * GLA_DK, GLA_HEADS * GLA_DK, GLA_HEADS * GLA_DV, GLA_LR, GLA_HEADS * GLA_DV,
             FOX_HEADS * FOX_HD, FOX_HEADS * FOX_HD, FOX_HEADS * FOX_HD, FOX_HEADS,
             SC_WIDTH, SC_WIDTH, SC_WIDTH, N_BRANCH * D_MODEL)
    offs = np.concatenate([[0], np.cumsum(sizes)])
    part = lambda n: w_in[:, :, offs[n]:offs[n + 1]]
    gq, gk, gv, glr, gg, fq, fk, fv, ff, scb, scc, sch, gates = [part(n) for n in range(len(sizes))]
    w_main = jnp.concatenate([gates, gq, gk, gv, gg, fq, fk, fv, scb, scc, sch], axis=-1).astype(BF16)
    pad = jnp.zeros(w_in.shape[:2] + (SMALL_COLS - GLA_LR - FOX_HEADS,), w_in.dtype)
    w_small = jnp.concatenate([glr, ff, pad], axis=-1).astype(BF16)
    w_fft = jnp.swapaxes(ff, 1, 2).astype(BF16)
    return w_main, w_small, w_fft


def kernel(x_prompt, x_sample, cache_fox_k, cache_fox_v, cache_fox_logf, state_gla, state_sconv, state_ffn_conv,
           page_table, norm1_g, w_in, w_alpha2, b_alpha, b_forget, gla_norm_g, sconv_w, w_branch, w_out, norm2_g,
           w_up, ffn_conv_w, ffn_conv_b, w_down, final_norm_g):
    depth = w_in.shape[0]
    B, L, _ = x_prompt.shape
    nb, nt, _ = x_sample.shape
    n_pool, page = cache_fox_k.shape[1], cache_fox_k.shape[2]
    n_pages = page_table.shape[1]
    W = FOX_HEADS * FOX_HD
    Tp = B * L
    Ts = nb * nt
    bm_p = min(512, L)
    bm_in = min(1024, L)
    fox_bq, fox_bk = min(512, L), min(512, L)
    H_p, H_s = SUBLANES, (CONV_K - 1) * nb

    w_main, w_small, w_fft = _pack_in_proj(w_in)
    wa_pad = jnp.concatenate(
        [w_alpha2, jnp.zeros((depth, SMALL_COLS - GLA_LR, GLA_HEADS * GLA_DK), w_alpha2.dtype)], axis=1).astype(BF16)
    wa_heads = wa_pad.reshape(depth, SMALL_COLS, GLA_HEADS, GLA_DK).transpose(0, 2, 1, 3)
    ba_heads = b_alpha.reshape(depth, GLA_HEADS, 1, GLA_DK)
    wb = w_branch.astype(BF16)
    wo = w_out.astype(BF16)
    wu_a = w_up[:, :, :D_FF].astype(BF16)
    wu_g = w_up[:, :, D_FF:].astype(BF16)
    wd = w_down.astype(BF16)

    cache_kt = jnp.transpose(cache_fox_k, (0, 1, 3, 4, 2))
    cache_vt = jnp.transpose(cache_fox_v, (0, 1, 3, 4, 2))
    cache_lft = jnp.swapaxes(cache_fox_logf, 2, 3)
    pt_flat = page_table.reshape(-1).astype(jnp.int32)

    def to_seq(a):
        return jnp.swapaxes(a.reshape((a.shape[0] // nb, nb) + a.shape[1:]), 0, 1)

    def to_time(a):
        return jnp.swapaxes(a, 0, 1).reshape((a.shape[0] * a.shape[1],) + a.shape[2:])

    hp = x_prompt.reshape(Tp, D_MODEL)
    hs = to_time(x_sample)
    outs_p, outs_s = [], []
    for l in range(depth):
        g1 = norm1_g[l].reshape(1, D_MODEL)
        g2 = norm2_g[l].reshape(1, D_MODEL)
        bf = b_forget[l].reshape(FOX_HEADS, 1)
        ba = b_alpha[l].reshape(1, -1)
        gn = gla_norm_g[l].reshape(1, GLA_DV)
        cb = ffn_conv_b[l].reshape(1, D_FF)
        fg = final_norm_g.reshape(1, D_MODEL) if l == depth - 1 else None

        z, fkt, fvt, small, fft = _in_proj(hp, g1, w_main[l], w_small[l], w_fft[l], bm_in, (B, L))
        yg, st, logf_t, c_t = _gla_prompt(z, small, fft, wa_pad[l], ba, bf, gn, B, L)
        c_rows = c_t.reshape(FOX_HEADS // 2, 2, Tp)
        yf = _fox_prompt(z, jnp.swapaxes(c_rows, 1, 2), c_rows, B, L, fox_bq, fox_bk)
        hp, sc_tail = _mix(hp, z, yg, yf, sconv_w[l], wb[l], wo[l], None, bm_p, H_p, 1, L // bm_p)
        hp, ffn_tail = _ffn(hp, g2, wu_a[l], wu_g[l], ffn_conv_w[l], cb, wd[l], None, fg, bm_p, H_p, 1, L // bm_p)
        last = lambda t: t.reshape(B, L // bm_p, H_p, -1)[:, -1, H_p - (CONV_K - 1):, :]
        p_gla = st.reshape(B, GLA_HEADS // 2, GLA_DV, 2, GLA_DK).transpose(0, 1, 3, 4, 2)
        heads_last = lambda a: a.reshape(B, FOX_HEADS, FOX_HD, L).transpose(0, 3, 1, 2)
        outs_p.append((heads_last(fkt), heads_last(fvt), logf_t.T.reshape(B, L, FOX_HEADS), p_gla.reshape(B, GLA_HEADS, GLA_DK, GLA_DV),
                       last(sc_tail), last(ffn_tail)))

        z, fk, fv, small, fft = _in_proj(hs, g1, w_main[l], w_small[l], w_fft[l], Ts)
        qk_h = z[:, Z_GQK:Z_GV].reshape(Ts, 2, GLA_HEADS, GLA_DK).transpose(1, 2, 0, 3)
        yg, s1 = _gla_sample(qk_h[0], qk_h[1], z, small, wa_heads[l], ba_heads[l], gn,
                             state_gla[l].reshape(nb, -1), nb, nt)
        yf_s, lfn = _fox_sample(pt_flat, to_seq(z[:, Z_FQ:Z_FK].astype(F32)), to_seq(fk), to_seq(fv),
                                jnp.swapaxes(fft.reshape(FOX_HEADS, nt, nb), 0, 2).swapaxes(1, 2), bf,
                                cache_kt, cache_vt, cache_lft, l, nb, nt, n_pages, page)
        yf = to_time(yf_s).astype(BF16)
        halo_sc = to_time(state_sconv[l])
        halo_ffn = to_time(state_ffn_conv[l])
        hs, sc_tail = _mix(hs, z, yg, yf, sconv_w[l], wb[l], wo[l], halo_sc, Ts, H_s, nb, 1)
        hs, ffn_tail = _ffn(hs, g2, wu_a[l], wu_g[l], ffn_conv_w[l], cb, wd[l], halo_ffn, fg, Ts, H_s, nb, 1)
        outs_s.append((to_seq(fk).reshape(nb, nt, FOX_HEADS, FOX_HD), to_seq(fv).reshape(nb, nt, FOX_HEADS, FOX_HD),
                       jnp.swapaxes(lfn, 1, 2), s1.reshape(nb, GLA_HEADS, GLA_DK, GLA_DV),
                       to_seq(sc_tail[0]), to_seq(ffn_tail[0])))

    p_states = [jnp.stack(t) for t in zip(*outs_p)]
    s_states = [jnp.stack(t) for t in zip(*outs_s)]
    y_prompt = hp.reshape(B, L, D_MODEL)
    y_sample = to_seq(hs)
    return (y_prompt, y_sample, *p_states, *s_states)
```

```python
import functools

import numpy as np
import jax
import jax.numpy as jnp
from jax import lax
from jax.experimental import pallas as pl
from jax.experimental.pallas import tpu as pltpu

F32 = jnp.float32
BF16 = jnp.bfloat16

D_MODEL = 1024
GLA_HEADS = 4
GLA_DK = 64
GLA_DV = 128
GLA_LR = 16
GLA_TAU = 16.0
FOX_HEADS = 8
FOX_HD = 64
SC_WIDTH = 512
CONV_K = 3
D_FF = 2816
N_BRANCH = 3
EPS = 1e-6

LANES = 128
SUBLANES = 8
VMEM_BYTES = 64 << 20

Z_GATES = 0
Z_GQK = 3072
Z_GV = 3584
Z_GG = 4096
Z_FQ = 4608
Z_FK = 5120
Z_FV = 5632
Z_SC = 6144
Z_COLS = 7680
IN_BN = 1536
SMALL_COLS = 128

GLA_CHUNK = 128
NEG = -1e30


def _cparams(sem, vmem_mb):
    return pltpu.CompilerParams(dimension_semantics=sem, vmem_limit_bytes=min(vmem_mb << 20, VMEM_BYTES - (4 << 20)))


def _log_sigmoid(x):
    return jnp.minimum(x, 0.0) - jnp.log1p(jnp.exp(-jnp.abs(x)))


def _dot(a, b):
    return jnp.dot(a, b, preferred_element_type=F32)


def _dot_nt(a, b):
    return lax.dot_general(a, b, (((1,), (1,)), ((), ())), preferred_element_type=F32)


def _dot_tn(a, b):
    return lax.dot_general(a, b, (((0,), (0,)), ((), ())), preferred_element_type=F32)


def _rms(x, g):
    return x * lax.rsqrt(jnp.mean(x * x, axis=-1, keepdims=True) + EPS) * g


def _lane_prefix_sum(c, lane):
    sh = 1
    while sh < LANES:
        c = c + jnp.where(lane >= sh, pltpu.roll(c, sh, axis=1), 0.0)
        sh *= 2
    return c


def _in_proj_body(x_ref, g_ref, w_ref, ws_ref, wft_ref, z_ref, fk_ref, fv_ref, small_ref, fft_ref, xn_scr,
                  *, transposed):
    j = pl.program_id(1)

    @pl.when(j == 0)
    def _():
        xb = _rms(x_ref[...], g_ref[...]).astype(BF16)
        xn_scr[...] = xb
        small_ref[...] = _dot(xb, ws_ref[...])
        fft_ref[...] = _dot_nt(wft_ref[...], xb)

    zt = _dot(xn_scr[...], w_ref[...])
    z_ref[...] = zt.astype(BF16)

    W = FOX_HEADS * FOX_HD

    def part(off):
        blk = zt[:, off % IN_BN:off % IN_BN + W]
        return blk.T if transposed else blk

    @pl.when(j == Z_FK // IN_BN)
    def _():
        fk_ref[...] = part(Z_FK)

    @pl.when(j == Z_FV // IN_BN)
    def _():
        fv_ref[...] = part(Z_FV)


def _in_proj(x, g, w, ws, wft, bm, seqs=None):
    T = x.shape[0]
    W = FOX_HEADS * FOX_HD
    grid = (T // bm, Z_COLS // IN_BN)
    if seqs is None:
        kv_spec = pl.BlockSpec((bm, W), lambda i, j: (i, 0))
        kv_shape = jax.ShapeDtypeStruct((T, W), F32)
    else:
        B, L = seqs
        per_seq = L // bm
        kv_spec = pl.BlockSpec((None, W, bm), lambda i, j: (i // per_seq, 0, i % per_seq))
        kv_shape = jax.ShapeDtypeStruct((B, W, L), F32)
    return pl.pallas_call(
        functools.partial(_in_proj_body, transposed=seqs is not None),
        grid=grid,
        in_specs=[
            pl.BlockSpec((bm, D_MODEL), lambda i, j: (i, 0)),
            pl.BlockSpec((1, D_MODEL), lambda i, j: (0, 0)),
            pl.BlockSpec((D_MODEL, IN_BN), lambda i, j: (0, j)),
            pl.BlockSpec((D_MODEL, SMALL_COLS), lambda i, j: (0, 0)),
            pl.BlockSpec((FOX_HEADS, D_MODEL), lambda i, j: (0, 0)),
        ],
        out_specs=[
            pl.BlockSpec((bm, IN_BN), lambda i, j: (i, j)),
            kv_spec,
            kv_spec,
            pl.BlockSpec((bm, SMALL_COLS), lambda i, j: (i, 0)),
            pl.BlockSpec((FOX_HEADS, bm), lambda i, j: (0, i)),
        ],
        out_shape=[
            jax.ShapeDtypeStruct((T, Z_COLS), BF16),
            kv_shape,
            kv_shape,
            jax.ShapeDtypeStruct((T, SMALL_COLS), F32),
            jax.ShapeDtypeStruct((FOX_HEADS, T), F32),
        ],
        scratch_shapes=[pltpu.VMEM((bm, D_MODEL), BF16)],
        compiler_params=_cparams(("arbitrary", "arbitrary"), 56),
    )(x, g, w, ws, wft)


def _gla_consts(C):
    t = np.arange(C)
    r = t[None, :]
    mats = [r <= t[:, None], r > t[:, None]]
    masks = []
    n = C // 2
    while n >= 1:
        pair = t // (2 * n)
        second = (t // n) % 2 == 1
        mid = pair * 2 * n + n
        mats.append(np.where(second[:, None], (r >= mid[:, None]) & (r <= t[:, None]),
                             (r > t[:, None]) & (r < mid[:, None])))
        masks.append(second[:, None] & ~second[None, :] & (pair[:, None] == pair[None, :]))
        n //= 2
    masks.append(np.eye(C, dtype=bool))
    return (jnp.asarray(np.concatenate(mats, 0).astype(np.float32), BF16),
            jnp.asarray(np.stack(masks).astype(np.float32)))


def _gla_prompt_body(qk_ref, v_ref, gg_ref, small_ref, fft_ref, wa_ref, ba_ref, bf_ref, gn_ref, wall_ref,
                     masks_ref, y_ref, st_ref, logf_ref, c_ref, s_scr, c_scr, *, C, nlev):
    i = pl.program_id(1)

    @pl.when(i == 0)
    def _():
        s_scr[...] = jnp.zeros_like(s_scr)
        c_scr[...] = jnp.zeros_like(c_scr)

    lf = _log_sigmoid(fft_ref[...] + bf_ref[...])
    logf_ref[...] = lf
    lane = lax.broadcasted_iota(jnp.int32, lf.shape, 1)
    c = _lane_prefix_sum(lf, lane) + c_scr[...]
    c_ref[...] = c
    c_scr[...] = jnp.broadcast_to(c[:, C - 1:C], c_scr.shape)

    la = _log_sigmoid(_dot(small_ref[...].astype(BF16), wa_ref[...]) + ba_ref[...]) * (1.0 / GLA_TAU)
    hi = la.astype(BF16)
    lo = (la - hi.astype(F32)).astype(BF16)
    wall = wall_ref[...]
    P = jnp.exp(_dot(wall, hi) + _dot(wall, lo))

    W = GLA_HEADS * GLA_DK
    qk = qk_ref[...].astype(F32)
    q = qk[:, :W] * (GLA_DK ** -0.5)
    k = qk[:, W:]
    qd = (q * P[0:C]).astype(BF16)
    kd = (k * P[C:2 * C]).astype(BF16)
    a_last = P[C - 1:C, :]
    row = lax.broadcasted_iota(jnp.int32, (C, 1), 0)
    X = []
    n = C // 2
    for l in range(nlev):
        X.append((jnp.where((row & n) != 0, q, k) * P[(2 + l) * C:(3 + l) * C]).astype(BF16))
        n //= 2
    qb = q.astype(BF16)
    kb = k.astype(BF16)

    lane_w = lax.broadcasted_iota(jnp.int32, (1, W), 1) >> (GLA_DK.bit_length() - 1)

    def by_head(a):
        return jnp.concatenate([jnp.where(lane_w == h, a, jnp.zeros_like(a)) for h in range(GLA_HEADS)], axis=0)

    scores = masks_ref[nlev] * _dot_nt(by_head(qb), kb).reshape(GLA_HEADS, C, C)
    for l in range(nlev):
        scores = scores + masks_ref[l] * _dot_nt(by_head(X[l]), X[l]).reshape(GLA_HEADS, C, C)

    lane_p = lax.broadcasted_iota(jnp.int32, (1, LANES), 1)
    for p in range(GLA_HEADS // 2):
        sl = slice(LANES * p, LANES * (p + 1))
        ST = s_scr[p]
        STb = ST.astype(BF16)
        U = []
        for hh in range(2):
            h = 2 * p + hh
            hm = (lane_p >= GLA_DK) == bool(hh)

            def msk(a):
                return jnp.where(hm, a, jnp.zeros_like(a))

            vh = v_ref[:, GLA_DV * h:GLA_DV * (h + 1)]
            o = _dot(scores[h].astype(BF16), vh) + _dot_nt(msk(qd[:, sl]), STb)
            U.append(_dot_tn(vh, kd[:, sl]))
            g = gg_ref[:, GLA_DV * h:GLA_DV * (h + 1)].astype(F32)
            y_ref[:, GLA_DV * h:GLA_DV * (h + 1)] = (_rms(o, gn_ref[...]) * (g * jax.nn.sigmoid(g))).astype(BF16)
        s_scr[p] = ST * a_last[:, sl] + jnp.where(lane_p < GLA_DK, U[0], U[1])

    @pl.when(i == pl.num_programs(1) - 1)
    def _():
        st_ref[0] = s_scr[...]


def _gla_prompt(z, small, fft, wa, ba, bf, gn, B, L):
    C = GLA_CHUNK
    nL = L // C
    wall, masks = _gla_consts(C)
    nlev = masks.shape[0] - 1
    T = B * L
    W = GLA_HEADS * GLA_DK
    row = lambda b, i: b * nL + i
    return pl.pallas_call(
        functools.partial(_gla_prompt_body, C=C, nlev=nlev),
        grid=(B, nL),
        in_specs=[
            pl.BlockSpec((C, 2 * W), lambda b, i: (row(b, i), Z_GQK // (2 * W))),
            pl.BlockSpec((C, GLA_HEADS * GLA_DV), lambda b, i: (row(b, i), Z_GV // (GLA_HEADS * GLA_DV))),
            pl.BlockSpec((C, GLA_HEADS * GLA_DV), lambda b, i: (row(b, i), Z_GG // (GLA_HEADS * GLA_DV))),
            pl.BlockSpec((C, SMALL_COLS), lambda b, i: (row(b, i), 0)),
            pl.BlockSpec((FOX_HEADS, C), lambda b, i: (0, row(b, i))),
            pl.BlockSpec((SMALL_COLS, W), lambda b, i: (0, 0)),
            pl.BlockSpec((1, W), lambda b, i: (0, 0)),
            pl.BlockSpec((FOX_HEADS, 1), lambda b, i: (0, 0)),
            pl.BlockSpec((1, GLA_DV), lambda b, i: (0, 0)),
            pl.BlockSpec(wall.shape, lambda b, i: (0, 0)),
            pl.BlockSpec(masks.shape, lambda b, i: (0, 0, 0)),
        ],
        out_specs=[
            pl.BlockSpec((C, GLA_HEADS * GLA_DV), lambda b, i: (row(b, i), 0)),
            pl.BlockSpec((1, GLA_HEADS // 2, GLA_DV, 2 * GLA_DK), lambda b, i: (b, 0, 0, 0)),
            pl.BlockSpec((FOX_HEADS, C), lambda b, i: (0, row(b, i))),
            pl.BlockSpec((FOX_HEADS, C), lambda b, i: (0, row(b, i))),
        ],
        out_shape=[
            jax.ShapeDtypeStruct((T, GLA_HEADS * GLA_DV), BF16),
            jax.ShapeDtypeStruct((B, GLA_HEADS // 2, GLA_DV, 2 * GLA_DK), F32),
            jax.ShapeDtypeStruct((FOX_HEADS, T), F32),
            jax.ShapeDtypeStruct((FOX_HEADS, T), F32),
        ],
        scratch_shapes=[pltpu.VMEM((GLA_HEADS // 2, GLA_DV, 2 * GLA_DK), F32),
                        pltpu.VMEM((FOX_HEADS, LANES), F32)],
        compiler_params=_cparams(("arbitrary", "arbitrary"), 32),
    )(z, z, z, small, fft, wa, ba, bf, gn, wall, masks)


def _gla_sample_body(q_ref, k_ref, v_ref, gg_ref, small_ref, wa_ref, ba_ref, gn_ref, s0_ref, y_ref, s1_ref,
                     *, nb, nt):
    la = _log_sigmoid(_dot(small_ref[...].astype(BF16), wa_ref[0]) + ba_ref[0]) * (1.0 / GLA_TAU)
    q = q_ref[0].astype(F32) * (GLA_DK ** -0.5)
    k = k_ref[0].astype(F32)
    rows = [slice(t * nb, (t + 1) * nb) for t in range(nt)]
    b = []
    for t in range(nt):
        b.append(la[rows[t]] if t == 0 else b[-1] + la[rows[t]])
    qt = [q[r] for r in rows]
    kt = [k[r] for r in rows]
    qd = [qt[t] * jnp.exp(b[t]) for t in range(nt)]
    kd = [kt[t] * jnp.exp(b[nt - 1] - b[t]) for t in range(nt)]
    a_last = jnp.exp(b[nt - 1])

    for d in range(GLA_DK):
        cs = slice(d * GLA_DV, (d + 1) * GLA_DV)
        acc = a_last[:, d:d + 1] * s0_ref[:, cs]
        for t in range(nt):
            acc = acc + kd[t][:, d:d + 1] * v_ref[rows[t], :].astype(F32)
        s1_ref[:, cs] = acc

    for t in range(nt):
        o = jnp.zeros((nb, GLA_DV), F32)
        for s in range(t + 1):
            w = jnp.sum(qt[t] * kt[s] * jnp.exp(b[t] - b[s]), axis=-1, keepdims=True)
            o = o + w * v_ref[rows[s], :].astype(F32)
        for d in range(GLA_DK):
            o = o + qd[t][:, d:d + 1] * s0_ref[:, d * GLA_DV:(d + 1) * GLA_DV]
        g = gg_ref[rows[t], :].astype(F32)
        y_ref[rows[t], :] = (_rms(o, gn_ref[...]) * (g * jax.nn.sigmoid(g))).astype(BF16)


def _gla_sample(qh, kh, z, small, wa_h, ba_h, gn, s0, nb, nt):
    T = nb * nt
    slab = GLA_DK * GLA_DV
    return pl.pallas_call(
        functools.partial(_gla_sample_body, nb=nb, nt=nt),
        grid=(GLA_HEADS,),
        in_specs=[
            pl.BlockSpec((1, T, GLA_DK), lambda h: (h, 0, 0)),
            pl.BlockSpec((1, T, GLA_DK), lambda h: (h, 0, 0)),
            pl.BlockSpec((T, GLA_DV), lambda h: (0, Z_GV // GLA_DV + h)),
            pl.BlockSpec((T, GLA_DV), lambda h: (0, Z_GG // GLA_DV + h)),
            pl.BlockSpec((T, SMALL_COLS), lambda h: (0, 0)),
            pl.BlockSpec((1, SMALL_COLS, GLA_DK), lambda h: (h, 0, 0)),
            pl.BlockSpec((1, 1, GLA_DK), lambda h: (h, 0, 0)),
            pl.BlockSpec((1, GLA_DV), lambda h: (0, 0)),
            pl.BlockSpec((nb, slab), lambda h: (0, h)),
        ],
        out_specs=[
            pl.BlockSpec((T, GLA_DV), lambda h: (0, h)),
            pl.BlockSpec((nb, slab), lambda h: (0, h)),
        ],
        out_shape=[
            jax.ShapeDtypeStruct((T, GLA_HEADS * GLA_DV), BF16),
            jax.ShapeDtypeStruct((nb, GLA_HEADS * slab), F32),
        ],
        compiler_params=_cparams(("arbitrary",), 40),
    )(qh, kh, z, z, small, wa_h, ba_h, gn, s0)


def _split3(c):
    def top(x):
        bits = lax.bitcast_convert_type(x, jnp.uint32) & jnp.uint32(0xFFFF0000)
        return lax.bitcast_convert_type(bits, F32)

    hi = top(c)
    mid = top(c - hi)
    return hi, mid, c - hi - mid


FOX_PREP_ROWS = 512
FOX_SUM_ROWS = 16
LOG2E = 1.4426950408889634


def _fox_prompt_body(q_ref, k_ref, v_ref, ck_ref, cq_ref, o_ref, k_scr, vt_scr, s_scr, *, bq, bk):
    i = pl.program_id(2)
    row0 = i * bq
    n_full = lax.div(row0, bk)
    lane = lax.broadcasted_iota(jnp.int32, (1, LANES), 1)
    NB = 3

    @pl.when(i == 0)
    def _():
        ch = min(FOX_PREP_ROWS, k_ref.shape[0])

        def prep(r, carry):
            start = pl.multiple_of(r * ch, ch)
            kc = k_ref[pl.ds(start, ch), :].astype(F32)
            cc = ck_ref[0, pl.ds(start, ch), :]
            vt = v_ref[pl.ds(start, ch), :].astype(F32).T
            ones = jnp.ones((FOX_SUM_ROWS, ch), F32)
            for hh in range(2):
                kh = kc if hh == 0 else pltpu.roll(kc, FOX_HD, axis=1)
                tile = jnp.where(lane < FOX_HD + NB, 1.0, 0.0)
                for n, term in enumerate(_split3(cc[:, hh:hh + 1] * LOG2E)):
                    tile = jnp.where(lane == FOX_HD + NB + n, -term, tile)
                k_scr[hh, pl.ds(start, ch), :] = jnp.where(lane < FOX_HD, kh, tile).astype(BF16)
                vt_scr[hh, :, pl.ds(start, ch)] = jnp.concatenate(
                    [vt[hh * FOX_HD:(hh + 1) * FOX_HD], ones], axis=0).astype(BF16)
            return carry

        lax.fori_loop(0, k_ref.shape[0] // ch, prep, 0)

    qt = (q_ref[...].astype(F32) * (FOX_HD ** -0.5 * LOG2E)).T
    row = lax.broadcasted_iota(jnp.int32, (LANES, 1), 0)
    qts = []
    for hh in range(2):
        tile = jnp.concatenate([qt[hh * FOX_HD:(hh + 1) * FOX_HD], jnp.zeros((LANES - FOX_HD, bq), F32)], axis=0)
        tile = jnp.where((row >= FOX_HD + NB) & (row < FOX_HD + 2 * NB), 1.0, tile)
        for n, term in enumerate(_split3(cq_ref[0, hh:hh + 1, :] * LOG2E)):
            tile = jnp.where(row == FOX_HD + n, term, tile)
        qts.append(tile.astype(BF16))

    def scores(j):
        start = pl.multiple_of(j * bk, bk)
        return tuple(_dot(k_scr[hh, pl.ds(start, bk), :], qts[hh]) for hh in range(2))

    def softmax(j, s, m, masked):
        out = []
        for hh in range(2):
            sh = s[hh]
            if masked:
                key = j * bk + lax.broadcasted_iota(jnp.int32, (bk, bq), 0)
                qry = row0 + lax.broadcasted_iota(jnp.int32, (bk, bq), 1)
                sh = jnp.where(key <= qry, sh, NEG)
            m_new = jnp.maximum(m[hh], jnp.max(sh, axis=0, keepdims=True))
            out.append((m_new, jnp.exp2(sh - m_new).astype(BF16), jnp.exp2(m[hh] - m_new)))
        return tuple(zip(*out))

    def accumulate(j, p, alpha, acc):
        start = pl.multiple_of(j * bk, bk)
        return tuple(alpha[hh] * acc[hh] + _dot(vt_scr[hh, :, pl.ds(start, bk)], p[hh]) for hh in range(2))

    def put_scores(j):
        for hh, sh in enumerate(scores(j)):
            s_scr[hh] = sh

    def step(j, carry):
        m, acc = carry
        m, p, alpha = softmax(j, (s_scr[0], s_scr[1]), m, False)
        put_scores(j + 1)
        return m, accumulate(j, p, alpha, acc)

    two = lambda a: (a, a)
    put_scores(0)
    acc0 = jnp.zeros((FOX_HD + FOX_SUM_ROWS, bq), F32)
    m, acc = lax.fori_loop(0, n_full, step, (two(jnp.full((1, bq), NEG, F32)), two(acc0)))
    m, p, alpha = softmax(n_full, (s_scr[0], s_scr[1]), m, True)
    acc = accumulate(n_full, p, alpha, acc)
    o = jnp.concatenate([a[:FOX_HD] / a[FOX_HD:FOX_HD + 1] for a in acc], axis=0)
    o_ref[...] = o.T.astype(BF16)


def _fox_prompt(z, c_cols, c_rows, B, L, bq, bk):
    nq = L // bq
    T = B * L
    npair = FOX_HEADS // 2
    return pl.pallas_call(
        functools.partial(_fox_prompt_body, bq=bq, bk=bk),
        grid=(B, npair, nq),
        in_specs=[
            pl.BlockSpec((bq, LANES), lambda b, p, i: (b * nq + i, Z_FQ // LANES + p)),
            pl.BlockSpec((L, LANES), lambda b, p, i: (b, Z_FK // LANES + p)),
            pl.BlockSpec((L, LANES), lambda b, p, i: (b, Z_FV // LANES + p)),
            pl.BlockSpec((1, L, 2), lambda b, p, i: (p, b, 0)),
            pl.BlockSpec((1, 2, bq), lambda b, p, i: (p, 0, b * nq + i)),
        ],
        out_specs=pl.BlockSpec((bq, LANES), lambda b, p, i: (b * nq + i, p)),
        out_shape=jax.ShapeDtypeStruct((T, FOX_HEADS * FOX_HD), BF16),
        scratch_shapes=[pltpu.VMEM((2, L, LANES), BF16), pltpu.VMEM((2, FOX_HD + FOX_SUM_ROWS, L), BF16),
                        pltpu.VMEM((2, bk, bq), F32)],
        compiler_params=_cparams(("arbitrary", "arbitrary", "arbitrary"), 56),
    )(z, z, z, c_cols, c_rows)


def _fox_sample_body(pt_ref, q_ref, kn_ref, vn_ref, fft_ref, bf_ref, kt_hbm, vt_hbm, lft_hbm, o_ref, lfo_ref,
                     kbuf, vbuf, lfbuf, k_scr, v_scr, sem, *, layer, n_pages, nt, page):
    W = FOX_HEADS * FOX_HD
    H = FOX_HEADS
    s = pl.program_id(0)
    slot = lax.rem(s, 2)

    def page_copies(seq, slot_):
        out = []
        for j in range(n_pages):
            pid = pt_ref[seq * n_pages + j]
            out.append(pltpu.make_async_copy(kt_hbm.at[layer, pid], kbuf.at[slot_, j], sem.at[slot_, 0]))
            out.append(pltpu.make_async_copy(vt_hbm.at[layer, pid], vbuf.at[slot_, j], sem.at[slot_, 1]))
            out.append(pltpu.make_async_copy(lft_hbm.at[layer, pid], lfbuf.at[slot_, j], sem.at[slot_, 2]))
        return out

    @pl.when(s == 0)
    def _():
        for c in page_copies(s, slot):
            c.start()

    @pl.when(s + 1 < pl.num_programs(0))
    def _():
        for c in page_copies(s + 1, 1 - slot):
            c.start()

    for n, (hbm, buf) in enumerate(((kt_hbm, kbuf), (vt_hbm, vbuf), (lft_hbm, lfbuf))):
        pltpu.make_async_copy(hbm.at[layer, pl.ds(0, n_pages)], buf.at[slot], sem.at[slot, n]).wait()
    k_refs = [kbuf.at[slot, j] for j in range(n_pages)]
    v_refs = [vbuf.at[slot, j] for j in range(n_pages)]
    lf_refs = [lfbuf.at[slot, j] for j in range(n_pages)]

    q = q_ref[0] * (FOX_HD ** -0.5)
    head_shift = FOX_HD.bit_length() - 1
    hmask = (lax.broadcasted_iota(jnp.int32, (H, W), 1) >> head_shift) == lax.broadcasted_iota(jnp.int32, (H, W), 0)
    qbd = jnp.concatenate([jnp.where(hmask, jnp.broadcast_to(q[t:t + 1], (H, W)), 0.0) for t in range(nt)], axis=0)
    qbd_b = qbd.astype(BF16)

    lane = lax.broadcasted_iota(jnp.int32, (H, LANES), 1)
    carry = jnp.zeros((H, 1), F32)
    ck = []
    for j in range(n_pages):
        c = _lane_prefix_sum(lf_refs[j][...], lane) + carry
        carry = c[:, LANES - 1:LANES]
        ck.append(c)
    lfn = _log_sigmoid(fft_ref[0] + bf_ref[...])
    lfo_ref[0] = lfn
    lane_t = lax.broadcasted_iota(jnp.int32, (H, nt), 1)
    cn = carry
    for j in range(nt):
        cn = cn + jnp.where(lane_t >= j, lfn[:, j:j + 1], 0.0)
    cq = jnp.concatenate([cn[:, t:t + 1] for t in range(nt)], axis=0)

    for j in range(n_pages):
        k_scr[:, j * page:(j + 1) * page] = k_refs[j][...].reshape(W, page).astype(BF16)
        v_scr[:, j * page:(j + 1) * page] = v_refs[j][...].reshape(W, page).astype(BF16)
    ck_all = jnp.concatenate(ck, axis=1)
    s_past = _dot(qbd_b, k_scr[...]) + (cq - jnp.concatenate([ck_all] * nt, axis=0))

    kn = kn_ref[0]
    vn = vn_ref[0]
    trow = lax.broadcasted_iota(jnp.int32, (nt * H, 1), 0) >> (H.bit_length() - 1)
    s_new = []
    for tp in range(nt):
        ckn = jnp.concatenate([cn[:, tp:tp + 1]] * nt, axis=0)
        sn = jnp.sum(qbd * kn[tp:tp + 1], axis=-1, keepdims=True) + (cq - ckn)
        s_new.append(jnp.where(trow >= tp, sn, NEG))
    m = jnp.max(s_past, axis=-1, keepdims=True)
    for sn in s_new:
        m = jnp.maximum(m, sn)
    p_past = jnp.exp(s_past - m)
    den = jnp.sum(p_past, axis=-1, keepdims=True)
    out = _dot_nt(p_past.astype(BF16), v_scr[...])
    for tp in range(nt):
        pn = jnp.exp(s_new[tp] - m)
        den = den + pn
        out = out + pn * vn[tp:tp + 1]
    out = out / den
    for t in range(nt):
        o_ref[0, t:t + 1, :] = jnp.sum(jnp.where(hmask, out[t * H:(t + 1) * H], 0.0), axis=0, keepdims=True)


def _fox_sample(pt_flat, q_s, kn_s, vn_s, fft_s, bf, cache_kt, cache_vt, cache_lft, layer, nb, nt, n_pages, page):
    W = FOX_HEADS * FOX_HD
    assert page == LANES

    seq3 = lambda s, pt: (s, 0, 0)
    in_specs = [
        pl.BlockSpec((1, nt, W), seq3),
        pl.BlockSpec((1, nt, W), seq3),
        pl.BlockSpec((1, nt, W), seq3),
        pl.BlockSpec((1, FOX_HEADS, nt), seq3),
        pl.BlockSpec((FOX_HEADS, 1), lambda s, pt: (0, 0)),
        pl.BlockSpec(memory_space=pl.ANY),
        pl.BlockSpec(memory_space=pl.ANY),
        pl.BlockSpec(memory_space=pl.ANY),
    ]
    grid_spec = pltpu.PrefetchScalarGridSpec(
        num_scalar_prefetch=1,
        grid=(nb,),
        in_specs=in_specs,
        out_specs=[pl.BlockSpec((1, nt, W), seq3), pl.BlockSpec((1, FOX_HEADS, nt), seq3)],
        scratch_shapes=[pltpu.VMEM((2, n_pages, FOX_HEADS, FOX_HD, page), F32),
                        pltpu.VMEM((2, n_pages, FOX_HEADS, FOX_HD, page), F32),
                        pltpu.VMEM((2, n_pages, FOX_HEADS, page), F32),
                        pltpu.VMEM((W, n_pages * page), BF16), pltpu.VMEM((W, n_pages * page), BF16),
                        pltpu.SemaphoreType.DMA((2, 3))],
    )
    return pl.pallas_call(
        functools.partial(_fox_sample_body, layer=layer, n_pages=n_pages, nt=nt, page=page),
        grid_spec=grid_spec,
        out_shape=[jax.ShapeDtypeStruct((nb, nt, W), F32), jax.ShapeDtypeStruct((nb, FOX_HEADS, nt), F32)],
        compiler_params=_cparams(("arbitrary",), 48),
    )(pt_flat, q_s, kn_s, vn_s, fft_s, bf, cache_kt, cache_vt, cache_lft)


def _causal_conv(ext_scr, u, cw_ref, H, shift):
    bm = u.shape[0]
    ext_scr[H:H + bm, :] = u
    out = cw_ref[CONV_K - 1:CONV_K, :] * u
    for i in range(1, CONV_K):
        lo = H - i * shift
        out = out + cw_ref[CONV_K - 1 - i:CONV_K - i, :] * ext_scr[lo:lo + bm, :]
    return out


def _mix_body(*refs, H, shift, tiles_per_seq, has_state):
    if has_state:
        h_ref, gates_ref, sc_ref, yg_ref, yf_ref, cw_ref, wb_ref, wo_ref, halo_ref, hout_ref, tail_ref, ext_scr = refs
    else:
        h_ref, gates_ref, sc_ref, yg_ref, yf_ref, cw_ref, wb_ref, wo_ref, hout_ref, tail_ref, ext_scr = refs
    bm = h_ref.shape[0]
    if has_state:
        ext_scr[0:H, :] = halo_ref[...]
    else:
        seq_start = pl.program_id(0) % tiles_per_seq == 0

        @pl.when(seq_start)
        def _():
            ext_scr[0:H, :] = jnp.zeros((H, SC_WIDTH), F32)

        @pl.when(jnp.logical_not(seq_start))
        def _():
            ext_scr[0:H, :] = ext_scr[bm:bm + H, :]

    sc = sc_ref[...].astype(F32)
    u = sc[:, SC_WIDTH:2 * SC_WIDTH] * sc[:, 2 * SC_WIDTH:]
    y_sc = (sc[:, :SC_WIDTH] * _causal_conv(ext_scr, u, cw_ref, H, shift)).astype(BF16)
    tail_ref[0] = ext_scr[bm:bm + H, :]

    acc = jnp.zeros((bm, D_MODEL), F32)
    for b, y in enumerate((yg_ref[...], yf_ref[...], y_sc)):
        g = jax.nn.sigmoid(gates_ref[:, b * D_MODEL:(b + 1) * D_MODEL].astype(F32))
        acc = acc + g * _dot(y, wb_ref[b])
    hout_ref[...] = h_ref[...] + _dot(acc.astype(BF16), wo_ref[...])


def _mix(h, z, yg, yf, cw, wb, wo, halo, bm, H, shift, tiles_per_seq):
    T = h.shape[0]
    has_state = halo is not None
    in_specs = [
        pl.BlockSpec((bm, D_MODEL), lambda i: (i, 0)),
        pl.BlockSpec((bm, N_BRANCH * D_MODEL), lambda i: (i, 0)),
        pl.BlockSpec((bm, 3 * SC_WIDTH), lambda i: (i, Z_SC // (3 * SC_WIDTH))),
        pl.BlockSpec((bm, SC_WIDTH), lambda i: (i, 0)),
        pl.BlockSpec((bm, SC_WIDTH), lambda i: (i, 0)),
        pl.BlockSpec((CONV_K, SC_WIDTH), lambda i: (0, 0)),
        pl.BlockSpec((N_BRANCH, SC_WIDTH, D_MODEL), lambda i: (0, 0, 0)),
        pl.BlockSpec((D_MODEL, D_MODEL), lambda i: (0, 0)),
    ]
    args = [h, z, z, yg, yf, cw, wb, wo]
    if has_state:
        in_specs.append(pl.BlockSpec((H, SC_WIDTH), lambda i: (0, 0)))
        args.append(halo)
    return pl.pallas_call(
        functools.partial(_mix_body, H=H, shift=shift, tiles_per_seq=tiles_per_seq, has_state=has_state),
        grid=(T // bm,),
        in_specs=in_specs,
        out_specs=[pl.BlockSpec((bm, D_MODEL), lambda i: (i, 0)),
                   pl.BlockSpec((1, H, SC_WIDTH), lambda i: (i, 0, 0))],
        out_shape=[jax.ShapeDtypeStruct((T, D_MODEL), F32),
                   jax.ShapeDtypeStruct((T // bm, H, SC_WIDTH), F32)],
        scratch_shapes=[pltpu.VMEM((H + bm, SC_WIDTH), F32)],
        compiler_params=_cparams(("arbitrary",), 48),
    )(*args)


FFN_CK = 1408


def _ffn_body(*refs, H, shift, tiles_per_seq, has_state, final_norm):
    refs = list(refs)
    h_ref, g_ref, wa_ref, wg_ref, cw_ref, cb_ref, wd_ref = refs[:7]
    refs = refs[7:]
    halo_ref = refs.pop(0) if has_state else None
    fg_ref = refs.pop(0) if final_norm else None
    out_ref, tail_ref, xn_scr, acc_scr, ext_scr, halo_scr = refs
    i = pl.program_id(0)
    c = pl.program_id(1)
    bm = h_ref.shape[0]

    @pl.when(c == 0)
    def _():
        xn_scr[...] = _rms(h_ref[...], g_ref[...]).astype(BF16)
        acc_scr[...] = jnp.zeros_like(acc_scr)

    if has_state:
        ext_scr[0:H, :] = halo_ref[...]
    else:
        seq_start = i % tiles_per_seq == 0

        @pl.when(seq_start)
        def _():
            ext_scr[0:H, :] = jnp.zeros((H, ext_scr.shape[1]), F32)

        @pl.when(jnp.logical_not(seq_start))
        def _():
            ext_scr[0:H, :] = halo_scr[c]

    xn = xn_scr[...]
    a = _dot(xn, wa_ref[...])
    gate = _dot(xn, wg_ref[...])
    ac = _causal_conv(ext_scr, a, cw_ref, H, shift) + cb_ref[...]
    tail = ext_scr[bm:bm + H, :]
    tail_ref[0] = tail
    if not has_state:
        halo_scr[c] = tail
    act = 0.5 * ac * (1.0 + lax.erf(ac * (2.0 ** -0.5))) * gate
    acc_scr[...] += _dot(act.astype(BF16), wd_ref[...])

    @pl.when(c == pl.num_programs(1) - 1)
    def _():
        hn = h_ref[...] + acc_scr[...]
        out_ref[...] = _rms(hn, fg_ref[...]) if final_norm else hn


def _ffn(h, g, wa, wg, cw, cb, wd, halo, fg, bm, H, shift, tiles_per_seq):
    T = h.shape[0]
    ck = FFN_CK
    nc = D_FF // ck
    has_state = halo is not None
    final_norm = fg is not None
    in_specs = [
        pl.BlockSpec((bm, D_MODEL), lambda i, c: (i, 0)),
        pl.BlockSpec((1, D_MODEL), lambda i, c: (0, 0)),
        pl.BlockSpec((D_MODEL, ck), lambda i, c: (0, c)),
        pl.BlockSpec((D_MODEL, ck), lambda i, c: (0, c)),
        pl.BlockSpec((CONV_K, ck), lambda i, c: (0, c)),
        pl.BlockSpec((1, ck), lambda i, c: (0, c)),
        pl.BlockSpec((ck, D_MODEL), lambda i, c: (c, 0)),
    ]
    args = [h, g, wa, wg, cw, cb, wd]
    if has_state:
        in_specs.append(pl.BlockSpec((H, ck), lambda i, c: (0, c)))
        args.append(halo)
    if final_norm:
        in_specs.append(pl.BlockSpec((1, D_MODEL), lambda i, c: (0, 0)))
        args.append(fg)
    return pl.pallas_call(
        functools.partial(_ffn_body, H=H, shift=shift, tiles_per_seq=tiles_per_seq, has_state=has_state,
                          final_norm=final_norm),
        grid=(T // bm, nc),
        in_specs=in_specs,
        out_specs=[pl.BlockSpec((bm, D_MODEL), lambda i, c: (i, 0)),
                   pl.BlockSpec((1, H, ck), lambda i, c: (i, 0, c))],
        out_shape=[jax.ShapeDtypeStruct((T, D_MODEL), F32),
                   jax.ShapeDtypeStruct((T // bm, H, D_FF), F32)],
        scratch_shapes=[pltpu.VMEM((bm, D_MODEL), BF16), pltpu.VMEM((bm, D_MODEL), F32),
                        pltpu.VMEM((H + bm, ck), F32), pltpu.VMEM((nc, H, ck), F32)],
        compiler_params=_cparams(("arbitrary", "arbitrary"), 56),
    )(*args)


def _pack_in_proj(w_in):
    sizes = (GLA_HEADS * GLA_DK, GLA_HEADS * GLA_DK, GLA_HEADS * GLA_DV, GLA_LR, GLA_HEADS * GLA_DV,
             FOX_HEADS * FOX_HD, FOX_HEADS * FOX_HD, FOX_HEADS * FOX_HD, FOX_HEADS,
             SC_WIDTH, SC_WIDTH, SC_WIDTH, N_BRANCH * D_MODEL)
    offs = np.concatenate([[0], np.cumsum(sizes)])
    part = lambda n: w_in[:, :, offs[n]:offs[n + 1]]
    gq, gk, gv, glr, gg, fq, fk, fv, ff, scb, scc, sch, gates = [part(n) for n in range(len(sizes))]
    w_main = jnp.concatenate([gates, gq, gk, gv, gg, fq, fk, fv, scb, scc, sch], axis=-1).astype(BF16)
    pad = jnp.zeros(w_in.shape[:2] + (SMALL_COLS - GLA_LR - FOX_HEADS,), w_in.dtype)
    w_small = jnp.concatenate([glr, ff, pad], axis=-1).astype(BF16)
    w_fft = jnp.swapaxes(ff, 1, 2).astype(BF16)
    return w_main, w_small, w_fft


def kernel(x_prompt, x_sample, cache_fox_k, cache_fox_v, cache_fox_logf, state_gla, state_sconv, state_ffn_conv,
           page_table, norm1_g, w_in, w_alpha2, b_alpha, b_forget, gla_norm_g, sconv_w, w_branch, w_out, norm2_g,
           w_up, ffn_conv_w, ffn_conv_b, w_down, final_norm_g):
    depth = w_in.shape[0]
    B, L, _ = x_prompt.shape
    nb, nt, _ = x_sample.shape
    n_pool, page = cache_fox_k.shape[1], cache_fox_k.shape[2]
    n_pages = page_table.shape[1]
    W = FOX_HEADS * FOX_HD
    Tp = B * L
    Ts = nb * nt
    bm_p = min(512, L)
    bm_in = min(1024, L)
    fox_bq, fox_bk = min(512, L), min(512, L)
    H_p, H_s = SUBLANES, (CONV_K - 1) * nb

    w_main, w_small, w_fft = _pack_in_proj(w_in)
    wa_pad = jnp.concatenate(
        [w_alpha2, jnp.zeros((depth, SMALL_COLS - GLA_LR, GLA_HEADS * GLA_DK), w_alpha2.dtype)], axis=1).astype(BF16)
    wa_heads = wa_pad.reshape(depth, SMALL_COLS, GLA_HEADS, GLA_DK).transpose(0, 2, 1, 3)
    ba_heads = b_alpha.reshape(depth, GLA_HEADS, 1, GLA_DK)
    wb = w_branch.astype(BF16)
    wo = w_out.astype(BF16)
    wu_a = w_up[:, :, :D_FF].astype(BF16)
    wu_g = w_up[:, :, D_FF:].astype(BF16)
    wd = w_down.astype(BF16)

    cache_kt = jnp.transpose(cache_fox_k, (0, 1, 3, 4, 2))
    cache_vt = jnp.transpose(cache_fox_v, (0, 1, 3, 4, 2))
    cache_lft = jnp.swapaxes(cache_fox_logf, 2, 3)
    pt_flat = page_table.reshape(-1).astype(jnp.int32)

    def to_seq(a):
        return jnp.swapaxes(a.reshape((a.shape[0] // nb, nb) + a.shape[1:]), 0, 1)

    def to_time(a):
        return jnp.swapaxes(a, 0, 1).reshape((a.shape[0] * a.shape[1],) + a.shape[2:])

    hp = x_prompt.reshape(Tp, D_MODEL)
    hs = to_time(x_sample)
    outs_p, outs_s = [], []
    for l in range(depth):
        g1 = norm1_g[l].reshape(1, D_MODEL)
        g2 = norm2_g[l].reshape(1, D_MODEL)
        bf = b_forget[l].reshape(FOX_HEADS, 1)
        ba = b_alpha[l].reshape(1, -1)
        gn = gla_norm_g[l].reshape(1, GLA_DV)
        cb = ffn_conv_b[l].reshape(1, D_FF)
        fg = final_norm_g.reshape(1, D_MODEL) if l == depth - 1 else None

        z, fkt, fvt, small, fft = _in_proj(hp, g1, w_main[l], w_small[l], w_fft[l], bm_in, (B, L))
        yg, st, logf_t, c_t = _gla_prompt(z, small, fft, wa_pad[l], ba, bf, gn, B, L)
        c_rows = c_t.reshape(FOX_HEADS // 2, 2, Tp)
        yf = _fox_prompt(z, jnp.swapaxes(c_rows, 1, 2), c_rows, B, L, fox_bq, fox_bk)
        hp, sc_tail = _mix(hp, z, yg, yf, sconv_w[l], wb[l], wo[l], None, bm_p, H_p, 1, L // bm_p)
        hp, ffn_tail = _ffn(hp, g2, wu_a[l], wu_g[l], ffn_conv_w[l], cb, wd[l], None, fg, bm_p, H_p, 1, L // bm_p)
        last = lambda t: t.reshape(B, L // bm_p, H_p, -1)[:, -1, H_p - (CONV_K - 1):, :]
        p_gla = st.reshape(B, GLA_HEADS // 2, GLA_DV, 2, GLA_DK).transpose(0, 1, 3, 4, 2)
        heads_last = lambda a: a.reshape(B, FOX_HEADS, FOX_HD, L).transpose(0, 3, 1, 2)
        outs_p.append((heads_last(fkt), heads_last(fvt), logf_t.T.reshape(B, L, FOX_HEADS), p_gla.reshape(B, GLA_HEADS, GLA_DK, GLA_DV),
                       last(sc_tail), last(ffn_tail)))

        z, fk, fv, small, fft = _in_proj(hs, g1, w_main[l], w_small[l], w_fft[l], Ts)
        qk_h = z[:, Z_GQK:Z_GV].reshape(Ts, 2, GLA_HEADS, GLA_DK).transpose(1, 2, 0, 3)
        yg, s1 = _gla_sample(qk_h[0], qk_h[1], z, small, wa_heads[l], ba_heads[l], gn,
                             state_gla[l].reshape(nb, -1), nb, nt)
        yf_s, lfn = _fox_sample(pt_flat, to_seq(z[:, Z_FQ:Z_FK].astype(F32)), to_seq(fk), to_seq(fv),
                                jnp.swapaxes(fft.reshape(FOX_HEADS, nt, nb), 0, 2).swapaxes(1, 2), bf,
                                cache_kt, cache_vt, cache_lft, l, nb, nt, n_pages, page)
        yf = to_time(yf_s).astype(BF16)
        halo_sc = to_time(state_sconv[l])
        halo_ffn = to_time(state_ffn_conv[l])
        hs, sc_tail = _mix(hs, z, yg, yf, sconv_w[l], wb[l], wo[l], halo_sc, Ts, H_s, nb, 1)
        hs, ffn_tail = _ffn(hs, g2, wu_a[l], wu_g[l], ffn_conv_w[l], cb, wd[l], halo_ffn, fg, Ts, H_s, nb, 1)
        outs_s.append((to_seq(fk).reshape(nb, nt, FOX_HEADS, FOX_HD), to_seq(fv).reshape(nb, nt, FOX_HEADS, FOX_HD),
                       jnp.swapaxes(lfn, 1, 2), s1.reshape(nb, GLA_HEADS, GLA_DK, GLA_DV),
                       to_seq(sc_tail[0]), to_seq(ffn_tail[0])))

    p_states = [jnp.stack(t) for t in zip(*outs_p)]
    s_states = [jnp.stack(t) for t in zip(*outs_s)]
    y_prompt = hp.reshape(B, L, D_MODEL)
    y_sample = to_seq(hs)
    return (y_prompt, y_sample, *p_states, *s_states)
```

```python
import functools

import numpy as np
import jax
import jax.numpy as jnp
from jax import lax
from jax.experimental import pallas as pl
from jax.experimental.pallas import tpu as pltpu

F32 = jnp.float32
BF16 = jnp.bfloat16

D_MODEL = 1024
GLA_HEADS = 4
GLA_DK = 64
GLA_DV = 128
GLA_LR = 16
GLA_TAU = 16.0
FOX_HEADS = 8
FOX_HD = 64
SC_WIDTH = 512
CONV_K = 3
D_FF = 2816
N_BRANCH = 3
EPS = 1e-6

LANES = 128
SUBLANES = 8
VMEM_BYTES = 64 << 20

Z_GATES = 0
Z_GQK = 3072
Z_GV = 3584
Z_GG = 4096
Z_FQ = 4608
Z_FK = 5120
Z_FV = 5632
Z_SC = 6144
Z_COLS = 7680
IN_BN = 1536
SMALL_COLS = 128

GLA_CHUNK = 128
NEG = -1e30


def _cparams(sem, vmem_mb):
    return pltpu.CompilerParams(dimension_semantics=sem, vmem_limit_bytes=min(vmem_mb << 20, VMEM_BYTES - (4 << 20)))


def _log_sigmoid(x):
    return jnp.minimum(x, 0.0) - jnp.log1p(jnp.exp(-jnp.abs(x)))


def _dot(a, b):
    return jnp.dot(a, b, preferred_element_type=F32)


def _dot_nt(a, b):
    return lax.dot_general(a, b, (((1,), (1,)), ((), ())), preferred_element_type=F32)


def _dot_tn(a, b):
    return lax.dot_general(a, b, (((0,), (0,)), ((), ())), preferred_element_type=F32)


def _rms(x, g):
    return x * lax.rsqrt(jnp.mean(x * x, axis=-1, keepdims=True) + EPS) * g


def _lane_prefix_sum(c, lane):
    sh = 1
    while sh < LANES:
        c = c + jnp.where(lane >= sh, pltpu.roll(c, sh, axis=1), 0.0)
        sh *= 2
    return c


def _in_proj_body(x_ref, g_ref, w_ref, ws_ref, wft_ref, z_ref, fk_ref, fv_ref, small_ref, fft_ref, xn_scr,
                  *, transposed):
    j = pl.program_id(1)

    @pl.when(j == 0)
    def _():
        xb = _rms(x_ref[...], g_ref[...]).astype(BF16)
        xn_scr[...] = xb
        small_ref[...] = _dot(xb, ws_ref[...])
        fft_ref[...] = _dot_nt(wft_ref[...], xb)

    zt = _dot(xn_scr[...], w_ref[...])
    z_ref[...] = zt.astype(BF16)

    W = FOX_HEADS * FOX_HD

    def part(off):
        blk = zt[:, off % IN_BN:off % IN_BN + W]
        return blk.T if transposed else blk

    @pl.when(j == Z_FK // IN_BN)
    def _():
        fk_ref[...] = part(Z_FK)

    @pl.when(j == Z_FV // IN_BN)
    def _():
        fv_ref[...] = part(Z_FV)


def _in_proj(x, g, w, ws, wft, bm, seqs=None):
    T = x.shape[0]
    W = FOX_HEADS * FOX_HD
    grid = (T // bm, Z_COLS // IN_BN)
    if seqs is None:
        kv_spec = pl.BlockSpec((bm, W), lambda i, j: (i, 0))
        kv_shape = jax.ShapeDtypeStruct((T, W), F32)
    else:
        B, L = seqs
        per_seq = L // bm
        kv_spec = pl.BlockSpec((None, W, bm), lambda i, j: (i // per_seq, 0, i % per_seq))
        kv_shape = jax.ShapeDtypeStruct((B, W, L), F32)
    return pl.pallas_call(
        functools.partial(_in_proj_body, transposed=seqs is not None),
        grid=grid,
        in_specs=[
            pl.BlockSpec((bm, D_MODEL), lambda i, j: (i, 0)),
            pl.BlockSpec((1, D_MODEL), lambda i, j: (0, 0)),
            pl.BlockSpec((D_MODEL, IN_BN), lambda i, j: (0, j)),
            pl.BlockSpec((D_MODEL, SMALL_COLS), lambda i, j: (0, 0)),
            pl.BlockSpec((FOX_HEADS, D_MODEL), lambda i, j: (0, 0)),
        ],
        out_specs=[
            pl.BlockSpec((bm, IN_BN), lambda i, j: (i, j)),
            kv_spec,
            kv_spec,
            pl.BlockSpec((bm, SMALL_COLS), lambda i, j: (i, 0)),
            pl.BlockSpec((FOX_HEADS, bm), lambda i, j: (0, i)),
        ],
        out_shape=[
            jax.ShapeDtypeStruct((T, Z_COLS), BF16),
            kv_shape,
            kv_shape,
            jax.ShapeDtypeStruct((T, SMALL_COLS), F32),
            jax.ShapeDtypeStruct((FOX_HEADS, T), F32),
        ],
        scratch_shapes=[pltpu.VMEM((bm, D_MODEL), BF16)],
        compiler_params=_cparams(("arbitrary", "arbitrary"), 56),
    )(x, g, w, ws, wft)


def _gla_consts(C):
    t = np.arange(C)
    r = t[None, :]
    mats = [r <= t[:, None], r > t[:, None]]
    masks = []
    n = C // 2
    while n >= 1:
        pair = t // (2 * n)
        second = (t // n) % 2 == 1
        mid = pair * 2 * n + n
        mats.append(np.where(second[:, None], (r >= mid[:, None]) & (r <= t[:, None]),
                             (r > t[:, None]) & (r < mid[:, None])))
        masks.append(second[:, None] & ~second[None, :] & (pair[:, None] == pair[None, :]))
        n //= 2
    masks.append(np.eye(C, dtype=bool))
    return (jnp.asarray(np.concatenate(mats, 0).astype(np.float32), BF16),
            jnp.asarray(np.stack(masks).astype(np.float32)))


def _gla_prompt_body(qk_ref, v_ref, gg_ref, small_ref, fft_ref, wa_ref, ba_ref, bf_ref, gn_ref, wall_ref,
                     masks_ref, y_ref, st_ref, logf_ref, c_ref, s_scr, c_scr, *, C, nlev):
    i = pl.program_id(1)

    @pl.when(i == 0)
    def _():
        s_scr[...] = jnp.zeros_like(s_scr)
        c_scr[...] = jnp.zeros_like(c_scr)

    lf = _log_sigmoid(fft_ref[...] + bf_ref[...])
    logf_ref[...] = lf
    lane = lax.broadcasted_iota(jnp.int32, lf.shape, 1)
    c = _lane_prefix_sum(lf, lane) + c_scr[...]
    c_ref[...] = c
    c_scr[...] = jnp.broadcast_to(c[:, C - 1:C], c_scr.shape)

    la = _log_sigmoid(_dot(small_ref[...].astype(BF16), wa_ref[...]) + ba_ref[...]) * (1.0 / GLA_TAU)
    hi = la.astype(BF16)
    lo = (la - hi.astype(F32)).astype(BF16)
    wall = wall_ref[...]
    P = jnp.exp(_dot(wall, hi) + _dot(wall, lo))

    W = GLA_HEADS * GLA_DK
    qk = qk_ref[...].astype(F32)
    q = qk[:, :W] * (GLA_DK ** -0.5)
    k = qk[:, W:]
    qd = (q * P[0:C]).astype(BF16)
    kd = (k * P[C:2 * C]).astype(BF16)
    a_last = P[C - 1:C, :]
    row = lax.broadcasted_iota(jnp.int32, (C, 1), 0)
    X = []
    n = C // 2
    for l in range(nlev):
        X.append((jnp.where((row & n) != 0, q, k) * P[(2 + l) * C:(3 + l) * C]).astype(BF16))
        n //= 2
    qb = q.astype(BF16)
    kb = k.astype(BF16)

    lane_w = lax.broadcasted_iota(jnp.int32, (1, W), 1) >> (GLA_DK.bit_length() - 1)

    def by_head(a):
        return jnp.concatenate([jnp.where(lane_w == h, a, jnp.zeros_like(a)) for h in range(GLA_HEADS)], axis=0)

    scores = masks_ref[nlev] * _dot_nt(by_head(qb), kb).reshape(GLA_HEADS, C, C)
    for l in range(nlev):
        scores = scores + masks_ref[l] * _dot_nt(by_head(X[l]), X[l]).reshape(GLA_HEADS, C, C)

    lane_p = lax.broadcasted_iota(jnp.int32, (1, LANES), 1)
    for p in range(GLA_HEADS // 2):
        sl = slice(LANES * p, LANES * (p + 1))
        ST = s_scr[p]
        STb = ST.astype(BF16)
        U = []
        for hh in range(2):
            h = 2 * p + hh
            hm = (lane_p >= GLA_DK) == bool(hh)

            def msk(a):
                return jnp.where(hm, a, jnp.zeros_like(a))

            vh = v_ref[:, GLA_DV * h:GLA_DV * (h + 1)]
            o = _dot(scores[h].astype(BF16), vh) + _dot_nt(msk(qd[:, sl]), STb)
            U.append(_dot_tn(vh, kd[:, sl]))
            g = gg_ref[:, GLA_DV * h:GLA_DV * (h + 1)].astype(F32)
            y_ref[:, GLA_DV * h:GLA_DV * (h + 1)] = (_rms(o, gn_ref[...]) * (g * jax.nn.sigmoid(g))).astype(BF16)
        s_scr[p] = ST * a_last[:, sl] + jnp.where(lane_p < GLA_DK, U[0], U[1])

    @pl.when(i == pl.num_programs(1) - 1)
    def _():
        st_ref[0] = s_scr[...]


def _gla_prompt(z, small, fft, wa, ba, bf, gn, B, L):
    C = GLA_CHUNK
    nL = L // C
    wall, masks = _gla_consts(C)
    nlev = masks.shape[0] - 1
    T = B * L
    W = GLA_HEADS * GLA_DK
    row = lambda b, i: b * nL + i
    return pl.pallas_call(
        functools.partial(_gla_prompt_body, C=C, nlev=nlev),
        grid=(B, nL),
        in_specs=[
            pl.BlockSpec((C, 2 * W), lambda b, i: (row(b, i), Z_GQK // (2 * W))),
            pl.BlockSpec((C, GLA_HEADS * GLA_DV), lambda b, i: (row(b, i), Z_GV // (GLA_HEADS * GLA_DV))),
            pl.BlockSpec((C, GLA_HEADS * GLA_DV), lambda b, i: (row(b, i), Z_GG // (GLA_HEADS * GLA_DV))),
            pl.BlockSpec((C, SMALL_COLS), lambda b, i: (row(b, i), 0)),
            pl.BlockSpec((FOX_HEADS, C), lambda b, i: (0, row(b, i))),
            pl.BlockSpec((SMALL_COLS, W), lambda b, i: (0, 0)),
            pl.BlockSpec((1, W), lambda b, i: (0, 0)),
            pl.BlockSpec((FOX_HEADS, 1), lambda b, i: (0, 0)),
            pl.BlockSpec((1, GLA_DV), lambda b, i: (0, 0)),
            pl.BlockSpec(wall.shape, lambda b, i: (0, 0)),
            pl.BlockSpec(masks.shape, lambda b, i: (0, 0, 0)),
        ],
        out_specs=[
            pl.BlockSpec((C, GLA_HEADS * GLA_DV), lambda b, i: (row(b, i), 0)),
            pl.BlockSpec((1, GLA_HEADS // 2, GLA_DV, 2 * GLA_DK), lambda b, i: (b, 0, 0, 0)),
            pl.BlockSpec((FOX_HEADS, C), lambda b, i: (0, row(b, i))),
            pl.BlockSpec((FOX_HEADS, C), lambda b, i: (0, row(b, i))),
        ],
        out_shape=[
            jax.ShapeDtypeStruct((T, GLA_HEADS * GLA_DV), BF16),
            jax.ShapeDtypeStruct((B, GLA_HEADS // 2, GLA_DV, 2 * GLA_DK), F32),
            jax.ShapeDtypeStruct((FOX_HEADS, T), F32),
            jax.ShapeDtypeStruct((FOX_HEADS, T), F32),
        ],
        scratch_shapes=[pltpu.VMEM((GLA_HEADS // 2, GLA_DV, 2 * GLA_DK), F32),
                        pltpu.VMEM((FOX_HEADS, LANES), F32)],
        compiler_params=_cparams(("arbitrary", "arbitrary"), 32),
    )(z, z, z, small, fft, wa, ba, bf, gn, wall, masks)


def _gla_sample_body(q_ref, k_ref, v_ref, gg_ref, small_ref, wa_ref, ba_ref, gn_ref, s0_ref, y_ref, s1_ref,
                     *, nb, nt):
    la = _log_sigmoid(_dot(small_ref[...].astype(BF16), wa_ref[0]) + ba_ref[0]) * (1.0 / GLA_TAU)
    q = q_ref[0].astype(F32) * (GLA_DK ** -0.5)
    k = k_ref[0].astype(F32)
    rows = [slice(t * nb, (t + 1) * nb) for t in range(nt)]
    b = []
    for t in range(nt):
        b.append(la[rows[t]] if t == 0 else b[-1] + la[rows[t]])
    qt = [q[r] for r in rows]
    kt = [k[r] for r in rows]
    qd = [qt[t] * jnp.exp(b[t]) for t in range(nt)]
    kd = [kt[t] * jnp.exp(b[nt - 1] - b[t]) for t in range(nt)]
    a_last = jnp.exp(b[nt - 1])

    for d in range(GLA_DK):
        cs = slice(d * GLA_DV, (d + 1) * GLA_DV)
        acc = a_last[:, d:d + 1] * s0_ref[:, cs]
        for t in range(nt):
            acc = acc + kd[t][:, d:d + 1] * v_ref[rows[t], :].astype(F32)
        s1_ref[:, cs] = acc

    for t in range(nt):
        o = jnp.zeros((nb, GLA_DV), F32)
        for s in range(t + 1):
            w = jnp.sum(qt[t] * kt[s] * jnp.exp(b[t] - b[s]), axis=-1, keepdims=True)
            o = o + w * v_ref[rows[s], :].astype(F32)
        for d in range(GLA_DK):
            o = o + qd[t][:, d:d + 1] * s0_ref[:, d * GLA_DV:(d + 1) * GLA_DV]
        g = gg_ref[rows[t], :].astype(F32)
        y_ref[rows[t], :] = (_rms(o, gn_ref[...]) * (g * jax.nn.sigmoid(g))).astype(BF16)


def _gla_sample(qh, kh, z, small, wa_h, ba_h, gn, s0, nb, nt):
    T = nb * nt
    slab = GLA_DK * GLA_DV
    return pl.pallas_call(
        functools.partial(_gla_sample_body, nb=nb, nt=nt),
        grid=(GLA_HEADS,),
        in_specs=[
            pl.BlockSpec((1, T, GLA_DK), lambda h: (h, 0, 0)),
            pl.BlockSpec((1, T, GLA_DK), lambda h: (h, 0, 0)),
            pl.BlockSpec((T, GLA_DV), lambda h: (0, Z_GV // GLA_DV + h)),
            pl.BlockSpec((T, GLA_DV), lambda h: (0, Z_GG // GLA_DV + h)),
            pl.BlockSpec((T, SMALL_COLS), lambda h: (0, 0)),
            pl.BlockSpec((1, SMALL_COLS, GLA_DK), lambda h: (h, 0, 0)),
            pl.BlockSpec((1, 1, GLA_DK), lambda h: (h, 0, 0)),
            pl.BlockSpec((1, GLA_DV), lambda h: (0, 0)),
            pl.BlockSpec((nb, slab), lambda h: (0, h)),
        ],
        out_specs=[
            pl.BlockSpec((T, GLA_DV), lambda h: (0, h)),
            pl.BlockSpec((nb, slab), lambda h: (0, h)),
        ],
        out_shape=[
            jax.ShapeDtypeStruct((T, GLA_HEADS * GLA_DV), BF16),
            jax.ShapeDtypeStruct((nb, GLA_HEADS * slab), F32),
        ],
        compiler_params=_cparams(("arbitrary",), 40),
    )(qh, kh, z, z, small, wa_h, ba_h, gn, s0)


def _split3(c):
    def top(x):
        bits = lax.bitcast_convert_type(x, jnp.uint32) & jnp.uint32(0xFFFF0000)
        return lax.bitcast_convert_type(bits, F32)

    hi = top(c)
    mid = top(c - hi)
    return hi, mid, c - hi - mid


FOX_PREP_ROWS = 512
FOX_SUM_ROWS = 16
LOG2E = 1.4426950408889634


def _fox_prompt_body(q_ref, k_ref, v_ref, ck_ref, cq_ref, o_ref, k_scr, vt_scr, s_scr, *, bq, bk):
    i = pl.program_id(2)
    row0 = i * bq
    n_full = lax.div(row0, bk)
    lane = lax.broadcasted_iota(jnp.int32, (1, LANES), 1)
    NB = 3

    @pl.when(i == 0)
    def _():
        ch = min(FOX_PREP_ROWS, k_ref.shape[0])

        def prep(r, carry):
            start = pl.multiple_of(r * ch, ch)
            kc = k_ref[pl.ds(start, ch), :].astype(F32)
            cc = ck_ref[0, pl.ds(start, ch), :]
            vt = v_ref[pl.ds(start, ch), :].astype(F32).T
            ones = jnp.ones((FOX_SUM_ROWS, ch), F32)
            for hh in range(2):
                kh = kc if hh == 0 else pltpu.roll(kc, FOX_HD, axis=1)
                tile = jnp.where(lane < FOX_HD + NB, 1.0, 0.0)
                for n, term in enumerate(_split3(cc[:, hh:hh + 1] * LOG2E)):
                    tile = jnp.where(lane == FOX_HD + NB + n, -term, tile)
                k_scr[hh, pl.ds(start, ch), :] = jnp.where(lane < FOX_HD, kh, tile).astype(BF16)
                vt_scr[hh, :, pl.ds(start, ch)] = jnp.concatenate(
                    [vt[hh * FOX_HD:(hh + 1) * FOX_HD], ones], axis=0).astype(BF16)
            return carry

        lax.fori_loop(0, k_ref.shape[0] // ch, prep, 0)

    qt = (q_ref[...].astype(F32) * (FOX_HD ** -0.5 * LOG2E)).T
    row = lax.broadcasted_iota(jnp.int32, (LANES, 1), 0)
    qts = []
    for hh in range(2):
        tile = jnp.concatenate([qt[hh * FOX_HD:(hh + 1) * FOX_HD], jnp.zeros((LANES - FOX_HD, bq), F32)], axis=0)
        tile = jnp.where((row >= FOX_HD + NB) & (row < FOX_HD + 2 * NB), 1.0, tile)
        for n, term in enumerate(_split3(cq_ref[0, hh:hh + 1, :] * LOG2E)):
            tile = jnp.where(row == FOX_HD + n, term, tile)
        qts.append(tile.astype(BF16))

    def scores(j):
        start = pl.multiple_of(j * bk, bk)
        return tuple(_dot(k_scr[hh, pl.ds(start, bk), :], qts[hh]) for hh in range(2))

    def softmax(j, s, m, masked):
        out = []
        for hh in range(2):
            sh = s[hh]
            if masked:
                key = j * bk + lax.broadcasted_iota(jnp.int32, (bk, bq), 0)
                qry = row0 + lax.broadcasted_iota(jnp.int32, (bk, bq), 1)
                sh = jnp.where(key <= qry, sh, NEG)
            m_new = jnp.maximum(m[hh], jnp.max(sh, axis=0, keepdims=True))
            out.append((m_new, jnp.exp2(sh - m_new).astype(BF16), jnp.exp2(m[hh] - m_new)))
        return tuple(zip(*out))

    def accumulate(j, p, alpha, acc):
        start = pl.multiple_of(j * bk, bk)
        return tuple(alpha[hh] * acc[hh] + _dot(vt_scr[hh, :, pl.ds(start, bk)], p[hh]) for hh in range(2))

    def put_scores(j):
        for hh, sh in enumerate(scores(j)):
            s_scr[hh] = sh

    def step(j, carry):
        m, acc = carry
        m, p, alpha = softmax(j, (s_scr[0], s_scr[1]), m, False)
        put_scores(j + 1)
        return m, accumulate(j, p, alpha, acc)

    two = lambda a: (a, a)
    put_scores(0)
    acc0 = jnp.zeros((FOX_HD + FOX_SUM_ROWS, bq), F32)
    m, acc = lax.fori_loop(0, n_full, step, (two(jnp.full((1, bq), NEG, F32)), two(acc0)))
    m, p, alpha = softmax(n_full, (s_scr[0], s_scr[1]), m, True)
    acc = accumulate(n_full, p, alpha, acc)
    o = jnp.concatenate([a[:FOX_HD] / a[FOX_HD:FOX_HD + 1] for a in acc], axis=0)
    o_ref[...] = o.T.astype(BF16)


def _fox_prompt(z, c_cols, c_rows, B, L, bq, bk):
    nq = L // bq
    T = B * L
    npair = FOX_HEADS // 2
    return pl.pallas_call(
        functools.partial(_fox_prompt_body, bq=bq, bk=bk),
        grid=(B, npair, nq),
        in_specs=[
            pl.BlockSpec((bq, LANES), lambda b, p, i: (b * nq + i, Z_FQ // LANES + p)),
            pl.BlockSpec((L, LANES), lambda b, p, i: (b, Z_FK // LANES + p)),
            pl.BlockSpec((L, LANES), lambda b, p, i: (b, Z_FV // LANES + p)),
            pl.BlockSpec((1, L, 2), lambda b, p, i: (p, b, 0)),
            pl.BlockSpec((1, 2, bq), lambda b, p, i: (p, 0, b * nq + i)),
        ],
        out_specs=pl.BlockSpec((bq, LANES), lambda b, p, i: (b * nq + i, p)),
        out_shape=jax.ShapeDtypeStruct((T, FOX_HEADS * FOX_HD), BF16),
        scratch_shapes=[pltpu.VMEM((2, L, LANES), BF16), pltpu.VMEM((2, FOX_HD + FOX_SUM_ROWS, L), BF16),
                        pltpu.VMEM((2, bk, bq), F32)],
        compiler_params=_cparams(("arbitrary", "arbitrary", "arbitrary"), 56),
    )(z, z, z, c_cols, c_rows)


def _fox_sample_body(pt_ref, q_ref, kn_ref, vn_ref, fft_ref, bf_ref, kt_hbm, vt_hbm, lft_hbm, o_ref, lfo_ref,
                     kbuf, vbuf, lfbuf, k_scr, v_scr, sem, *, layer, n_pages, nt, page, nseq):
    s = pl.program_id(0)
    slot = lax.rem(s, 2)

    def page_copies(step, slot_):
        out = []
        for n in range(nseq * n_pages):
            pid = pt_ref[step * (nseq * n_pages) + n]
            out.append(pltpu.make_async_copy(kt_hbm.at[layer, pid], kbuf.at[slot_, n], sem.at[slot_, 0]))
            out.append(pltpu.make_async_copy(vt_hbm.at[layer, pid], vbuf.at[slot_, n], sem.at[slot_, 1]))
            out.append(pltpu.make_async_copy(lft_hbm.at[layer, pid], lfbuf.at[slot_, n], sem.at[slot_, 2]))
        return out

    @pl.when(s == 0)
    def _():
        for c in page_copies(s, slot):
            c.start()

    @pl.when(s + 1 < pl.num_programs(0))
    def _():
        for c in page_copies(s + 1, 1 - slot):
            c.start()

    for n, (hbm, buf) in enumerate(((kt_hbm, kbuf), (vt_hbm, vbuf), (lft_hbm, lfbuf))):
        pltpu.make_async_copy(hbm.at[layer, pl.ds(0, nseq * n_pages)], buf.at[slot], sem.at[slot, n]).wait()

    for u in range(nseq):
        pages = [u * n_pages + j for j in range(n_pages)]
        _fox_sample_one(q_ref.at[u], kn_ref.at[u], vn_ref.at[u], fft_ref.at[u], bf_ref,
                        [kbuf.at[slot, n] for n in pages], [vbuf.at[slot, n] for n in pages],
                        [lfbuf.at[slot, n] for n in pages], o_ref.at[u], lfo_ref.at[u], k_scr.at[u], v_scr.at[u],
                        n_pages=n_pages, nt=nt, page=page)


def _fox_sample_one(q_ref, kn_ref, vn_ref, fft_ref, bf_ref, k_refs, v_refs, lf_refs, o_ref, lfo_ref, k_scr, v_scr,
                    *, n_pages, nt, page):
    W = FOX_HEADS * FOX_HD
    H = FOX_HEADS
    q = q_ref[...] * (FOX_HD ** -0.5)
    head_shift = FOX_HD.bit_length() - 1
    hmask = (lax.broadcasted_iota(jnp.int32, (H, W), 1) >> head_shift) == lax.broadcasted_iota(jnp.int32, (H, W), 0)
    qbd = jnp.concatenate([jnp.where(hmask, jnp.broadcast_to(q[t:t + 1], (H, W)), 0.0) for t in range(nt)], axis=0)
    qbd_b = qbd.astype(BF16)

    lane = lax.broadcasted_iota(jnp.int32, (H, LANES), 1)
    carry = jnp.zeros((H, 1), F32)
    ck = []
    for j in range(n_pages):
        c = _lane_prefix_sum(lf_refs[j][...], lane) + carry
        carry = c[:, LANES - 1:LANES]
        ck.append(c)
    lfn = _log_sigmoid(fft_ref[...] + bf_ref[...])
    lfo_ref[...] = lfn
    lane_t = lax.broadcasted_iota(jnp.int32, (H, nt), 1)
    cn = carry
    for j in range(nt):
        cn = cn + jnp.where(lane_t >= j, lfn[:, j:j + 1], 0.0)
    cq = jnp.concatenate([cn[:, t:t + 1] for t in range(nt)], axis=0)

    for j in range(n_pages):
        k_scr[:, j * page:(j + 1) * page] = k_refs[j][...].reshape(W, page).astype(BF16)
        v_scr[:, j * page:(j + 1) * page] = v_refs[j][...].reshape(W, page).astype(BF16)
    ck_all = jnp.concatenate(ck, axis=1)
    s_past = _dot(qbd_b, k_scr[...]) + (cq - jnp.concatenate([ck_all] * nt, axis=0))

    kn = kn_ref[...]
    vn = vn_ref[...]
    trow = lax.broadcasted_iota(jnp.int32, (nt * H, 1), 0) >> (H.bit_length() - 1)
    s_new = []
    for tp in range(nt):
        ckn = jnp.concatenate([cn[:, tp:tp + 1]] * nt, axis=0)
        sn = jnp.sum(qbd * kn[tp:tp + 1], axis=-1, keepdims=True) + (cq - ckn)
        s_new.append(jnp.where(trow >= tp, sn, NEG))
    m = jnp.max(s_past, axis=-1, keepdims=True)
    for sn in s_new:
        m = jnp.maximum(m, sn)
    p_past = jnp.exp(s_past - m)
    den = jnp.sum(p_past, axis=-1, keepdims=True)
    out = _dot_nt(p_past.astype(BF16), v_scr[...])
    for tp in range(nt):
        pn = jnp.exp(s_new[tp] - m)
        den = den + pn
        out = out + pn * vn[tp:tp + 1]
    out = out / den
    for t in range(nt):
        o_ref[t:t + 1, :] = jnp.sum(jnp.where(hmask, out[t * H:(t + 1) * H], 0.0), axis=0, keepdims=True)


FOX_SEQS_PER_STEP = 2


def _fox_sample(pt_flat, q_s, kn_s, vn_s, fft_s, bf, cache_kt, cache_vt, cache_lft, layer, nb, nt, n_pages, page):
    W = FOX_HEADS * FOX_HD
    nseq = FOX_SEQS_PER_STEP if nb % FOX_SEQS_PER_STEP == 0 else 1
    assert page == LANES and cache_kt.shape[1] >= nseq * n_pages

    seq3 = lambda s, pt: (s, 0, 0)
    in_specs = [
        pl.BlockSpec((nseq, nt, W), seq3),
        pl.BlockSpec((nseq, nt, W), seq3),
        pl.BlockSpec((nseq, nt, W), seq3),
        pl.BlockSpec((nseq, FOX_HEADS, nt), seq3),
        pl.BlockSpec((FOX_HEADS, 1), lambda s, pt: (0, 0)),
        pl.BlockSpec(memory_space=pl.ANY),
        pl.BlockSpec(memory_space=pl.ANY),
        pl.BlockSpec(memory_space=pl.ANY),
    ]
    grid_spec = pltpu.PrefetchScalarGridSpec(
        num_scalar_prefetch=1,
        grid=(nb // nseq,),
        in_specs=in_specs,
        out_specs=[pl.BlockSpec((nseq, nt, W), seq3), pl.BlockSpec((nseq, FOX_HEADS, nt), seq3)],
        scratch_shapes=[pltpu.VMEM((2, nseq * n_pages, FOX_HEADS, FOX_HD, page), F32),
                        pltpu.VMEM((2, nseq * n_pages, FOX_HEADS, FOX_HD, page), F32),
                        pltpu.VMEM((2, nseq * n_pages, FOX_HEADS, page), F32),
                        pltpu.VMEM((nseq, W, n_pages * page), BF16), pltpu.VMEM((nseq, W, n_pages * page), BF16),
                        pltpu.SemaphoreType.DMA((2, 3))],
    )
    return pl.pallas_call(
        functools.partial(_fox_sample_body, layer=layer, n_pages=n_pages, nt=nt, page=page, nseq=nseq),
        grid_spec=grid_spec,
        out_shape=[jax.ShapeDtypeStruct((nb, nt, W), F32), jax.ShapeDtypeStruct((nb, FOX_HEADS, nt), F32)],
        compiler_params=_cparams(("arbitrary",), 56),
    )(pt_flat, q_s, kn_s, vn_s, fft_s, bf, cache_kt, cache_vt, cache_lft)


def _causal_conv(ext_scr, u, cw_ref, H, shift):
    bm = u.shape[0]
    ext_scr[H:H + bm, :] = u
    out = cw_ref[CONV_K - 1:CONV_K, :] * u
    for i in range(1, CONV_K):
        lo = H - i * shift
        out = out + cw_ref[CONV_K - 1 - i:CONV_K - i, :] * ext_scr[lo:lo + bm, :]
    return out


def _mix_body(*refs, H, shift, tiles_per_seq, has_state):
    if has_state:
        h_ref, gates_ref, sc_ref, yg_ref, yf_ref, cw_ref, wb_ref, wo_ref, halo_ref, hout_ref, tail_ref, ext_scr = refs
    else:
        h_ref, gates_ref, sc_ref, yg_ref, yf_ref, cw_ref, wb_ref, wo_ref, hout_ref, tail_ref, ext_scr = refs
    bm = h_ref.shape[0]
    if has_state:
        ext_scr[0:H, :] = halo_ref[...]
    else:
        seq_start = pl.program_id(0) % tiles_per_seq == 0

        @pl.when(seq_start)
        def _():
            ext_scr[0:H, :] = jnp.zeros((H, SC_WIDTH), F32)

        @pl.when(jnp.logical_not(seq_start))
        def _():
            ext_scr[0:H, :] = ext_scr[bm:bm + H, :]

    sc = sc_ref[...].astype(F32)
    u = sc[:, SC_WIDTH:2 * SC_WIDTH] * sc[:, 2 * SC_WIDTH:]
    y_sc = (sc[:, :SC_WIDTH] * _causal_conv(ext_scr, u, cw_ref, H, shift)).astype(BF16)
    tail_ref[0] = ext_scr[bm:bm + H, :]

    acc = jnp.zeros((bm, D_MODEL), F32)
    for b, y in enumerate((yg_ref[...], yf_ref[...], y_sc)):
        g = jax.nn.sigmoid(gates_ref[:, b * D_MODEL:(b + 1) * D_MODEL].astype(F32))
        acc = acc + g * _dot(y, wb_ref[b])
    hout_ref[...] = h_ref[...] + _dot(acc.astype(BF16), wo_ref[...])


def _mix(h, z, yg, yf, cw, wb, wo, halo, bm, H, shift, tiles_per_seq):
    T = h.shape[0]
    has_state = halo is not None
    in_specs = [
        pl.BlockSpec((bm, D_MODEL), lambda i: (i, 0)),
        pl.BlockSpec((bm, N_BRANCH * D_MODEL), lambda i: (i, 0)),
        pl.BlockSpec((bm, 3 * SC_WIDTH), lambda i: (i, Z_SC // (3 * SC_WIDTH))),
        pl.BlockSpec((bm, SC_WIDTH), lambda i: (i, 0)),
        pl.BlockSpec((bm, SC_WIDTH), lambda i: (i, 0)),
        pl.BlockSpec((CONV_K, SC_WIDTH), lambda i: (0, 0)),
        pl.BlockSpec((N_BRANCH, SC_WIDTH, D_MODEL), lambda i: (0, 0, 0)),
        pl.BlockSpec((D_MODEL, D_MODEL), lambda i: (0, 0)),
    ]
    args = [h, z, z, yg, yf, cw, wb, wo]
    if has_state:
        in_specs.append(pl.BlockSpec((H, SC_WIDTH), lambda i: (0, 0)))
        args.append(halo)
    return pl.pallas_call(
        functools.partial(_mix_body, H=H, shift=shift, tiles_per_seq=tiles_per_seq, has_state=has_state),
        grid=(T // bm,),
        in_specs=in_specs,
        out_specs=[pl.BlockSpec((bm, D_MODEL), lambda i: (i, 0)),
                   pl.BlockSpec((1, H, SC_WIDTH), lambda i: (i, 0, 0))],
        out_shape=[jax.ShapeDtypeStruct((T, D_MODEL), F32),
                   jax.ShapeDtypeStruct((T // bm, H, SC_WIDTH), F32)],
        scratch_shapes=[pltpu.VMEM((H + bm, SC_WIDTH), F32)],
        compiler_params=_cparams(("arbitrary",), 48),
    )(*args)


FFN_CK = 1408


def _ffn_body(*refs, H, shift, tiles_per_seq, has_state, final_norm):
    refs = list(refs)
    h_ref, g_ref, wa_ref, wg_ref, cw_ref, cb_ref, wd_ref = refs[:7]
    refs = refs[7:]
    halo_ref = refs.pop(0) if has_state else None
    fg_ref = refs.pop(0) if final_norm else None
    out_ref, tail_ref, xn_scr, acc_scr, ext_scr, halo_scr = refs
    i = pl.program_id(0)
    c = pl.program_id(1)
    bm = h_ref.shape[0]

    @pl.when(c == 0)
    def _():
        xn_scr[...] = _rms(h_ref[...], g_ref[...]).astype(BF16)
        acc_scr[...] = jnp.zeros_like(acc_scr)

    if has_state:
        ext_scr[0:H, :] = halo_ref[...]
    else:
        seq_start = i % tiles_per_seq == 0

        @pl.when(seq_start)
        def _():
            ext_scr[0:H, :] = jnp.zeros((H, ext_scr.shape[1]), F32)

        @pl.when(jnp.logical_not(seq_start))
        def _():
            ext_scr[0:H, :] = halo_scr[c]

    xn = xn_scr[...]
    a = _dot(xn, wa_ref[...])
    gate = _dot(xn, wg_ref[...])
    ac = _causal_conv(ext_scr, a, cw_ref, H, shift) + cb_ref[...]
    tail = ext_scr[bm:bm + H, :]
    tail_ref[0] = tail
    if not has_state:
        halo_scr[c] = tail
    act = 0.5 * ac * (1.0 + lax.erf(ac * (2.0 ** -0.5))) * gate
    acc_scr[...] += _dot(act.astype(BF16), wd_ref[...])

    @pl.when(c == pl.num_programs(1) - 1)
    def _():
        hn = h_ref[...] + acc_scr[...]
        out_ref[...] = _rms(hn, fg_ref[...]) if final_norm else hn


def _ffn(h, g, wa, wg, cw, cb, wd, halo, fg, bm, H, shift, tiles_per_seq):
    T = h.shape[0]
    ck = FFN_CK
    nc = D_FF // ck
    has_state = halo is not None
    final_norm = fg is not None
    in_specs = [
        pl.BlockSpec((bm, D_MODEL), lambda i, c: (i, 0)),
        pl.BlockSpec((1, D_MODEL), lambda i, c: (0, 0)),
        pl.BlockSpec((D_MODEL, ck), lambda i, c: (0, c)),
        pl.BlockSpec((D_MODEL, ck), lambda i, c: (0, c)),
        pl.BlockSpec((CONV_K, ck), lambda i, c: (0, c)),
        pl.BlockSpec((1, ck), lambda i, c: (0, c)),
        pl.BlockSpec((ck, D_MODEL), lambda i, c: (c, 0)),
    ]
    args = [h, g, wa, wg, cw, cb, wd]
    if has_state:
        in_specs.append(pl.BlockSpec((H, ck), lambda i, c: (0, c)))
        args.append(halo)
    if final_norm:
        in_specs.append(pl.BlockSpec((1, D_MODEL), lambda i, c: (0, 0)))
        args.append(fg)
    return pl.pallas_call(
        functools.partial(_ffn_body, H=H, shift=shift, tiles_per_seq=tiles_per_seq, has_state=has_state,
                          final_norm=final_norm),
        grid=(T // bm, nc),
        in_specs=in_specs,
        out_specs=[pl.BlockSpec((bm, D_MODEL), lambda i, c: (i, 0)),
                   pl.BlockSpec((1, H, ck), lambda i, c: (i, 0, c))],
        out_shape=[jax.ShapeDtypeStruct((T, D_MODEL), F32),
                   jax.ShapeDtypeStruct((T // bm, H, D_FF), F32)],
        scratch_shapes=[pltpu.VMEM((bm, D_MODEL), BF16), pltpu.VMEM((bm, D_MODEL), F32),
                        pltpu.VMEM((H + bm, ck), F32), pltpu.VMEM((nc, H, ck), F32)],
        compiler_params=_cparams(("arbitrary", "arbitrary"), 56),
    )(*args)


def _pack_in_proj(w_in):
    sizes = (GLA_HEADS * GLA_DK, GLA_HEADS * GLA_DK, GLA_HEADS * GLA_DV, GLA_LR, GLA_HEADS * GLA_DV,
             FOX_HEADS * FOX_HD, FOX_HEADS * FOX_HD, FOX_HEADS * FOX_HD, FOX_HEADS,
             SC_WIDTH, SC_WIDTH, SC_WIDTH, N_BRANCH * D_MODEL)
    offs = np.concatenate([[0], np.cumsum(sizes)])
    part = lambda n: w_in[:, :, offs[n]:offs[n + 1]]
    gq, gk, gv, glr, gg, fq, fk, fv, ff, scb, scc, sch, gates = [part(n) for n in range(len(sizes))]
    w_main = jnp.concatenate([gates, gq, gk, gv, gg, fq, fk, fv, scb, scc, sch], axis=-1).astype(BF16)
    pad = jnp.zeros(w_in.shape[:2] + (SMALL_COLS - GLA_LR - FOX_HEADS,), w_in.dtype)
    w_small = jnp.concatenate([glr, ff, pad], axis=-1).astype(BF16)
    w_fft = jnp.swapaxes(ff, 1, 2).astype(BF16)
    return w_main, w_small, w_fft


def kernel(x_prompt, x_sample, cache_fox_k, cache_fox_v, cache_fox_logf, state_gla, state_sconv, state_ffn_conv,
           page_table, norm1_g, w_in, w_alpha2, b_alpha, b_forget, gla_norm_g, sconv_w, w_branch, w_out, norm2_g,
           w_up, ffn_conv_w, ffn_conv_b, w_down, final_norm_g):
    depth = w_in.shape[0]
    B, L, _ = x_prompt.shape
    nb, nt, _ = x_sample.shape
    n_pool, page = cache_fox_k.shape[1], cache_fox_k.shape[2]
    n_pages = page_table.shape[1]
    W = FOX_HEADS * FOX_HD
    Tp = B * L
    Ts = nb * nt
    bm_p = min(512, L)
    bm_in = min(1024, L)
    fox_bq, fox_bk = min(512, L), min(512, L)
    H_p, H_s = SUBLANES, (CONV_K - 1) * nb

    w_main, w_small, w_fft = _pack_in_proj(w_in)
    wa_pad = jnp.concatenate(
        [w_alpha2, jnp.zeros((depth, SMALL_COLS - GLA_LR, GLA_HEADS * GLA_DK), w_alpha2.dtype)], axis=1).astype(BF16)
    wa_heads = wa_pad.reshape(depth, SMALL_COLS, GLA_HEADS, GLA_DK).transpose(0, 2, 1, 3)
    ba_heads = b_alpha.reshape(depth, GLA_HEADS, 1, GLA_DK)
    wb = w_branch.astype(BF16)
    wo = w_out.astype(BF16)
    wu_a = w_up[:, :, :D_FF].astype(BF16)
    wu_g = w_up[:, :, D_FF:].astype(BF16)
    wd = w_down.astype(BF16)

    cache_kt = jnp.transpose(cache_fox_k, (0, 1, 3, 4, 2))
    cache_vt = jnp.transpose(cache_fox_v, (0, 1, 3, 4, 2))
    cache_lft = jnp.swapaxes(cache_fox_logf, 2, 3)
    pt_flat = page_table.reshape(-1).astype(jnp.int32)

    def to_seq(a):
        return jnp.swapaxes(a.reshape((a.shape[0] // nb, nb) + a.shape[1:]), 0, 1)

    def to_time(a):
        return jnp.swapaxes(a, 0, 1).reshape((a.shape[0] * a.shape[1],) + a.shape[2:])

    hp = x_prompt.reshape(Tp, D_MODEL)
    hs = to_time(x_sample)
    outs_p, outs_s = [], []
    for l in range(depth):
        g1 = norm1_g[l].reshape(1, D_MODEL)
        g2 = norm2_g[l].reshape(1, D_MODEL)
        bf = b_forget[l].reshape(FOX_HEADS, 1)
        ba = b_alpha[l].reshape(1, -1)
        gn = gla_norm_g[l].reshape(1, GLA_DV)
        cb = ffn_conv_b[l].reshape(1, D_FF)
        fg = final_norm_g.reshape(1, D_MODEL) if l == depth - 1 else None

        z, fkt, fvt, small, fft = _in_proj(hp, g1, w_main[l], w_small[l], w_fft[l], bm_in, (B, L))
        yg, st, logf_t, c_t = _gla_prompt(z, small, fft, wa_pad[l], ba, bf, gn, B, L)
        c_rows = c_t.reshape(FOX_HEADS // 2, 2, Tp)
        yf = _fox_prompt(z, jnp.swapaxes(c_rows, 1, 2), c_rows, B, L, fox_bq, fox_bk)
        hp, sc_tail = _mix(hp, z, yg, yf, sconv_w[l], wb[l], wo[l], None, bm_p, H_p, 1, L // bm_p)
        hp, ffn_tail = _ffn(hp, g2, wu_a[l], wu_g[l], ffn_conv_w[l], cb, wd[l], None, fg, bm_p, H_p, 1, L // bm_p)
        last = lambda t: t.reshape(B, L // bm_p, H_p, -1)[:, -1, H_p - (CONV_K - 1):, :]
        p_gla = st.reshape(B, GLA_HEADS // 2, GLA_DV, 2, GLA_DK).transpose(0, 1, 3, 4, 2)
        heads_last = lambda a: a.reshape(B, FOX_HEADS, FOX_HD, L).transpose(0, 3, 1, 2)
        outs_p.append((heads_last(fkt), heads_last(fvt), logf_t.T.reshape(B, L, FOX_HEADS), p_gla.reshape(B, GLA_HEADS, GLA_DK, GLA_DV),
                       last(sc_tail), last(ffn_tail)))

        z, fk, fv, small, fft = _in_proj(hs, g1, w_main[l], w_small[l], w_fft[l], Ts)
        qk_h = z[:, Z_GQK:Z_GV].reshape(Ts, 2, GLA_HEADS, GLA_DK).transpose(1, 2, 0, 3)
        yg, s1 = _gla_sample(qk_h[0], qk_h[1], z, small, wa_heads[l], ba_heads[l], gn,
                             state_gla[l].reshape(nb, -1), nb, nt)
        yf_s, lfn = _fox_sample(pt_flat, to_seq(z[:, Z_FQ:Z_FK].astype(F32)), to_seq(fk), to_seq(fv),
                                jnp.swapaxes(fft.reshape(FOX_HEADS, nt, nb), 0, 2).swapaxes(1, 2), bf,
                                cache_kt, cache_vt, cache_lft, l, nb, nt, n_pages, page)
        yf = to_time(yf_s).astype(BF16)
        halo_sc = to_time(state_sconv[l])
        halo_ffn = to_time(state_ffn_conv[l])
        hs, sc_tail = _mix(hs, z, yg, yf, sconv_w[l], wb[l], wo[l], halo_sc, Ts, H_s, nb, 1)
        hs, ffn_tail = _ffn(hs, g2, wu_a[l], wu_g[l], ffn_conv_w[l], cb, wd[l], halo_ffn, fg, Ts, H_s, nb, 1)
        outs_s.append((to_seq(fk).reshape(nb, nt, FOX_HEADS, FOX_HD), to_seq(fv).reshape(nb, nt, FOX_HEADS, FOX_HD),
                       jnp.swapaxes(lfn, 1, 2), s1.reshape(nb, GLA_HEADS, GLA_DK, GLA_DV),
                       to_seq(sc_tail[0]), to_seq(ffn_tail[0])))

    p_states = [jnp.stack(t) for t in zip(*outs_p)]
    s_states = [jnp.stack(t) for t in zip(*outs_s)]
    y_prompt = hp.reshape(B, L, D_MODEL)
    y_sample = to_seq(hs)
    return (y_prompt, y_sample, *p_states, *s_states)
```

```python
import functools

import numpy as np
import jax
import jax.numpy as jnp
from jax import lax
from jax.experimental import pallas as pl
from jax.experimental.pallas import tpu as pltpu

F32 = jnp.float32
BF16 = jnp.bfloat16

D_MODEL = 1024
GLA_HEADS = 4
GLA_DK = 64
GLA_DV = 128
GLA_LR = 16
GLA_TAU = 16.0
FOX_HEADS = 8
FOX_HD = 64
SC_WIDTH = 512
CONV_K = 3
D_FF = 2816
N_BRANCH = 3
EPS = 1e-6

LANES = 128
SUBLANES = 8
VMEM_BYTES = 64 << 20

Z_GATES = 0
Z_GQK = 3072
Z_GV = 3584
Z_GG = 4096
Z_FQ = 4608
Z_FK = 5120
Z_FV = 5632
Z_SC = 6144
Z_COLS = 7680
IN_BN = 1536
SMALL_COLS = 128

GLA_CHUNK = 128
NEG = -1e30


def _cparams(sem, vmem_mb):
    return pltpu.CompilerParams(dimension_semantics=sem, vmem_limit_bytes=min(vmem_mb << 20, VMEM_BYTES - (4 << 20)))


def _log_sigmoid(x):
    return jnp.minimum(x, 0.0) - jnp.log1p(jnp.exp(-jnp.abs(x)))


def _dot(a, b):
    return jnp.dot(a, b, preferred_element_type=F32)


def _dot_nt(a, b):
    return lax.dot_general(a, b, (((1,), (1,)), ((), ())), preferred_element_type=F32)


def _dot_tn(a, b):
    return lax.dot_general(a, b, (((0,), (0,)), ((), ())), preferred_element_type=F32)


def _rms(x, g):
    return x * lax.rsqrt(jnp.mean(x * x, axis=-1, keepdims=True) + EPS) * g


def _lane_prefix_sum(c, lane):
    sh = 1
    while sh < LANES:
        c = c + jnp.where(lane >= sh, pltpu.roll(c, sh, axis=1), 0.0)
        sh *= 2
    return c


def _in_proj_body(x_ref, g_ref, w_ref, ws_ref, wft_ref, z_ref, fk_ref, fv_ref, small_ref, fft_ref, xn_scr,
                  *, transposed):
    j = pl.program_id(1)

    @pl.when(j == 0)
    def _():
        xb = _rms(x_ref[...], g_ref[...]).astype(BF16)
        xn_scr[...] = xb
        small_ref[...] = _dot(xb, ws_ref[...])
        fft_ref[...] = _dot_nt(wft_ref[...], xb)

    zt = _dot(xn_scr[...], w_ref[...])
    z_ref[...] = zt.astype(BF16)

    W = FOX_HEADS * FOX_HD

    def part(off):
        blk = zt[:, off % IN_BN:off % IN_BN + W]
        return blk.T if transposed else blk

    @pl.when(j == Z_FK // IN_BN)
    def _():
        fk_ref[...] = part(Z_FK)

    @pl.when(j == Z_FV // IN_BN)
    def _():
        fv_ref[...] = part(Z_FV)


def _in_proj(x, g, w, ws, wft, layer, bm, seqs=None):
    T = x.shape[0]
    W = FOX_HEADS * FOX_HD
    grid = (T // bm, Z_COLS // IN_BN)
    if seqs is None:
        kv_spec = pl.BlockSpec((bm, W), lambda i, j: (i, 0))
        kv_shape = jax.ShapeDtypeStruct((T, W), F32)
    else:
        B, L = seqs
        per_seq = L // bm
        kv_spec = pl.BlockSpec((None, W, bm), lambda i, j: (i // per_seq, 0, i % per_seq))
        kv_shape = jax.ShapeDtypeStruct((B, W, L), F32)
    return pl.pallas_call(
        functools.partial(_in_proj_body, transposed=seqs is not None),
        grid=grid,
        in_specs=[
            pl.BlockSpec((bm, D_MODEL), lambda i, j: (i, 0)),
            pl.BlockSpec((1, D_MODEL), lambda i, j: (0, 0)),
            pl.BlockSpec((None, D_MODEL, IN_BN), lambda i, j: (layer, 0, j)),
            pl.BlockSpec((None, D_MODEL, SMALL_COLS), lambda i, j: (layer, 0, 0)),
            pl.BlockSpec((None, FOX_HEADS, D_MODEL), lambda i, j: (layer, 0, 0)),
        ],
        out_specs=[
            pl.BlockSpec((bm, IN_BN), lambda i, j: (i, j)),
            kv_spec,
            kv_spec,
            pl.BlockSpec((bm, SMALL_COLS), lambda i, j: (i, 0)),
            pl.BlockSpec((FOX_HEADS, bm), lambda i, j: (0, i)),
        ],
        out_shape=[
            jax.ShapeDtypeStruct((T, Z_COLS), BF16),
            kv_shape,
            kv_shape,
            jax.ShapeDtypeStruct((T, SMALL_COLS), F32),
            jax.ShapeDtypeStruct((FOX_HEADS, T), F32),
        ],
        scratch_shapes=[pltpu.VMEM((bm, D_MODEL), BF16)],
        compiler_params=_cparams(("arbitrary", "arbitrary"), 56),
    )(x, g, w, ws, wft)


def _gla_consts(C):
    t = np.arange(C)
    r = t[None, :]
    mats = [r <= t[:, None], r > t[:, None]]
    masks = []
    n = C // 2
    while n >= 1:
        pair = t // (2 * n)
        second = (t // n) % 2 == 1
        mid = pair * 2 * n + n
        mats.append(np.where(second[:, None], (r >= mid[:, None]) & (r <= t[:, None]),
                             (r > t[:, None]) & (r < mid[:, None])))
        masks.append(second[:, None] & ~second[None, :] & (pair[:, None] == pair[None, :]))
        n //= 2
    masks.append(np.eye(C, dtype=bool))
    return (jnp.asarray(np.concatenate(mats, 0).astype(np.float32), BF16),
            jnp.asarray(np.stack(masks).astype(np.float32)))


def _gla_prompt_body(qk_ref, v_ref, gg_ref, small_ref, fft_ref, wa_ref, ba_ref, bf_ref, gn_ref, wall_ref,
                     masks_ref, y_ref, st_ref, logf_ref, c_ref, s_scr, c_scr, *, C, nlev):
    i = pl.program_id(1)

    @pl.when(i == 0)
    def _():
        s_scr[...] = jnp.zeros_like(s_scr)
        c_scr[...] = jnp.zeros_like(c_scr)

    lf = _log_sigmoid(fft_ref[...] + bf_ref[...])
    logf_ref[...] = lf
    lane = lax.broadcasted_iota(jnp.int32, lf.shape, 1)
    c = _lane_prefix_sum(lf, lane) + c_scr[...]
    c_ref[...] = c
    c_scr[...] = jnp.broadcast_to(c[:, C - 1:C], c_scr.shape)

    la = _log_sigmoid(_dot(small_ref[...].astype(BF16), wa_ref[...]) + ba_ref[...]) * (1.0 / GLA_TAU)
    hi = la.astype(BF16)
    lo = (la - hi.astype(F32)).astype(BF16)
    wall = wall_ref[...]
    P = jnp.exp(_dot(wall, hi) + _dot(wall, lo))

    W = GLA_HEADS * GLA_DK
    qk = qk_ref[...].astype(F32)
    q = qk[:, :W] * (GLA_DK ** -0.5)
    k = qk[:, W:]
    qd = (q * P[0:C]).astype(BF16)
    kd = (k * P[C:2 * C]).astype(BF16)
    a_last = P[C - 1:C, :]
    row = lax.broadcasted_iota(jnp.int32, (C, 1), 0)
    X = []
    n = C // 2
    for l in range(nlev):
        X.append((jnp.where((row & n) != 0, q, k) * P[(2 + l) * C:(3 + l) * C]).astype(BF16))
        n //= 2
    qb = q.astype(BF16)
    kb = k.astype(BF16)

    lane_w = lax.broadcasted_iota(jnp.int32, (1, W), 1) >> (GLA_DK.bit_length() - 1)

    def by_head(a):
        return jnp.concatenate([jnp.where(lane_w == h, a, jnp.zeros_like(a)) for h in range(GLA_HEADS)], axis=0)

    scores = masks_ref[nlev] * _dot_nt(by_head(qb), kb).reshape(GLA_HEADS, C, C)
    for l in range(nlev):
        scores = scores + masks_ref[l] * _dot_nt(by_head(X[l]), X[l]).reshape(GLA_HEADS, C, C)

    lane_p = lax.broadcasted_iota(jnp.int32, (1, LANES), 1)
    for p in range(GLA_HEADS // 2):
        sl = slice(LANES * p, LANES * (p + 1))
        ST = s_scr[p]
        STb = ST.astype(BF16)
        U = []
        for hh in range(2):
            h = 2 * p + hh
            hm = (lane_p >= GLA_DK) == bool(hh)

            def msk(a):
                return jnp.where(hm, a, jnp.zeros_like(a))

            vh = v_ref[:, GLA_DV * h:GLA_DV * (h + 1)]
            o = _dot(scores[h].astype(BF16), vh) + _dot_nt(msk(qd[:, sl]), STb)
            U.append(_dot_tn(vh, kd[:, sl]))
            g = gg_ref[:, GLA_DV * h:GLA_DV * (h + 1)].astype(F32)
            y_ref[:, GLA_DV * h:GLA_DV * (h + 1)] = (_rms(o, gn_ref[...]) * (g * jax.nn.sigmoid(g))).astype(BF16)
        s_scr[p] = ST * a_last[:, sl] + jnp.where(lane_p < GLA_DK, U[0], U[1])

    @pl.when(i == pl.num_programs(1) - 1)
    def _():
        st_ref[0] = s_scr[...]


def _gla_prompt(z, small, fft, wa, ba, bf, gn, B, L):
    C = GLA_CHUNK
    nL = L // C
    wall, masks = _gla_consts(C)
    nlev = masks.shape[0] - 1
    T = B * L
    W = GLA_HEADS * GLA_DK
    row = lambda b, i: b * nL + i
    return pl.pallas_call(
        functools.partial(_gla_prompt_body, C=C, nlev=nlev),
        grid=(B, nL),
        in_specs=[
            pl.BlockSpec((C, 2 * W), lambda b, i: (row(b, i), Z_GQK // (2 * W))),
            pl.BlockSpec((C, GLA_HEADS * GLA_DV), lambda b, i: (row(b, i), Z_GV // (GLA_HEADS * GLA_DV))),
            pl.BlockSpec((C, GLA_HEADS * GLA_DV), lambda b, i: (row(b, i), Z_GG // (GLA_HEADS * GLA_DV))),
            pl.BlockSpec((C, SMALL_COLS), lambda b, i: (row(b, i), 0)),
            pl.BlockSpec((FOX_HEADS, C), lambda b, i: (0, row(b, i))),
            pl.BlockSpec((SMALL_COLS, W), lambda b, i: (0, 0)),
            pl.BlockSpec((1, W), lambda b, i: (0, 0)),
            pl.BlockSpec((FOX_HEADS, 1), lambda b, i: (0, 0)),
            pl.BlockSpec((1, GLA_DV), lambda b, i: (0, 0)),
            pl.BlockSpec(wall.shape, lambda b, i: (0, 0)),
            pl.BlockSpec(masks.shape, lambda b, i: (0, 0, 0)),
        ],
        out_specs=[
            pl.BlockSpec((C, GLA_HEADS * GLA_DV), lambda b, i: (row(b, i), 0)),
            pl.BlockSpec((1, GLA_HEADS // 2, GLA_DV, 2 * GLA_DK), lambda b, i: (b, 0, 0, 0)),
            pl.BlockSpec((FOX_HEADS, C), lambda b, i: (0, row(b, i))),
            pl.BlockSpec((FOX_HEADS, C), lambda b, i: (0, row(b, i))),
        ],
        out_shape=[
            jax.ShapeDtypeStruct((T, GLA_HEADS * GLA_DV), BF16),
            jax.ShapeDtypeStruct((B, GLA_HEADS // 2, GLA_DV, 2 * GLA_DK), F32),
            jax.ShapeDtypeStruct((FOX_HEADS, T), F32),
            jax.ShapeDtypeStruct((FOX_HEADS, T), F32),
        ],
        scratch_shapes=[pltpu.VMEM((GLA_HEADS // 2, GLA_DV, 2 * GLA_DK), F32),
                        pltpu.VMEM((FOX_HEADS, LANES), F32)],
        compiler_params=_cparams(("arbitrary", "arbitrary"), 32),
    )(z, z, z, small, fft, wa, ba, bf, gn, wall, masks)


def _gla_sample_body(q_ref, k_ref, v_ref, gg_ref, small_ref, wa_ref, ba_ref, gn_ref, s0_ref, y_ref, s1_ref,
                     *, nb, nt):
    la = _log_sigmoid(_dot(small_ref[...].astype(BF16), wa_ref[0]) + ba_ref[0]) * (1.0 / GLA_TAU)
    q = q_ref[0].astype(F32) * (GLA_DK ** -0.5)
    k = k_ref[0].astype(F32)
    rows = [slice(t * nb, (t + 1) * nb) for t in range(nt)]
    b = []
    for t in range(nt):
        b.append(la[rows[t]] if t == 0 else b[-1] + la[rows[t]])
    qt = [q[r] for r in rows]
    kt = [k[r] for r in rows]
    qd = [qt[t] * jnp.exp(b[t]) for t in range(nt)]
    kd = [kt[t] * jnp.exp(b[nt - 1] - b[t]) for t in range(nt)]
    a_last = jnp.exp(b[nt - 1])

    for d in range(GLA_DK):
        cs = slice(d * GLA_DV, (d + 1) * GLA_DV)
        acc = a_last[:, d:d + 1] * s0_ref[:, cs]
        for t in range(nt):
            acc = acc + kd[t][:, d:d + 1] * v_ref[rows[t], :].astype(F32)
        s1_ref[:, cs] = acc

    for t in range(nt):
        o = jnp.zeros((nb, GLA_DV), F32)
        for s in range(t + 1):
            w = jnp.sum(qt[t] * kt[s] * jnp.exp(b[t] - b[s]), axis=-1, keepdims=True)
            o = o + w * v_ref[rows[s], :].astype(F32)
        for d in range(GLA_DK):
            o = o + qd[t][:, d:d + 1] * s0_ref[:, d * GLA_DV:(d + 1) * GLA_DV]
        g = gg_ref[rows[t], :].astype(F32)
        y_ref[rows[t], :] = (_rms(o, gn_ref[...]) * (g * jax.nn.sigmoid(g))).astype(BF16)


def _gla_sample(qh, kh, z, small, wa_h, ba_h, gn, s0, layer, nb, nt):
    T = nb * nt
    slab = GLA_DK * GLA_DV
    return pl.pallas_call(
        functools.partial(_gla_sample_body, nb=nb, nt=nt),
        grid=(GLA_HEADS,),
        in_specs=[
            pl.BlockSpec((1, T, GLA_DK), lambda h: (h, 0, 0)),
            pl.BlockSpec((1, T, GLA_DK), lambda h: (h, 0, 0)),
            pl.BlockSpec((T, GLA_DV), lambda h: (0, Z_GV // GLA_DV + h)),
            pl.BlockSpec((T, GLA_DV), lambda h: (0, Z_GG // GLA_DV + h)),
            pl.BlockSpec((T, SMALL_COLS), lambda h: (0, 0)),
            pl.BlockSpec((1, SMALL_COLS, GLA_DK), lambda h: (h, 0, 0)),
            pl.BlockSpec((1, 1, GLA_DK), lambda h: (h, 0, 0)),
            pl.BlockSpec((1, GLA_DV), lambda h: (0, 0)),
            pl.BlockSpec((None, nb, slab), lambda h: (layer, 0, h)),
        ],
        out_specs=[
            pl.BlockSpec((T, GLA_DV), lambda h: (0, h)),
            pl.BlockSpec((nb, slab), lambda h: (0, h)),
        ],
        out_shape=[
            jax.ShapeDtypeStruct((T, GLA_HEADS * GLA_DV), BF16),
            jax.ShapeDtypeStruct((nb, GLA_HEADS * slab), F32),
        ],
        compiler_params=_cparams(("arbitrary",), 40),
    )(qh, kh, z, z, small, wa_h, ba_h, gn, s0)


def _split3(c):
    def top(x):
        bits = lax.bitcast_convert_type(x, jnp.uint32) & jnp.uint32(0xFFFF0000)
        return lax.bitcast_convert_type(bits, F32)

    hi = top(c)
    mid = top(c - hi)
    return hi, mid, c - hi - mid


FOX_PREP_ROWS = 512
FOX_SUM_ROWS = 16
LOG2E = 1.4426950408889634


def _fox_prompt_body(q_ref, k_ref, v_ref, ck_ref, cq_ref, o_ref, k_scr, vt_scr, s_scr, *, bq, bk):
    i = pl.program_id(2)
    row0 = i * bq
    n_full = lax.div(row0, bk)
    lane = lax.broadcasted_iota(jnp.int32, (1, LANES), 1)
    NB = 3

    @pl.when(i == 0)
    def _():
        ch = min(FOX_PREP_ROWS, k_ref.shape[0])

        def prep(r, carry):
            start = pl.multiple_of(r * ch, ch)
            kc = k_ref[pl.ds(start, ch), :].astype(F32)
            cc = ck_ref[0, pl.ds(start, ch), :]
            vt = v_ref[pl.ds(start, ch), :].astype(F32).T
            ones = jnp.ones((FOX_SUM_ROWS, ch), F32)
            for hh in range(2):
                kh = kc if hh == 0 else pltpu.roll(kc, FOX_HD, axis=1)
                tile = jnp.where(lane < FOX_HD + NB, 1.0, 0.0)
                for n, term in enumerate(_split3(cc[:, hh:hh + 1] * LOG2E)):
                    tile = jnp.where(lane == FOX_HD + NB + n, -term, tile)
                k_scr[hh, pl.ds(start, ch), :] = jnp.where(lane < FOX_HD, kh, tile).astype(BF16)
                vt_scr[hh, :, pl.ds(start, ch)] = jnp.concatenate(
                    [vt[hh * FOX_HD:(hh + 1) * FOX_HD], ones], axis=0).astype(BF16)
            return carry

        lax.fori_loop(0, k_ref.shape[0] // ch, prep, 0)

    qt = (q_ref[...].astype(F32) * (FOX_HD ** -0.5 * LOG2E)).T
    row = lax.broadcasted_iota(jnp.int32, (LANES, 1), 0)
    qts = []
    for hh in range(2):
        tile = jnp.concatenate([qt[hh * FOX_HD:(hh + 1) * FOX_HD], jnp.zeros((LANES - FOX_HD, bq), F32)], axis=0)
        tile = jnp.where((row >= FOX_HD + NB) & (row < FOX_HD + 2 * NB), 1.0, tile)
        for n, term in enumerate(_split3(cq_ref[0, hh:hh + 1, :] * LOG2E)):
            tile = jnp.where(row == FOX_HD + n, term, tile)
        qts.append(tile.astype(BF16))

    def scores(j):
        start = pl.multiple_of(j * bk, bk)
        return tuple(_dot(k_scr[hh, pl.ds(start, bk), :], qts[hh]) for hh in range(2))

    def softmax(j, s, m, masked):
        out = []
        for hh in range(2):
            sh = s[hh]
            if masked:
                key = j * bk + lax.broadcasted_iota(jnp.int32, (bk, bq), 0)
                qry = row0 + lax.broadcasted_iota(jnp.int32, (bk, bq), 1)
                sh = jnp.where(key <= qry, sh, NEG)
            m_new = jnp.maximum(m[hh], jnp.max(sh, axis=0, keepdims=True))
            out.append((m_new, jnp.exp2(sh - m_new).astype(BF16), jnp.exp2(m[hh] - m_new)))
        return tuple(zip(*out))

    def accumulate(j, p, alpha, acc):
        start = pl.multiple_of(j * bk, bk)
        return tuple(alpha[hh] * acc[hh] + _dot(vt_scr[hh, :, pl.ds(start, bk)], p[hh]) for hh in range(2))

    def put_scores(j):
        for hh, sh in enumerate(scores(j)):
            s_scr[hh] = sh

    def step(j, carry):
        m, acc = carry
        m, p, alpha = softmax(j, (s_scr[0], s_scr[1]), m, False)
        put_scores(j + 1)
        return m, accumulate(j, p, alpha, acc)

    two = lambda a: (a, a)
    put_scores(0)
    acc0 = jnp.zeros((FOX_HD + FOX_SUM_ROWS, bq), F32)
    m, acc = lax.fori_loop(0, n_full, step, (two(jnp.full((1, bq), NEG, F32)), two(acc0)))
    m, p, alpha = softmax(n_full, (s_scr[0], s_scr[1]), m, True)
    acc = accumulate(n_full, p, alpha, acc)
    o = jnp.concatenate([a[:FOX_HD] / a[FOX_HD:FOX_HD + 1] for a in acc], axis=0)
    o_ref[...] = o.T.astype(BF16)


def _fox_prompt(z, c_cols, c_rows, B, L, bq, bk):
    nq = L // bq
    T = B * L
    npair = FOX_HEADS // 2
    return pl.pallas_call(
        functools.partial(_fox_prompt_body, bq=bq, bk=bk),
        grid=(B, npair, nq),
        in_specs=[
            pl.BlockSpec((bq, LANES), lambda b, p, i: (b * nq + i, Z_FQ // LANES + p)),
            pl.BlockSpec((L, LANES), lambda b, p, i: (b, Z_FK // LANES + p)),
            pl.BlockSpec((L, LANES), lambda b, p, i: (b, Z_FV // LANES + p)),
            pl.BlockSpec((1, L, 2), lambda b, p, i: (p, b, 0)),
            pl.BlockSpec((1, 2, bq), lambda b, p, i: (p, 0, b * nq + i)),
        ],
        out_specs=pl.BlockSpec((bq, LANES), lambda b, p, i: (b * nq + i, p)),
        out_shape=jax.ShapeDtypeStruct((T, FOX_HEADS * FOX_HD), BF16),
        scratch_shapes=[pltpu.VMEM((2, L, LANES), BF16), pltpu.VMEM((2, FOX_HD + FOX_SUM_ROWS, L), BF16),
                        pltpu.VMEM((2, bk, bq), F32)],
        compiler_params=_cparams(("arbitrary", "arbitrary", "arbitrary"), 56),
    )(z, z, z, c_cols, c_rows)


def _fox_sample_body(pt_ref, q_ref, kn_ref, vn_ref, fft_ref, bf_ref, kt_hbm, vt_hbm, lft_hbm, o_ref, lfo_ref,
                     kbuf, vbuf, lfbuf, k_scr, v_scr, sem, *, layer, n_pages, nt, page, nseq):
    s = pl.program_id(0)
    slot = lax.rem(s, 2)

    def page_copies(step, slot_):
        out = []
        for n in range(nseq * n_pages):
            pid = pt_ref[step * (nseq * n_pages) + n]
            out.append(pltpu.make_async_copy(kt_hbm.at[layer, pid], kbuf.at[slot_, n], sem.at[slot_, 0]))
            out.append(pltpu.make_async_copy(vt_hbm.at[layer, pid], vbuf.at[slot_, n], sem.at[slot_, 1]))
            out.append(pltpu.make_async_copy(lft_hbm.at[layer, pid], lfbuf.at[slot_, n], sem.at[slot_, 2]))
        return out

    @pl.when(s == 0)
    def _():
        for c in page_copies(s, slot):
            c.start()

    @pl.when(s + 1 < pl.num_programs(0))
    def _():
        for c in page_copies(s + 1, 1 - slot):
            c.start()

    for n, (hbm, buf) in enumerate(((kt_hbm, kbuf), (vt_hbm, vbuf), (lft_hbm, lfbuf))):
        pltpu.make_async_copy(hbm.at[layer, pl.ds(0, nseq * n_pages)], buf.at[slot], sem.at[slot, n]).wait()

    for u in range(nseq):
        pages = [u * n_pages + j for j in range(n_pages)]
        _fox_sample_one(q_ref.at[u], kn_ref.at[u], vn_ref.at[u], fft_ref.at[u], bf_ref,
                        [kbuf.at[slot, n] for n in pages], [vbuf.at[slot, n] for n in pages],
                        [lfbuf.at[slot, n] for n in pages], o_ref.at[u], lfo_ref.at[u], k_scr.at[u], v_scr.at[u],
                        n_pages=n_pages, nt=nt, page=page)


def _fox_sample_one(q_ref, kn_ref, vn_ref, fft_ref, bf_ref, k_refs, v_refs, lf_refs, o_ref, lfo_ref, k_scr, v_scr,
                    *, n_pages, nt, page):
    W = FOX_HEADS * FOX_HD
    H = FOX_HEADS
    q = q_ref[...] * (FOX_HD ** -0.5)
    head_shift = FOX_HD.bit_length() - 1
    hmask = (lax.broadcasted_iota(jnp.int32, (H, W), 1) >> head_shift) == lax.broadcasted_iota(jnp.int32, (H, W), 0)
    qbd = jnp.concatenate([jnp.where(hmask, jnp.broadcast_to(q[t:t + 1], (H, W)), 0.0) for t in range(nt)], axis=0)
    qbd_b = qbd.astype(BF16)

    lane = lax.broadcasted_iota(jnp.int32, (H, LANES), 1)
    carry = jnp.zeros((H, 1), F32)
    ck = []
    for j in range(n_pages):
        c = _lane_prefix_sum(lf_refs[j][...], lane) + carry
        carry = c[:, LANES - 1:LANES]
        ck.append(c)
    lfn = _log_sigmoid(fft_ref[...] + bf_ref[...])
    lfo_ref[...] = lfn
    lane_t = lax.broadcasted_iota(jnp.int32, (H, nt), 1)
    cn = carry
    for j in range(nt):
        cn = cn + jnp.where(lane_t >= j, lfn[:, j:j + 1], 0.0)
    cq = jnp.concatenate([cn[:, t:t + 1] for t in range(nt)], axis=0)

    for j in range(n_pages):
        k_scr[:, j * page:(j + 1) * page] = k_refs[j][...].reshape(W, page).astype(BF16)
        v_scr[:, j * page:(j + 1) * page] = v_refs[j][...].reshape(W, page).astype(BF16)
    ck_all = jnp.concatenate(ck, axis=1)
    s_past = _dot(qbd_b, k_scr[...]) + (cq - jnp.concatenate([ck_all] * nt, axis=0))

    kn = kn_ref[...]
    vn = vn_ref[...]
    trow = lax.broadcasted_iota(jnp.int32, (nt * H, 1), 0) >> (H.bit_length() - 1)
    s_new = []
    for tp in range(nt):
        ckn = jnp.concatenate([cn[:, tp:tp + 1]] * nt, axis=0)
        sn = jnp.sum(qbd * kn[tp:tp + 1], axis=-1, keepdims=True) + (cq - ckn)
        s_new.append(jnp.where(trow >= tp, sn, NEG))
    m = jnp.max(s_past, axis=-1, keepdims=True)
    for sn in s_new:
        m = jnp.maximum(m, sn)
    p_past = jnp.exp(s_past - m)
    den = jnp.sum(p_past, axis=-1, keepdims=True)
    out = _dot_nt(p_past.astype(BF16), v_scr[...])
    for tp in range(nt):
        pn = jnp.exp(s_new[tp] - m)
        den = den + pn
        out = out + pn * vn[tp:tp + 1]
    out = out / den
    for t in range(nt):
        o_ref[t:t + 1, :] = jnp.sum(jnp.where(hmask, out[t * H:(t + 1) * H], 0.0), axis=0, keepdims=True)


FOX_SEQS_PER_STEP = 2


def _fox_sample(pt_flat, q_s, kn_s, vn_s, fft_s, bf, cache_kt, cache_vt, cache_lft, layer, nb, nt, n_pages, page):
    W = FOX_HEADS * FOX_HD
    nseq = FOX_SEQS_PER_STEP if nb % FOX_SEQS_PER_STEP == 0 else 1
    assert page == LANES and cache_kt.shape[1] >= nseq * n_pages

    seq3 = lambda s, pt: (s, 0, 0)
    in_specs = [
        pl.BlockSpec((nseq, nt, W), seq3),
        pl.BlockSpec((nseq, nt, W), seq3),
        pl.BlockSpec((nseq, nt, W), seq3),
        pl.BlockSpec((nseq, FOX_HEADS, nt), seq3),
        pl.BlockSpec((FOX_HEADS, 1), lambda s, pt: (0, 0)),
        pl.BlockSpec(memory_space=pl.ANY),
        pl.BlockSpec(memory_space=pl.ANY),
        pl.BlockSpec(memory_space=pl.ANY),
    ]
    grid_spec = pltpu.PrefetchScalarGridSpec(
        num_scalar_prefetch=1,
        grid=(nb // nseq,),
        in_specs=in_specs,
        out_specs=[pl.BlockSpec((nseq, nt, W), seq3), pl.BlockSpec((nseq, FOX_HEADS, nt), seq3)],
        scratch_shapes=[pltpu.VMEM((2, nseq * n_pages, FOX_HEADS, FOX_HD, page), F32),
                        pltpu.VMEM((2, nseq * n_pages, FOX_HEADS, FOX_HD, page), F32),
                        pltpu.VMEM((2, nseq * n_pages, FOX_HEADS, page), F32),
                        pltpu.VMEM((nseq, W, n_pages * page), BF16), pltpu.VMEM((nseq, W, n_pages * page), BF16),
                        pltpu.SemaphoreType.DMA((2, 3))],
    )
    return pl.pallas_call(
        functools.partial(_fox_sample_body, layer=layer, n_pages=n_pages, nt=nt, page=page, nseq=nseq),
        grid_spec=grid_spec,
        out_shape=[jax.ShapeDtypeStruct((nb, nt, W), F32), jax.ShapeDtypeStruct((nb, FOX_HEADS, nt), F32)],
        compiler_params=_cparams(("arbitrary",), 56),
    )(pt_flat, q_s, kn_s, vn_s, fft_s, bf, cache_kt, cache_vt, cache_lft)


def _causal_conv(ext_scr, u, cw_ref, H, shift):
    bm = u.shape[0]
    ext_scr[H:H + bm, :] = u
    out = cw_ref[CONV_K - 1:CONV_K, :] * u
    for i in range(1, CONV_K):
        lo = H - i * shift
        out = out + cw_ref[CONV_K - 1 - i:CONV_K - i, :] * ext_scr[lo:lo + bm, :]
    return out


def _mix_body(*refs, H, shift, tiles_per_seq, has_state):
    if has_state:
        h_ref, gates_ref, sc_ref, yg_ref, yf_ref, cw_ref, wb_ref, wo_ref, halo_ref, hout_ref, tail_ref, ext_scr = refs
    else:
        h_ref, gates_ref, sc_ref, yg_ref, yf_ref, cw_ref, wb_ref, wo_ref, hout_ref, tail_ref, ext_scr = refs
    bm = h_ref.shape[0]
    if has_state:
        ext_scr[0:H, :] = halo_ref[...]
    else:
        seq_start = pl.program_id(0) % tiles_per_seq == 0

        @pl.when(seq_start)
        def _():
            ext_scr[0:H, :] = jnp.zeros((H, SC_WIDTH), F32)

        @pl.when(jnp.logical_not(seq_start))
        def _():
            ext_scr[0:H, :] = ext_scr[bm:bm + H, :]

    sc = sc_ref[...].astype(F32)
    u = sc[:, SC_WIDTH:2 * SC_WIDTH] * sc[:, 2 * SC_WIDTH:]
    y_sc = (sc[:, :SC_WIDTH] * _causal_conv(ext_scr, u, cw_ref, H, shift)).astype(BF16)
    tail_ref[0] = ext_scr[bm:bm + H, :]

    acc = jnp.zeros((bm, D_MODEL), F32)
    for b, y in enumerate((yg_ref[...], yf_ref[...], y_sc)):
        g = jax.nn.sigmoid(gates_ref[:, b * D_MODEL:(b + 1) * D_MODEL].astype(F32))
        acc = acc + g * _dot(y, wb_ref[b])
    hout_ref[...] = h_ref[...] + _dot(acc.astype(BF16), wo_ref[...])


def _mix(h, z, yg, yf, cw, wb, wo, layer, halo, bm, H, shift, tiles_per_seq):
    T = h.shape[0]
    has_state = halo is not None
    in_specs = [
        pl.BlockSpec((bm, D_MODEL), lambda i: (i, 0)),
        pl.BlockSpec((bm, N_BRANCH * D_MODEL), lambda i: (i, 0)),
        pl.BlockSpec((bm, 3 * SC_WIDTH), lambda i: (i, Z_SC // (3 * SC_WIDTH))),
        pl.BlockSpec((bm, SC_WIDTH), lambda i: (i, 0)),
        pl.BlockSpec((bm, SC_WIDTH), lambda i: (i, 0)),
        pl.BlockSpec((CONV_K, SC_WIDTH), lambda i: (0, 0)),
        pl.BlockSpec((None, N_BRANCH, SC_WIDTH, D_MODEL), lambda i: (layer, 0, 0, 0)),
        pl.BlockSpec((None, D_MODEL, D_MODEL), lambda i: (layer, 0, 0)),
    ]
    args = [h, z, z, yg, yf, cw, wb, wo]
    if has_state:
        in_specs.append(pl.BlockSpec((H, SC_WIDTH), lambda i: (0, 0)))
        args.append(halo)
    return pl.pallas_call(
        functools.partial(_mix_body, H=H, shift=shift, tiles_per_seq=tiles_per_seq, has_state=has_state),
        grid=(T // bm,),
        in_specs=in_specs,
        out_specs=[pl.BlockSpec((bm, D_MODEL), lambda i: (i, 0)),
                   pl.BlockSpec((1, H, SC_WIDTH), lambda i: (i, 0, 0))],
        out_shape=[jax.ShapeDtypeStruct((T, D_MODEL), F32),
                   jax.ShapeDtypeStruct((T // bm, H, SC_WIDTH), F32)],
        scratch_shapes=[pltpu.VMEM((H + bm, SC_WIDTH), F32)],
        compiler_params=_cparams(("arbitrary",), 48),
    )(*args)


FFN_CK = 1408


def _ffn_body(*refs, H, shift, tiles_per_seq, has_state, final_norm):
    refs = list(refs)
    h_ref, g_ref, wa_ref, wg_ref, cw_ref, cb_ref, wd_ref = refs[:7]
    refs = refs[7:]
    halo_ref = refs.pop(0) if has_state else None
    fg_ref = refs.pop(0) if final_norm else None
    out_ref, tail_ref, xn_scr, acc_scr, ext_scr, halo_scr = refs
    i = pl.program_id(0)
    c = pl.program_id(1)
    bm = h_ref.shape[0]

    @pl.when(c == 0)
    def _():
        xn_scr[...] = _rms(h_ref[...], g_ref[...]).astype(BF16)
        acc_scr[...] = jnp.zeros_like(acc_scr)

    if has_state:
        ext_scr[0:H, :] = halo_ref[...]
    else:
        seq_start = i % tiles_per_seq == 0

        @pl.when(seq_start)
        def _():
            ext_scr[0:H, :] = jnp.zeros((H, ext_scr.shape[1]), F32)

        @pl.when(jnp.logical_not(seq_start))
        def _():
            ext_scr[0:H, :] = halo_scr[c]

    xn = xn_scr[...]
    a = _dot(xn, wa_ref[...])
    gate = _dot(xn, wg_ref[...])
    ac = _causal_conv(ext_scr, a, cw_ref, H, shift) + cb_ref[...]
    tail = ext_scr[bm:bm + H, :]
    tail_ref[0] = tail
    if not has_state:
        halo_scr[c] = tail
    act = 0.5 * ac * (1.0 + lax.erf(ac * (2.0 ** -0.5))) * gate
    acc_scr[...] += _dot(act.astype(BF16), wd_ref[...])

    @pl.when(c == pl.num_programs(1) - 1)
    def _():
        hn = h_ref[...] + acc_scr[...]
        out_ref[...] = _rms(hn, fg_ref[...]) if final_norm else hn


def _ffn(h, g, wu, cw, cb, wd, layer, halo, fg, bm, H, shift, tiles_per_seq):
    T = h.shape[0]
    ck = FFN_CK
    nc = D_FF // ck
    has_state = halo is not None
    final_norm = fg is not None
    in_specs = [
        pl.BlockSpec((bm, D_MODEL), lambda i, c: (i, 0)),
        pl.BlockSpec((1, D_MODEL), lambda i, c: (0, 0)),
        pl.BlockSpec((None, D_MODEL, ck), lambda i, c: (layer, 0, c)),
        pl.BlockSpec((None, D_MODEL, ck), lambda i, c: (layer, 0, nc + c)),
        pl.BlockSpec((CONV_K, ck), lambda i, c: (0, c)),
        pl.BlockSpec((1, ck), lambda i, c: (0, c)),
        pl.BlockSpec((None, ck, D_MODEL), lambda i, c: (layer, c, 0)),
    ]
    args = [h, g, wu, wu, cw, cb, wd]
    if has_state:
        in_specs.append(pl.BlockSpec((H, ck), lambda i, c: (0, c)))
        args.append(halo)
    if final_norm:
        in_specs.append(pl.BlockSpec((1, D_MODEL), lambda i, c: (0, 0)))
        args.append(fg)
    return pl.pallas_call(
        functools.partial(_ffn_body, H=H, shift=shift, tiles_per_seq=tiles_per_seq, has_state=has_state,
                          final_norm=final_norm),
        grid=(T // bm, nc),
        in_specs=in_specs,
        out_specs=[pl.BlockSpec((bm, D_MODEL), lambda i, c: (i, 0)),
                   pl.BlockSpec((1, H, ck), lambda i, c: (i, 0, c))],
        out_shape=[jax.ShapeDtypeStruct((T, D_MODEL), F32),
                   jax.ShapeDtypeStruct((T // bm, H, D_FF), F32)],
        scratch_shapes=[pltpu.VMEM((bm, D_MODEL), BF16), pltpu.VMEM((bm, D_MODEL), F32),
                        pltpu.VMEM((H + bm, ck), F32), pltpu.VMEM((nc, H, ck), F32)],
        compiler_params=_cparams(("arbitrary", "arbitrary"), 56),
    )(*args)


def _pack_in_proj(w_in):
    sizes = (GLA_HEADS * GLA_DK, GLA_HEADS * GLA_DK, GLA_HEADS * GLA_DV, GLA_LR, GLA_HEADS * GLA_DV,
             FOX_HEADS * FOX_HD, FOX_HEADS * FOX_HD, FOX_HEADS * FOX_HD, FOX_HEADS,
             SC_WIDTH, SC_WIDTH, SC_WIDTH, N_BRANCH * D_MODEL)
    offs = np.concatenate([[0], np.cumsum(sizes)])
    part = lambda n: w_in[:, :, offs[n]:offs[n + 1]]
    gq, gk, gv, glr, gg, fq, fk, fv, ff, scb, scc, sch, gates = [part(n) for n in range(len(sizes))]
    w_main = jnp.concatenate([gates, gq, gk, gv, gg, fq, fk, fv, scb, scc, sch], axis=-1).astype(BF16)
    pad = jnp.zeros(w_in.shape[:2] + (SMALL_COLS - GLA_LR - FOX_HEADS,), w_in.dtype)
    w_small = jnp.concatenate([glr, ff, pad], axis=-1).astype(BF16)
    w_fft = jnp.swapaxes(ff, 1, 2).astype(BF16)
    return w_main, w_small, w_fft


def kernel(x_prompt, x_sample, cache_fox_k, cache_fox_v, cache_fox_logf, state_gla, state_sconv, state_ffn_conv,
           page_table, norm1_g, w_in, w_alpha2, b_alpha, b_forget, gla_norm_g, sconv_w, w_branch, w_out, norm2_g,
           w_up, ffn_conv_w, ffn_conv_b, w_down, final_norm_g):
    depth = w_in.shape[0]
    B, L, _ = x_prompt.shape
    nb, nt, _ = x_sample.shape
    n_pool, page = cache_fox_k.shape[1], cache_fox_k.shape[2]
    n_pages = page_table.shape[1]
    W = FOX_HEADS * FOX_HD
    Tp = B * L
    Ts = nb * nt
    bm_p = min(512, L)
    bm_in = min(1024, L)
    fox_bq, fox_bk = min(512, L), min(512, L)
    H_p, H_s = SUBLANES, (CONV_K - 1) * nb

    w_main, w_small, w_fft = _pack_in_proj(w_in)
    wa_pad = jnp.concatenate(
        [w_alpha2, jnp.zeros((depth, SMALL_COLS - GLA_LR, GLA_HEADS * GLA_DK), w_alpha2.dtype)], axis=1).astype(BF16)
    wa_heads = wa_pad.reshape(depth, SMALL_COLS, GLA_HEADS, GLA_DK).transpose(0, 2, 1, 3)
    ba_heads = b_alpha.reshape(depth, GLA_HEADS, 1, GLA_DK)
    wb = w_branch.astype(BF16)
    wo = w_out.astype(BF16)
    wu = w_up.astype(BF16)
    wd = w_down.astype(BF16)
    gla_states = state_gla.reshape(depth, nb, -1)

    cache_kt = jnp.transpose(cache_fox_k, (0, 1, 3, 4, 2))
    cache_vt = jnp.transpose(cache_fox_v, (0, 1, 3, 4, 2))
    cache_lft = jnp.swapaxes(cache_fox_logf, 2, 3)
    pt_flat = page_table.reshape(-1).astype(jnp.int32)

    def to_seq(a):
        return jnp.swapaxes(a.reshape((a.shape[0] // nb, nb) + a.shape[1:]), 0, 1)

    def to_time(a):
        return jnp.swapaxes(a, 0, 1).reshape((a.shape[0] * a.shape[1],) + a.shape[2:])

    hp = x_prompt.reshape(Tp, D_MODEL)
    hs = to_time(x_sample)
    outs_p, outs_s = [], []
    for l in range(depth):
        g1 = norm1_g[l].reshape(1, D_MODEL)
        g2 = norm2_g[l].reshape(1, D_MODEL)
        bf = b_forget[l].reshape(FOX_HEADS, 1)
        ba = b_alpha[l].reshape(1, -1)
        gn = gla_norm_g[l].reshape(1, GLA_DV)
        cb = ffn_conv_b[l].reshape(1, D_FF)
        fg = final_norm_g.reshape(1, D_MODEL) if l == depth - 1 else None

        z, fkt, fvt, small, fft = _in_proj(hp, g1, w_main, w_small, w_fft, l, bm_in, (B, L))
        yg, st, logf_t, c_t = _gla_prompt(z, small, fft, wa_pad[l], ba, bf, gn, B, L)
        c_rows = c_t.reshape(FOX_HEADS // 2, 2, Tp)
        yf = _fox_prompt(z, jnp.swapaxes(c_rows, 1, 2), c_rows, B, L, fox_bq, fox_bk)
        hp, sc_tail = _mix(hp, z, yg, yf, sconv_w[l], wb, wo, l, None, bm_p, H_p, 1, L // bm_p)
        hp, ffn_tail = _ffn(hp, g2, wu, ffn_conv_w[l], cb, wd, l, None, fg, bm_p, H_p, 1, L // bm_p)
        last = lambda t: t.reshape(B, L // bm_p, H_p, -1)[:, -1, H_p - (CONV_K - 1):, :]
        p_gla = st.reshape(B, GLA_HEADS // 2, GLA_DV, 2, GLA_DK).transpose(0, 1, 3, 4, 2)
        heads_last = lambda a: a.reshape(B, FOX_HEADS, FOX_HD, L).transpose(0, 3, 1, 2)
        outs_p.append((heads_last(fkt), heads_last(fvt), logf_t.T.reshape(B, L, FOX_HEADS), p_gla.reshape(B, GLA_HEADS, GLA_DK, GLA_DV),
                       last(sc_tail), last(ffn_tail)))

        z, fk, fv, small, fft = _in_proj(hs, g1, w_main, w_small, w_fft, l, Ts)
        qk_h = z[:, Z_GQK:Z_GV].reshape(Ts, 2, GLA_HEADS, GLA_DK).transpose(1, 2, 0, 3)
        yg, s1 = _gla_sample(qk_h[0], qk_h[1], z, small, wa_heads[l], ba_heads[l], gn,
                             gla_states, l, nb, nt)
        yf_s, lfn = _fox_sample(pt_flat, to_seq(z[:, Z_FQ:Z_FK].astype(F32)), to_seq(fk), to_seq(fv),
                                jnp.swapaxes(fft.reshape(FOX_HEADS, nt, nb), 0, 2).swapaxes(1, 2), bf,
                                cache_kt, cache_vt, cache_lft, l, nb, nt, n_pages, page)
        yf = to_time(yf_s).astype(BF16)
        halo_sc = to_time(state_sconv[l])
        halo_ffn = to_time(state_ffn_conv[l])
        hs, sc_tail = _mix(hs, z, yg, yf, sconv_w[l], wb, wo, l, halo_sc, Ts, H_s, nb, 1)
        hs, ffn_tail = _ffn(hs, g2, wu, ffn_conv_w[l], cb, wd, l, halo_ffn, fg, Ts, H_s, nb, 1)
        outs_s.append((to_seq(fk).reshape(nb, nt, FOX_HEADS, FOX_HD), to_seq(fv).reshape(nb, nt, FOX_HEADS, FOX_HD),
                       jnp.swapaxes(lfn, 1, 2), s1.reshape(nb, GLA_HEADS, GLA_DK, GLA_DV),
                       to_seq(sc_tail[0]), to_seq(ffn_tail[0])))

    p_states = [jnp.stack(t) for t in zip(*outs_p)]
    s_states = [jnp.stack(t) for t in zip(*outs_s)]
    y_prompt = hp.reshape(B, L, D_MODEL)
    y_sample = to_seq(hs)
    return (y_prompt, y_sample, *p_states, *s_states)
```

```python
import functools

import numpy as np
import jax
import jax.numpy as jnp
from jax import lax
from jax.experimental import pallas as pl
from jax.experimental.pallas import tpu as pltpu

F32 = jnp.float32
BF16 = jnp.bfloat16

D_MODEL = 1024
GLA_HEADS = 4
GLA_DK = 64
GLA_DV = 128
GLA_LR = 16
GLA_TAU = 16.0
FOX_HEADS = 8
FOX_HD = 64
SC_WIDTH = 512
CONV_K = 3
D_FF = 2816
N_BRANCH = 3
EPS = 1e-6

LANES = 128
SUBLANES = 8
VMEM_BYTES = 64 << 20

Z_GATES = 0
Z_GQK = 3072
Z_GV = 3584
Z_GG = 4096
Z_FQ = 4608
Z_FK = 5120
Z_FV = 5632
Z_SC = 6144
Z_COLS = 7680
IN_BN = 1536
SMALL_COLS = 128

GLA_CHUNK = 128
NEG = -1e30


def _cparams(sem, vmem_mb):
    return pltpu.CompilerParams(dimension_semantics=sem, vmem_limit_bytes=min(vmem_mb << 20, VMEM_BYTES - (4 << 20)))


def _log_sigmoid(x):
    return jnp.minimum(x, 0.0) - jnp.log1p(jnp.exp(-jnp.abs(x)))


def _dot(a, b):
    return jnp.dot(a, b, preferred_element_type=F32)


def _dot_nt(a, b):
    return lax.dot_general(a, b, (((1,), (1,)), ((), ())), preferred_element_type=F32)


def _dot_tn(a, b):
    return lax.dot_general(a, b, (((0,), (0,)), ((), ())), preferred_element_type=F32)


def _rms(x, g):
    return x * lax.rsqrt(jnp.mean(x * x, axis=-1, keepdims=True) + EPS) * g


def _lane_prefix_sum(c, lane):
    sh = 1
    while sh < LANES:
        c = c + jnp.where(lane >= sh, pltpu.roll(c, sh, axis=1), 0.0)
        sh *= 2
    return c


def _in_proj_body(x_ref, g_ref, w_ref, ws_ref, wft_ref, z_ref, fk_ref, fv_ref, small_ref, fft_ref, xn_scr,
                  *, transposed):
    j = pl.program_id(1)

    @pl.when(j == 0)
    def _():
        xb = _rms(x_ref[...], g_ref[...]).astype(BF16)
        xn_scr[...] = xb
        small_ref[...] = _dot(xb, ws_ref[...])
        fft_ref[...] = _dot_nt(wft_ref[...], xb)

    zt = _dot(xn_scr[...], w_ref[...])
    z_ref[...] = zt.astype(BF16)

    W = FOX_HEADS * FOX_HD

    def part(off):
        blk = zt[:, off % IN_BN:off % IN_BN + W]
        return blk.T if transposed else blk

    @pl.when(j == Z_FK // IN_BN)
    def _():
        fk_ref[...] = part(Z_FK)

    @pl.when(j == Z_FV // IN_BN)
    def _():
        fv_ref[...] = part(Z_FV)


def _in_proj(x, g, w, ws, wft, layer, bm, seqs=None):
    T = x.shape[0]
    W = FOX_HEADS * FOX_HD
    grid = (T // bm, Z_COLS // IN_BN)
    if seqs is None:
        kv_spec = pl.BlockSpec((bm, W), lambda i, j: (i, 0))
        kv_shape = jax.ShapeDtypeStruct((T, W), F32)
    else:
        B, L = seqs
        per_seq = L // bm
        kv_spec = pl.BlockSpec((None, W, bm), lambda i, j: (i // per_seq, 0, i % per_seq))
        kv_shape = jax.ShapeDtypeStruct((B, W, L), F32)
    return pl.pallas_call(
        functools.partial(_in_proj_body, transposed=seqs is not None),
        grid=grid,
        in_specs=[
            pl.BlockSpec((bm, D_MODEL), lambda i, j: (i, 0)),
            pl.BlockSpec((1, D_MODEL), lambda i, j: (0, 0)),
            pl.BlockSpec((None, D_MODEL, IN_BN), lambda i, j: (layer, 0, j)),
            pl.BlockSpec((None, D_MODEL, SMALL_COLS), lambda i, j: (layer, 0, 0)),
            pl.BlockSpec((None, FOX_HEADS, D_MODEL), lambda i, j: (layer, 0, 0)),
        ],
        out_specs=[
            pl.BlockSpec((bm, IN_BN), lambda i, j: (i, j)),
            kv_spec,
            kv_spec,
            pl.BlockSpec((bm, SMALL_COLS), lambda i, j: (i, 0)),
            pl.BlockSpec((FOX_HEADS, bm), lambda i, j: (0, i)),
        ],
        out_shape=[
            jax.ShapeDtypeStruct((T, Z_COLS), BF16),
            kv_shape,
            kv_shape,
            jax.ShapeDtypeStruct((T, SMALL_COLS), F32),
            jax.ShapeDtypeStruct((FOX_HEADS, T), F32),
        ],
        scratch_shapes=[pltpu.VMEM((bm, D_MODEL), BF16)],
        compiler_params=_cparams(("arbitrary", "arbitrary"), 56),
    )(x, g, w, ws, wft)


def _gla_consts(C):
    t = np.arange(C)
    r = t[None, :]
    mats = [r <= t[:, None], r > t[:, None]]
    masks = []
    n = C // 2
    while n >= 1:
        pair = t // (2 * n)
        second = (t // n) % 2 == 1
        mid = pair * 2 * n + n
        mats.append(np.where(second[:, None], (r >= mid[:, None]) & (r <= t[:, None]),
                             (r > t[:, None]) & (r < mid[:, None])))
        masks.append(second[:, None] & ~second[None, :] & (pair[:, None] == pair[None, :]))
        n //= 2
    masks.append(np.eye(C, dtype=bool))
    return (jnp.asarray(np.concatenate(mats, 0).astype(np.float32), BF16),
            jnp.asarray(np.stack(masks).astype(np.float32)))


def _gla_prompt_body(qk_ref, v_ref, gg_ref, small_ref, fft_ref, wa_ref, ba_ref, bf_ref, gn_ref, wall_ref,
                     masks_ref, y_ref, st_ref, logf_ref, c_ref, s_scr, c_scr, *, C, nlev):
    i = pl.program_id(1)

    @pl.when(i == 0)
    def _():
        s_scr[...] = jnp.zeros_like(s_scr)
        c_scr[...] = jnp.zeros_like(c_scr)

    lf = _log_sigmoid(fft_ref[...] + bf_ref[...])
    logf_ref[...] = lf
    lane = lax.broadcasted_iota(jnp.int32, lf.shape, 1)
    c = _lane_prefix_sum(lf, lane) + c_scr[...]
    c_ref[...] = c
    c_scr[...] = jnp.broadcast_to(c[:, C - 1:C], c_scr.shape)

    la = _log_sigmoid(_dot(small_ref[...].astype(BF16), wa_ref[...]) + ba_ref[...]) * (1.0 / GLA_TAU)
    hi = la.astype(BF16)
    lo = (la - hi.astype(F32)).astype(BF16)
    wall = wall_ref[...]
    P = jnp.exp(_dot(wall, hi) + _dot(wall, lo))

    W = GLA_HEADS * GLA_DK
    qk = qk_ref[...].astype(F32)
    q = qk[:, :W] * (GLA_DK ** -0.5)
    k = qk[:, W:]
    qd = (q * P[0:C]).astype(BF16)
    kd = (k * P[C:2 * C]).astype(BF16)
    a_last = P[C - 1:C, :]
    row = lax.broadcasted_iota(jnp.int32, (C, 1), 0)
    X = []
    n = C // 2
    for l in range(nlev):
        X.append((jnp.where((row & n) != 0, q, k) * P[(2 + l) * C:(3 + l) * C]).astype(BF16))
        n //= 2
    qb = q.astype(BF16)
    kb = k.astype(BF16)

    lane_w = lax.broadcasted_iota(jnp.int32, (1, W), 1) >> (GLA_DK.bit_length() - 1)

    def by_head(a):
        return jnp.concatenate([jnp.where(lane_w == h, a, jnp.zeros_like(a)) for h in range(GLA_HEADS)], axis=0)

    scores = masks_ref[nlev] * _dot_nt(by_head(qb), kb).reshape(GLA_HEADS, C, C)
    for l in range(nlev):
        scores = scores + masks_ref[l] * _dot_nt(by_head(X[l]), X[l]).reshape(GLA_HEADS, C, C)

    lane_p = lax.broadcasted_iota(jnp.int32, (1, LANES), 1)
    for p in range(GLA_HEADS // 2):
        sl = slice(LANES * p, LANES * (p + 1))
        ST = s_scr[p]
        STb = ST.astype(BF16)
        U = []
        for hh in range(2):
            h = 2 * p + hh
            hm = (lane_p >= GLA_DK) == bool(hh)

            def msk(a):
                return jnp.where(hm, a, jnp.zeros_like(a))

            vh = v_ref[:, GLA_DV * h:GLA_DV * (h + 1)]
            o = _dot(scores[h].astype(BF16), vh) + _dot_nt(msk(qd[:, sl]), STb)
            U.append(_dot_tn(vh, kd[:, sl]))
            g = gg_ref[:, GLA_DV * h:GLA_DV * (h + 1)].astype(F32)
            y_ref[:, GLA_DV * h:GLA_DV * (h + 1)] = (_rms(o, gn_ref[...]) * (g * jax.nn.sigmoid(g))).astype(BF16)
        s_scr[p] = ST * a_last[:, sl] + jnp.where(lane_p < GLA_DK, U[0], U[1])

    @pl.when(i == pl.num_programs(1) - 1)
    def _():
        st_ref[0] = s_scr[...]


def _gla_prompt(z, small, fft, wa, ba, bf, gn, B, L):
    C = GLA_CHUNK
    nL = L // C
    wall, masks = _gla_consts(C)
    nlev = masks.shape[0] - 1
    T = B * L
    W = GLA_HEADS * GLA_DK
    row = lambda b, i: b * nL + i
    return pl.pallas_call(
        functools.partial(_gla_prompt_body, C=C, nlev=nlev),
        grid=(B, nL),
        in_specs=[
            pl.BlockSpec((C, 2 * W), lambda b, i: (row(b, i), Z_GQK // (2 * W))),
            pl.BlockSpec((C, GLA_HEADS * GLA_DV), lambda b, i: (row(b, i), Z_GV // (GLA_HEADS * GLA_DV))),
            pl.BlockSpec((C, GLA_HEADS * GLA_DV), lambda b, i: (row(b, i), Z_GG // (GLA_HEADS * GLA_DV))),
            pl.BlockSpec((C, SMALL_COLS), lambda b, i: (row(b, i), 0)),
            pl.BlockSpec((FOX_HEADS, C), lambda b, i: (0, row(b, i))),
            pl.BlockSpec((SMALL_COLS, W), lambda b, i: (0, 0)),
            pl.BlockSpec((1, W), lambda b, i: (0, 0)),
            pl.BlockSpec((FOX_HEADS, 1), lambda b, i: (0, 0)),
            pl.BlockSpec((1, GLA_DV), lambda b, i: (0, 0)),
            pl.BlockSpec(wall.shape, lambda b, i: (0, 0)),
            pl.BlockSpec(masks.shape, lambda b, i: (0, 0, 0)),
        ],
        out_specs=[
            pl.BlockSpec((C, GLA_HEADS * GLA_DV), lambda b, i: (row(b, i), 0)),
            pl.BlockSpec((1, GLA_HEADS // 2, GLA_DV, 2 * GLA_DK), lambda b, i: (b, 0, 0, 0)),
            pl.BlockSpec((FOX_HEADS, C), lambda b, i: (0, row(b, i))),
            pl.BlockSpec((FOX_HEADS, C), lambda b, i: (0, row(b, i))),
        ],
        out_shape=[
            jax.ShapeDtypeStruct((T, GLA_HEADS * GLA_DV), BF16),
            jax.ShapeDtypeStruct((B, GLA_HEADS // 2, GLA_DV, 2 * GLA_DK), F32),
            jax.ShapeDtypeStruct((FOX_HEADS, T), F32),
            jax.ShapeDtypeStruct((FOX_HEADS, T), F32),
        ],
        scratch_shapes=[pltpu.VMEM((GLA_HEADS // 2, GLA_DV, 2 * GLA_DK), F32),
                        pltpu.VMEM((FOX_HEADS, LANES), F32)],
        compiler_params=_cparams(("arbitrary", "arbitrary"), 32),
    )(z, z, z, small, fft, wa, ba, bf, gn, wall, masks)


def _gla_sample_body(q_ref, k_ref, v_ref, gg_ref, small_ref, wa_ref, ba_ref, gn_ref, s0_ref, y_ref, s1_ref,
                     *, nb, nt):
    la = _log_sigmoid(_dot(small_ref[...].astype(BF16), wa_ref[0]) + ba_ref[0]) * (1.0 / GLA_TAU)
    q = q_ref[0].astype(F32) * (GLA_DK ** -0.5)
    k = k_ref[0].astype(F32)
    rows = [slice(t * nb, (t + 1) * nb) for t in range(nt)]
    b = []
    for t in range(nt):
        b.append(la[rows[t]] if t == 0 else b[-1] + la[rows[t]])
    qt = [q[r] for r in rows]
    kt = [k[r] for r in rows]
    qd = [qt[t] * jnp.exp(b[t]) for t in range(nt)]
    kd = [kt[t] * jnp.exp(b[nt - 1] - b[t]) for t in range(nt)]
    a_last = jnp.exp(b[nt - 1])

    for d in range(GLA_DK):
        cs = slice(d * GLA_DV, (d + 1) * GLA_DV)
        acc = a_last[:, d:d + 1] * s0_ref[:, cs]
        for t in range(nt):
            acc = acc + kd[t][:, d:d + 1] * v_ref[rows[t], :].astype(F32)
        s1_ref[:, cs] = acc

    for t in range(nt):
        o = jnp.zeros((nb, GLA_DV), F32)
        for s in range(t + 1):
            w = jnp.sum(qt[t] * kt[s] * jnp.exp(b[t] - b[s]), axis=-1, keepdims=True)
            o = o + w * v_ref[rows[s], :].astype(F32)
        for d in range(GLA_DK):
            o = o + qd[t][:, d:d + 1] * s0_ref[:, d * GLA_DV:(d + 1) * GLA_DV]
        g = gg_ref[rows[t], :].astype(F32)
        y_ref[rows[t], :] = (_rms(o, gn_ref[...]) * (g * jax.nn.sigmoid(g))).astype(BF16)


def _gla_sample(qh, kh, z, small, wa_h, ba_h, gn, s0, layer, nb, nt):
    T = nb * nt
    slab = GLA_DK * GLA_DV
    return pl.pallas_call(
        functools.partial(_gla_sample_body, nb=nb, nt=nt),
        grid=(GLA_HEADS,),
        in_specs=[
            pl.BlockSpec((1, T, GLA_DK), lambda h: (h, 0, 0)),
            pl.BlockSpec((1, T, GLA_DK), lambda h: (h, 0, 0)),
            pl.BlockSpec((T, GLA_DV), lambda h: (0, Z_GV // GLA_DV + h)),
            pl.BlockSpec((T, GLA_DV), lambda h: (0, Z_GG // GLA_DV + h)),
            pl.BlockSpec((T, SMALL_COLS), lambda h: (0, 0)),
            pl.BlockSpec((1, SMALL_COLS, GLA_DK), lambda h: (h, 0, 0)),
            pl.BlockSpec((1, 1, GLA_DK), lambda h: (h, 0, 0)),
            pl.BlockSpec((1, GLA_DV), lambda h: (0, 0)),
            pl.BlockSpec((None, nb, slab), lambda h: (layer, 0, h)),
        ],
        out_specs=[
            pl.BlockSpec((T, GLA_DV), lambda h: (0, h)),
            pl.BlockSpec((nb, slab), lambda h: (0, h)),
        ],
        out_shape=[
            jax.ShapeDtypeStruct((T, GLA_HEADS * GLA_DV), BF16),
            jax.ShapeDtypeStruct((nb, GLA_HEADS * slab), F32),
        ],
        compiler_params=_cparams(("arbitrary",), 40),
    )(qh, kh, z, z, small, wa_h, ba_h, gn, s0)


def _split3(c):
    def top(x):
        bits = lax.bitcast_convert_type(x, jnp.uint32) & jnp.uint32(0xFFFF0000)
        return lax.bitcast_convert_type(bits, F32)

    hi = top(c)
    mid = top(c - hi)
    return hi, mid, c - hi - mid


FOX_PREP_ROWS = 512
FOX_SUM_ROWS = 16
LOG2E = 1.4426950408889634


def _fox_prompt_body(q_ref, k_ref, v_ref, ck_ref, cq_ref, o_ref, k_scr, vt_scr, s_scr, *, bq, bk):
    i = pl.program_id(2)
    row0 = i * bq
    n_full = lax.div(row0, bk)
    lane = lax.broadcasted_iota(jnp.int32, (1, LANES), 1)
    NB = 3

    @pl.when(i == 0)
    def _():
        ch = min(FOX_PREP_ROWS, k_ref.shape[0])

        def prep(r, carry):
            start = pl.multiple_of(r * ch, ch)
            kc = k_ref[pl.ds(start, ch), :].astype(F32)
            cc = ck_ref[0, pl.ds(start, ch), :]
            vt = v_ref[pl.ds(start, ch), :].astype(F32).T
            ones = jnp.ones((FOX_SUM_ROWS, ch), F32)
            for hh in range(2):
                kh = kc if hh == 0 else pltpu.roll(kc, FOX_HD, axis=1)
                tile = jnp.where(lane < FOX_HD + NB, 1.0, 0.0)
                for n, term in enumerate(_split3(cc[:, hh:hh + 1] * LOG2E)):
                    tile = jnp.where(lane == FOX_HD + NB + n, -term, tile)
                k_scr[hh, pl.ds(start, ch), :] = jnp.where(lane < FOX_HD, kh, tile).astype(BF16)
                vt_scr[hh, :, pl.ds(start, ch)] = jnp.concatenate(
                    [vt[hh * FOX_HD:(hh + 1) * FOX_HD], ones], axis=0).astype(BF16)
            return carry

        lax.fori_loop(0, k_ref.shape[0] // ch, prep, 0)

    qt = (q_ref[...].astype(F32) * (FOX_HD ** -0.5 * LOG2E)).T
    row = lax.broadcasted_iota(jnp.int32, (LANES, 1), 0)
    qts = []
    for hh in range(2):
        tile = jnp.concatenate([qt[hh * FOX_HD:(hh + 1) * FOX_HD], jnp.zeros((LANES - FOX_HD, bq), F32)], axis=0)
        tile = jnp.where((row >= FOX_HD + NB) & (row < FOX_HD + 2 * NB), 1.0, tile)
        for n, term in enumerate(_split3(cq_ref[0, hh:hh + 1, :] * LOG2E)):
            tile = jnp.where(row == FOX_HD + n, term, tile)
        qts.append(tile.astype(BF16))

    def scores(j):
        start = pl.multiple_of(j * bk, bk)
        return tuple(_dot(k_scr[hh, pl.ds(start, bk), :], qts[hh]) for hh in range(2))

    def softmax(j, s, m, masked):
        out = []
        for hh in range(2):
            sh = s[hh]
            if masked:
                key = j * bk + lax.broadcasted_iota(jnp.int32, (bk, bq), 0)
                qry = row0 + lax.broadcasted_iota(jnp.int32, (bk, bq), 1)
                sh = jnp.where(key <= qry, sh, NEG)
            m_new = jnp.maximum(m[hh], jnp.max(sh, axis=0, keepdims=True))
            out.append((m_new, jnp.exp2(sh - m_new).astype(BF16), jnp.exp2(m[hh] - m_new)))
        return tuple(zip(*out))

    def accumulate(j, p, alpha, acc):
        start = pl.multiple_of(j * bk, bk)
        return tuple(alpha[hh] * acc[hh] + _dot(vt_scr[hh, :, pl.ds(start, bk)], p[hh]) for hh in range(2))

    def put_scores(j):
        for hh, sh in enumerate(scores(j)):
            s_scr[hh] = sh

    def step(j, carry):
        m, acc = carry
        m, p, alpha = softmax(j, (s_scr[0], s_scr[1]), m, False)
        put_scores(j + 1)
        return m, accumulate(j, p, alpha, acc)

    two = lambda a: (a, a)
    put_scores(0)
    acc0 = jnp.zeros((FOX_HD + FOX_SUM_ROWS, bq), F32)
    m, acc = lax.fori_loop(0, n_full, step, (two(jnp.full((1, bq), NEG, F32)), two(acc0)))
    m, p, alpha = softmax(n_full, (s_scr[0], s_scr[1]), m, True)
    acc = accumulate(n_full, p, alpha, acc)
    o = jnp.concatenate([a[:FOX_HD] / a[FOX_HD:FOX_HD + 1] for a in acc], axis=0)
    o_ref[...] = o.T.astype(BF16)


def _fox_prompt(z, c_cols, c_rows, B, L, bq, bk):
    nq = L // bq
    T = B * L
    npair = FOX_HEADS // 2
    return pl.pallas_call(
        functools.partial(_fox_prompt_body, bq=bq, bk=bk),
        grid=(B, npair, nq),
        in_specs=[
            pl.BlockSpec((bq, LANES), lambda b, p, i: (b * nq + i, Z_FQ // LANES + p)),
            pl.BlockSpec((L, LANES), lambda b, p, i: (b, Z_FK // LANES + p)),
            pl.BlockSpec((L, LANES), lambda b, p, i: (b, Z_FV // LANES + p)),
            pl.BlockSpec((1, L, 2), lambda b, p, i: (p, b, 0)),
            pl.BlockSpec((1, 2, bq), lambda b, p, i: (p, 0, b * nq + i)),
        ],
        out_specs=pl.BlockSpec((bq, LANES), lambda b, p, i: (b * nq + i, p)),
        out_shape=jax.ShapeDtypeStruct((T, FOX_HEADS * FOX_HD), BF16),
        scratch_shapes=[pltpu.VMEM((2, L, LANES), BF16), pltpu.VMEM((2, FOX_HD + FOX_SUM_ROWS, L), BF16),
                        pltpu.VMEM((2, bk, bq), F32)],
        compiler_params=_cparams(("arbitrary", "arbitrary", "arbitrary"), 56),
    )(z, z, z, c_cols, c_rows)


def _fox_sample_body(pt_ref, q_ref, kn_ref, vn_ref, fft_ref, bf_ref, kt_hbm, vt_hbm, lft_hbm, o_ref, lfo_ref,
                     kbuf, vbuf, lfbuf, k_scr, v_scr, sem, *, layer, n_pages, nt, page, nseq):
    s = pl.program_id(0)
    slot = lax.rem(s, 2)

    def page_copies(step, slot_):
        out = []
        for n in range(nseq * n_pages):
            pid = pt_ref[step * (nseq * n_pages) + n]
            out.append(pltpu.make_async_copy(kt_hbm.at[layer, pid], kbuf.at[slot_, n], sem.at[slot_, 0]))
            out.append(pltpu.make_async_copy(vt_hbm.at[layer, pid], vbuf.at[slot_, n], sem.at[slot_, 1]))
            out.append(pltpu.make_async_copy(lft_hbm.at[layer, pid], lfbuf.at[slot_, n], sem.at[slot_, 2]))
        return out

    def start_all(copies):
        for n, c in enumerate(copies):
            c.start(priority=0 if n % 3 == 2 else 1)

    @pl.when(s == 0)
    def _():
        start_all(page_copies(s, slot))

    @pl.when(s + 1 < pl.num_programs(0))
    def _():
        start_all(page_copies(s + 1, 1 - slot))

    for n, (hbm, buf) in enumerate(((kt_hbm, kbuf), (vt_hbm, vbuf), (lft_hbm, lfbuf))):
        pltpu.make_async_copy(hbm.at[layer, pl.ds(0, nseq * n_pages)], buf.at[slot], sem.at[slot, n]).wait()

    for u in range(nseq):
        pages = [u * n_pages + j for j in range(n_pages)]
        _fox_sample_one(q_ref.at[u], kn_ref.at[u], vn_ref.at[u], fft_ref.at[u], bf_ref,
                        [kbuf.at[slot, n] for n in pages], [vbuf.at[slot, n] for n in pages],
                        [lfbuf.at[slot, n] for n in pages], o_ref.at[u], lfo_ref.at[u], k_scr.at[u], v_scr.at[u],
                        n_pages=n_pages, nt=nt, page=page)


def _fox_sample_one(q_ref, kn_ref, vn_ref, fft_ref, bf_ref, k_refs, v_refs, lf_refs, o_ref, lfo_ref, k_scr, v_scr,
                    *, n_pages, nt, page):
    W = FOX_HEADS * FOX_HD
    H = FOX_HEADS
    q = q_ref[...] * (FOX_HD ** -0.5)
    head_shift = FOX_HD.bit_length() - 1
    hmask = (lax.broadcasted_iota(jnp.int32, (H, W), 1) >> head_shift) == lax.broadcasted_iota(jnp.int32, (H, W), 0)
    qbd = jnp.concatenate([jnp.where(hmask, jnp.broadcast_to(q[t:t + 1], (H, W)), 0.0) for t in range(nt)], axis=0)
    qbd_b = qbd.astype(BF16)

    lane = lax.broadcasted_iota(jnp.int32, (H, LANES), 1)
    carry = jnp.zeros((H, 1), F32)
    ck = []
    for j in range(n_pages):
        c = _lane_prefix_sum(lf_refs[j][...], lane) + carry
        carry = c[:, LANES - 1:LANES]
        ck.append(c)
    lfn = _log_sigmoid(fft_ref[...] + bf_ref[...])
    lfo_ref[...] = lfn
    lane_t = lax.broadcasted_iota(jnp.int32, (H, nt), 1)
    cn = carry
    for j in range(nt):
        cn = cn + jnp.where(lane_t >= j, lfn[:, j:j + 1], 0.0)
    cq = jnp.concatenate([cn[:, t:t + 1] for t in range(nt)], axis=0)

    for j in range(n_pages):
        k_scr[:, j * page:(j + 1) * page] = k_refs[j][...].reshape(W, page).astype(BF16)
        v_scr[:, j * page:(j + 1) * page] = v_refs[j][...].reshape(W, page).astype(BF16)
    ck_all = jnp.concatenate(ck, axis=1)
    s_past = _dot(qbd_b, k_scr[...]) + (cq - jnp.concatenate([ck_all] * nt, axis=0))

    kn = kn_ref[...]
    vn = vn_ref[...]
    trow = lax.broadcasted_iota(jnp.int32, (nt * H, 1), 0) >> (H.bit_length() - 1)
    s_new = []
    for tp in range(nt):
        ckn = jnp.concatenate([cn[:, tp:tp + 1]] * nt, axis=0)
        sn = jnp.sum(qbd * kn[tp:tp + 1], axis=-1, keepdims=True) + (cq - ckn)
        s_new.append(jnp.where(trow >= tp, sn, NEG))
    m = jnp.max(s_past, axis=-1, keepdims=True)
    for sn in s_new:
        m = jnp.maximum(m, sn)
    p_past = jnp.exp(s_past - m)
    den = jnp.sum(p_past, axis=-1, keepdims=True)
    out = _dot_nt(p_past.astype(BF16), v_scr[...])
    for tp in range(nt):
        pn = jnp.exp(s_new[tp] - m)
        den = den + pn
        out = out + pn * vn[tp:tp + 1]
    out = out / den
    for t in range(nt):
        o_ref[t:t + 1, :] = jnp.sum(jnp.where(hmask, out[t * H:(t + 1) * H], 0.0), axis=0, keepdims=True)


FOX_SEQS_PER_STEP = 2


def _fox_sample(pt_flat, q_s, kn_s, vn_s, fft_s, bf, cache_kt, cache_vt, cache_lft, layer, nb, nt, n_pages, page):
    W = FOX_HEADS * FOX_HD
    nseq = FOX_SEQS_PER_STEP if nb % FOX_SEQS_PER_STEP == 0 else 1
    assert page == LANES and cache_kt.shape[1] >= nseq * n_pages

    seq3 = lambda s, pt: (s, 0, 0)
    in_specs = [
        pl.BlockSpec((nseq, nt, W), seq3),
        pl.BlockSpec((nseq, nt, W), seq3),
        pl.BlockSpec((nseq, nt, W), seq3),
        pl.BlockSpec((nseq, FOX_HEADS, nt), seq3),
        pl.BlockSpec((FOX_HEADS, 1), lambda s, pt: (0, 0)),
        pl.BlockSpec(memory_space=pl.ANY),
        pl.BlockSpec(memory_space=pl.ANY),
        pl.BlockSpec(memory_space=pl.ANY),
    ]
    grid_spec = pltpu.PrefetchScalarGridSpec(
        num_scalar_prefetch=1,
        grid=(nb // nseq,),
        in_specs=in_specs,
        out_specs=[pl.BlockSpec((nseq, nt, W), seq3), pl.BlockSpec((nseq, FOX_HEADS, nt), seq3)],
        scratch_shapes=[pltpu.VMEM((2, nseq * n_pages, FOX_HEADS, FOX_HD, page), F32),
                        pltpu.VMEM((2, nseq * n_pages, FOX_HEADS, FOX_HD, page), F32),
                        pltpu.VMEM((2, nseq * n_pages, FOX_HEADS, page), F32),
                        pltpu.VMEM((nseq, W, n_pages * page), BF16), pltpu.VMEM((nseq, W, n_pages * page), BF16),
                        pltpu.SemaphoreType.DMA((2, 3))],
    )
    return pl.pallas_call(
        functools.partial(_fox_sample_body, layer=layer, n_pages=n_pages, nt=nt, page=page, nseq=nseq),
        grid_spec=grid_spec,
        out_shape=[jax.ShapeDtypeStruct((nb, nt, W), F32), jax.ShapeDtypeStruct((nb, FOX_HEADS, nt), F32)],
        compiler_params=_cparams(("arbitrary",), 56),
    )(pt_flat, q_s, kn_s, vn_s, fft_s, bf, cache_kt, cache_vt, cache_lft)


def _causal_conv(ext_scr, u, cw_ref, H, shift):
    bm = u.shape[0]
    ext_scr[H:H + bm, :] = u
    out = cw_ref[CONV_K - 1:CONV_K, :] * u
    for i in range(1, CONV_K):
        lo = H - i * shift
        out = out + cw_ref[CONV_K - 1 - i:CONV_K - i, :] * ext_scr[lo:lo + bm, :]
    return out


def _mix_body(*refs, H, shift, tiles_per_seq, has_state):
    if has_state:
        h_ref, gates_ref, sc_ref, yg_ref, yf_ref, cw_ref, wb_ref, wo_ref, halo_ref, hout_ref, tail_ref, ext_scr = refs
    else:
        h_ref, gates_ref, sc_ref, yg_ref, yf_ref, cw_ref, wb_ref, wo_ref, hout_ref, tail_ref, ext_scr = refs
    bm = h_ref.shape[0]
    if has_state:
        ext_scr[0:H, :] = halo_ref[...]
    else:
        seq_start = pl.program_id(0) % tiles_per_seq == 0

        @pl.when(seq_start)
        def _():
            ext_scr[0:H, :] = jnp.zeros((H, SC_WIDTH), F32)

        @pl.when(jnp.logical_not(seq_start))
        def _():
            ext_scr[0:H, :] = ext_scr[bm:bm + H, :]

    sc = sc_ref[...].astype(F32)
    u = sc[:, SC_WIDTH:2 * SC_WIDTH] * sc[:, 2 * SC_WIDTH:]
    y_sc = (sc[:, :SC_WIDTH] * _causal_conv(ext_scr, u, cw_ref, H, shift)).astype(BF16)
    tail_ref[0] = ext_scr[bm:bm + H, :]

    acc = jnp.zeros((bm, D_MODEL), F32)
    for b, y in enumerate((yg_ref[...], yf_ref[...], y_sc)):
        g = jax.nn.sigmoid(gates_ref[:, b * D_MODEL:(b + 1) * D_MODEL].astype(F32))
        acc = acc + g * _dot(y, wb_ref[b])
    hout_ref[...] = h_ref[...] + _dot(acc.astype(BF16), wo_ref[...])


def _mix(h, z, yg, yf, cw, wb, wo, layer, halo, bm, H, shift, tiles_per_seq):
    T = h.shape[0]
    has_state = halo is not None
    in_specs = [
        pl.BlockSpec((bm, D_MODEL), lambda i: (i, 0)),
        pl.BlockSpec((bm, N_BRANCH * D_MODEL), lambda i: (i, 0)),
        pl.BlockSpec((bm, 3 * SC_WIDTH), lambda i: (i, Z_SC // (3 * SC_WIDTH))),
        pl.BlockSpec((bm, SC_WIDTH), lambda i: (i, 0)),
        pl.BlockSpec((bm, SC_WIDTH), lambda i: (i, 0)),
        pl.BlockSpec((CONV_K, SC_WIDTH), lambda i: (0, 0)),
        pl.BlockSpec((None, N_BRANCH, SC_WIDTH, D_MODEL), lambda i: (layer, 0, 0, 0)),
        pl.BlockSpec((None, D_MODEL, D_MODEL), lambda i: (layer, 0, 0)),
    ]
    args = [h, z, z, yg, yf, cw, wb, wo]
    if has_state:
        in_specs.append(pl.BlockSpec((H, SC_WIDTH), lambda i: (0, 0)))
        args.append(halo)
    return pl.pallas_call(
        functools.partial(_mix_body, H=H, shift=shift, tiles_per_seq=tiles_per_seq, has_state=has_state),
        grid=(T // bm,),
        in_specs=in_specs,
        out_specs=[pl.BlockSpec((bm, D_MODEL), lambda i: (i, 0)),
                   pl.BlockSpec((1, H, SC_WIDTH), lambda i: (i, 0, 0))],
        out_shape=[jax.ShapeDtypeStruct((T, D_MODEL), F32),
                   jax.ShapeDtypeStruct((T // bm, H, SC_WIDTH), F32)],
        scratch_shapes=[pltpu.VMEM((H + bm, SC_WIDTH), F32)],
        compiler_params=_cparams(("arbitrary",), 48),
    )(*args)


FFN_CK = 1408


def _ffn_body(*refs, H, shift, tiles_per_seq, has_state, final_norm):
    refs = list(refs)
    h_ref, g_ref, wa_ref, wg_ref, cw_ref, cb_ref, wd_ref = refs[:7]
    refs = refs[7:]
    halo_ref = refs.pop(0) if has_state else None
    fg_ref = refs.pop(0) if final_norm else None
    out_ref, tail_ref, xn_scr, acc_scr, ext_scr, halo_scr = refs
    i = pl.program_id(0)
    c = pl.program_id(1)
    bm = h_ref.shape[0]

    @pl.when(c == 0)
    def _():
        xn_scr[...] = _rms(h_ref[...], g_ref[...]).astype(BF16)
        acc_scr[...] = jnp.zeros_like(acc_scr)

    if has_state:
        ext_scr[0:H, :] = halo_ref[...]
    else:
        seq_start = i % tiles_per_seq == 0

        @pl.when(seq_start)
        def _():
            ext_scr[0:H, :] = jnp.zeros((H, ext_scr.shape[1]), F32)

        @pl.when(jnp.logical_not(seq_start))
        def _():
            ext_scr[0:H, :] = halo_scr[c]

    xn = xn_scr[...]
    a = _dot(xn, wa_ref[...])
    gate = _dot(xn, wg_ref[...])
    ac = _causal_conv(ext_scr, a, cw_ref, H, shift) + cb_ref[...]
    tail = ext_scr[bm:bm + H, :]
    tail_ref[0] = tail
    if not has_state:
        halo_scr[c] = tail
    act = 0.5 * ac * (1.0 + lax.erf(ac * (2.0 ** -0.5))) * gate
    acc_scr[...] += _dot(act.astype(BF16), wd_ref[...])

    @pl.when(c == pl.num_programs(1) - 1)
    def _():
        hn = h_ref[...] + acc_scr[...]
        out_ref[...] = _rms(hn, fg_ref[...]) if final_norm else hn


def _ffn(h, g, wu, cw, cb, wd, layer, halo, fg, bm, H, shift, tiles_per_seq):
    T = h.shape[0]
    ck = FFN_CK
    nc = D_FF // ck
    has_state = halo is not None
    final_norm = fg is not None
    in_specs = [
        pl.BlockSpec((bm, D_MODEL), lambda i, c: (i, 0)),
        pl.BlockSpec((1, D_MODEL), lambda i, c: (0, 0)),
        pl.BlockSpec((None, D_MODEL, ck), lambda i, c: (layer, 0, c)),
        pl.BlockSpec((None, D_MODEL, ck), lambda i, c: (layer, 0, nc + c)),
        pl.BlockSpec((CONV_K, ck), lambda i, c: (0, c)),
        pl.BlockSpec((1, ck), lambda i, c: (0, c)),
        pl.BlockSpec((None, ck, D_MODEL), lambda i, c: (layer, c, 0)),
    ]
    args = [h, g, wu, wu, cw, cb, wd]
    if has_state:
        in_specs.append(pl.BlockSpec((H, ck), lambda i, c: (0, c)))
        args.append(halo)
    if final_norm:
        in_specs.append(pl.BlockSpec((1, D_MODEL), lambda i, c: (0, 0)))
        args.append(fg)
    return pl.pallas_call(
        functools.partial(_ffn_body, H=H, shift=shift, tiles_per_seq=tiles_per_seq, has_state=has_state,
                          final_norm=final_norm),
        grid=(T // bm, nc),
        in_specs=in_specs,
        out_specs=[pl.BlockSpec((bm, D_MODEL), lambda i, c: (i, 0)),
                   pl.BlockSpec((1, H, ck), lambda i, c: (i, 0, c))],
        out_shape=[jax.ShapeDtypeStruct((T, D_MODEL), F32),
                   jax.ShapeDtypeStruct((T // bm, H, D_FF), F32)],
        scratch_shapes=[pltpu.VMEM((bm, D_MODEL), BF16), pltpu.VMEM((bm, D_MODEL), F32),
                        pltpu.VMEM((H + bm, ck), F32), pltpu.VMEM((nc, H, ck), F32)],
        compiler_params=_cparams(("arbitrary", "arbitrary"), 56),
    )(*args)


def _pack_in_proj(w_in):
    sizes = (GLA_HEADS * GLA_DK, GLA_HEADS * GLA_DK, GLA_HEADS * GLA_DV, GLA_LR, GLA_HEADS * GLA_DV,
             FOX_HEADS * FOX_HD, FOX_HEADS * FOX_HD, FOX_HEADS * FOX_HD, FOX_HEADS,
             SC_WIDTH, SC_WIDTH, SC_WIDTH, N_BRANCH * D_MODEL)
    offs = np.concatenate([[0], np.cumsum(sizes)])
    part = lambda n: w_in[:, :, offs[n]:offs[n + 1]]
    gq, gk, gv, glr, gg, fq, fk, fv, ff, scb, scc, sch, gates = [part(n) for n in range(len(sizes))]
    w_main = jnp.concatenate([gates, gq, gk, gv, gg, fq, fk, fv, scb, scc, sch], axis=-1).astype(BF16)
    pad = jnp.zeros(w_in.shape[:2] + (SMALL_COLS - GLA_LR - FOX_HEADS,), w_in.dtype)
    w_small = jnp.concatenate([glr, ff, pad], axis=-1).astype(BF16)
    w_fft = jnp.swapaxes(ff, 1, 2).astype(BF16)
    return w_main, w_small, w_fft


def kernel(x_prompt, x_sample, cache_fox_k, cache_fox_v, cache_fox_logf, state_gla, state_sconv, state_ffn_conv,
           page_table, norm1_g, w_in, w_alpha2, b_alpha, b_forget, gla_norm_g, sconv_w, w_branch, w_out, norm2_g,
           w_up, ffn_conv_w, ffn_conv_b, w_down, final_norm_g):
    depth = w_in.shape[0]
    B, L, _ = x_prompt.shape
    nb, nt, _ = x_sample.shape
    n_pool, page = cache_fox_k.shape[1], cache_fox_k.shape[2]
    n_pages = page_table.shape[1]
    W = FOX_HEADS * FOX_HD
    Tp = B * L
    Ts = nb * nt
    bm_p = min(512, L)
    bm_in = min(1024, L)
    fox_bq, fox_bk = min(512, L), min(512, L)
    H_p, H_s = SUBLANES, (CONV_K - 1) * nb

    w_main, w_small, w_fft = _pack_in_proj(w_in)
    wa_pad = jnp.concatenate(
        [w_alpha2, jnp.zeros((depth, SMALL_COLS - GLA_LR, GLA_HEADS * GLA_DK), w_alpha2.dtype)], axis=1).astype(BF16)
    wa_heads = wa_pad.reshape(depth, SMALL_COLS, GLA_HEADS, GLA_DK).transpose(0, 2, 1, 3)
    ba_heads = b_alpha.reshape(depth, GLA_HEADS, 1, GLA_DK)
    wb = w_branch.astype(BF16)
    wo = w_out.astype(BF16)
    wu = w_up.astype(BF16)
    wd = w_down.astype(BF16)
    gla_states = state_gla.reshape(depth, nb, -1)

    cache_kt = jnp.transpose(cache_fox_k, (0, 1, 3, 4, 2))
    cache_vt = jnp.transpose(cache_fox_v, (0, 1, 3, 4, 2))
    cache_lft = jnp.swapaxes(cache_fox_logf, 2, 3)
    pt_flat = page_table.reshape(-1).astype(jnp.int32)

    def to_seq(a):
        return jnp.swapaxes(a.reshape((a.shape[0] // nb, nb) + a.shape[1:]), 0, 1)

    def to_time(a):
        return jnp.swapaxes(a, 0, 1).reshape((a.shape[0] * a.shape[1],) + a.shape[2:])

    hp = x_prompt.reshape(Tp, D_MODEL)
    hs = to_time(x_sample)
    outs_p, outs_s = [], []
    for l in range(depth):
        g1 = norm1_g[l].reshape(1, D_MODEL)
        g2 = norm2_g[l].reshape(1, D_MODEL)
        bf = b_forget[l].reshape(FOX_HEADS, 1)
        ba = b_alpha[l].reshape(1, -1)
        gn = gla_norm_g[l].reshape(1, GLA_DV)
        cb = ffn_conv_b[l].reshape(1, D_FF)
        fg = final_norm_g.reshape(1, D_MODEL) if l == depth - 1 else None

        z, fkt, fvt, small, fft = _in_proj(hp, g1, w_main, w_small, w_fft, l, bm_in, (B, L))
        yg, st, logf_t, c_t = _gla_prompt(z, small, fft, wa_pad[l], ba, bf, gn, B, L)
        c_rows = c_t.reshape(FOX_HEADS // 2, 2, Tp)
        yf = _fox_prompt(z, jnp.swapaxes(c_rows, 1, 2), c_rows, B, L, fox_bq, fox_bk)
        hp, sc_tail = _mix(hp, z, yg, yf, sconv_w[l], wb, wo, l, None, bm_p, H_p, 1, L // bm_p)
        hp, ffn_tail = _ffn(hp, g2, wu, ffn_conv_w[l], cb, wd, l, None, fg, bm_p, H_p, 1, L // bm_p)
        last = lambda t: t.reshape(B, L // bm_p, H_p, -1)[:, -1, H_p - (CONV_K - 1):, :]
        p_gla = st.reshape(B, GLA_HEADS // 2, GLA_DV, 2, GLA_DK).transpose(0, 1, 3, 4, 2)
        heads_last = lambda a: a.reshape(B, FOX_HEADS, FOX_HD, L).transpose(0, 3, 1, 2)
        outs_p.append((heads_last(fkt), heads_last(fvt), logf_t.T.reshape(B, L, FOX_HEADS), p_gla.reshape(B, GLA_HEADS, GLA_DK, GLA_DV),
                       last(sc_tail), last(ffn_tail)))

        z, fk, fv, small, fft = _in_proj(hs, g1, w_main, w_small, w_fft, l, Ts)
        qk_h = z[:, Z_GQK:Z_GV].reshape(Ts, 2, GLA_HEADS, GLA_DK).transpose(1, 2, 0, 3)
        yg, s1 = _gla_sample(qk_h[0], qk_h[1], z, small, wa_heads[l], ba_heads[l], gn,
                             gla_states, l, nb, nt)
        yf_s, lfn = _fox_sample(pt_flat, to_seq(z[:, Z_FQ:Z_FK].astype(F32)), to_seq(fk), to_seq(fv),
                                jnp.swapaxes(fft.reshape(FOX_HEADS, nt, nb), 0, 2).swapaxes(1, 2), bf,
                                cache_kt, cache_vt, cache_lft, l, nb, nt, n_pages, page)
        yf = to_time(yf_s).astype(BF16)
        halo_sc = to_time(state_sconv[l])
        halo_ffn = to_time(state_ffn_conv[l])
        hs, sc_tail = _mix(hs, z, yg, yf, sconv_w[l], wb, wo, l, halo_sc, Ts, H_s, nb, 1)
        hs, ffn_tail = _ffn(hs, g2, wu, ffn_conv_w[l], cb, wd, l, halo_ffn, fg, Ts, H_s, nb, 1)
        outs_s.append((to_seq(fk).reshape(nb, nt, FOX_HEADS, FOX_HD), to_seq(fv).reshape(nb, nt, FOX_HEADS, FOX_HD),
                       jnp.swapaxes(lfn, 1, 2), s1.reshape(nb, GLA_HEADS, GLA_DK, GLA_DV),
                       to_seq(sc_tail[0]), to_seq(ffn_tail[0])))

    p_states = [jnp.stack(t) for t in zip(*outs_p)]
    s_states = [jnp.stack(t) for t in zip(*outs_s)]
    y_prompt = hp.reshape(B, L, D_MODEL)
    y_sample = to_seq(hs)
    return (y_prompt, y_sample, *p_states, *s_states)
```
